```python
import math
import numpy as np
import jax
import jax.numpy as jnp
from jax import lax

D_MODEL = 4096
BATCH = 2
SEQ = 8192
DEPTH = 4

GRID_W = 64
CTX_LEN = 256
HEAD_DIM = 128
NA_HEADS = 8
NA_KH = 8
NA_KW = 16
DIFF_HEADS = 8
DIFF_QK_DIM = 64
DIFF_V_DIM = 128
Q_BLOCK = 128
SWA_Q_HEADS = 8
SWA_KV_HEADS = 2
SWA_WINDOW = 128
SWA_BLOCK = 128
LRU_WIDTH = 1024
LRU_BLOCKS = 8
LRU_CONV = 4
LRU_CONV_LEFT = 2
LRU_C = 8.0
N_EXPERTS = 16
EXPERT_FF = 256
CAPACITY_FACTOR = 2
ROPE_THETA = 10000.0
NEG_INF = -1e30

NA_W = NA_HEADS * HEAD_DIM
DIFF_W = DIFF_HEADS * 2 * DIFF_QK_DIM
DIFF_OUT = DIFF_HEADS * DIFF_V_DIM
SWA_Q_W = SWA_Q_HEADS * HEAD_DIM
SWA_KV_W = SWA_KV_HEADS * HEAD_DIM
D_MIX = NA_W + DIFF_OUT + SWA_Q_W + LRU_WIDTH
IN_SPLITS = (NA_W, NA_W, NA_W, DIFF_W, DIFF_W, DIFF_OUT, SWA_Q_W, SWA_KV_W, SWA_KV_W, LRU_WIDTH, LRU_WIDTH)
D_IN = sum(IN_SPLITS)
SPLIT_OFFSETS = tuple(int(o) for o in np.cumsum(IN_SPLITS)[:-1])

kernel_name = "hybrid_diffusion_parallel_heads_ec_moe"


def rmsnorm(x, g, eps=1e-6):
    xf = x.astype(jnp.float32)
    y = xf * lax.rsqrt(jnp.mean(xf * xf, axis=-1, keepdims=True) + eps)
    return (y * g.astype(jnp.float32)).astype(x.dtype)


def modulate(x, shift, scale):
    return x * (1 + scale) + shift


def split_heads(t, h):
    b, n, _ = t.shape
    return t.reshape(b, n, h, -1).transpose(0, 2, 1, 3)


def split_diff_heads(t):
    b, n, _ = t.shape
    return t.reshape(b, n, DIFF_HEADS, 2, DIFF_QK_DIM).transpose(0, 2, 3, 1, 4)


def merge_heads(t):
    b, h, n, d = t.shape
    return t.transpose(0, 2, 1, 3).reshape(b, n, h * d)


def softmax_f32(s):
    return jax.nn.softmax(s.astype(jnp.float32), axis=-1)


def axial_rope_angles(n, dh):
    t = jnp.arange(n)
    pos = jnp.stack([t // GRID_W, t % GRID_W], axis=-1).astype(jnp.float32)
    quarter = dh // 4
    inv_freq = ROPE_THETA ** (-jnp.arange(quarter, dtype=jnp.float32) / quarter)
    return pos[:, :, None] * inv_freq


def apply_axial_rope(x, ang):
    quarter = x.shape[-1] // 4
    xs = x.reshape(x.shape[:-1] + (2, 2, quarter))
    x0, x1 = xs[..., 0, :], xs[..., 1, :]
    cos, sin = jnp.cos(ang).astype(x.dtype), jnp.sin(ang).astype(x.dtype)
    out = jnp.stack([x0 * cos - x1 * sin, x0 * sin + x1 * cos], axis=-2)
    return out.reshape(x.shape)


def dense_attention(q, k, v):
    s = jnp.einsum('bhqd,bhkd->bhqk', q, k) * (q.shape[-1] ** -0.5)
    return jnp.einsum('bhqk,bhkd->bhqd', softmax_f32(s).astype(v.dtype), v)


def neighborhood_attention(q, k, v, kc, vc, rpb):
    b, h, n, d = q.shape
    rows = n // GRID_W
    kh = min(NA_KH, rows)
    scale = d ** -0.5
    cols = np.arange(GRID_W)
    col_start = np.clip(cols - NA_KW // 2, 0, GRID_W - NA_KW)
    col_idx = col_start[:, None] + np.arange(NA_KW)
    dx = col_idx - cols[:, None] + (NA_KW - 1)
    kg = k.reshape(b, h, rows, GRID_W, d)
    vg = v.reshape(b, h, rows, GRID_W, d)
    q_rows = jnp.moveaxis(q.reshape(b, h, rows, GRID_W, d), 2, 0)

    def one_row(args):
        r, qr = args
        rs = jnp.clip(r - kh // 2, 0, rows - kh)
        kn = lax.dynamic_slice_in_dim(kg, rs, kh, axis=2)[:, :, :, col_idx]
        vn = lax.dynamic_slice_in_dim(vg, rs, kh, axis=2)[:, :, :, col_idx]
        dy = rs + jnp.arange(kh) - r + (NA_KH - 1)
        bias = rpb[:, dy][:, :, dx]
        s_n = jnp.einsum('bhwd,bhrwkd->bhwrk', qr, kn) * scale + jnp.transpose(bias, (0, 2, 1, 3))
        s_c = jnp.einsum('bhwd,bhld->bhwl', qr, kc) * scale
        p = softmax_f32(jnp.concatenate([s_n.reshape(b, h, GRID_W, kh * NA_KW), s_c], axis=-1)).astype(v.dtype)
        p_n = p[..., :kh * NA_KW].reshape(b, h, GRID_W, kh, NA_KW)
        p_c = p[..., kh * NA_KW:]
        return jnp.einsum('bhwrk,bhrwkd->bhwd', p_n, vn) + jnp.einsum('bhwl,bhld->bhwd', p_c, vc)

    o = lax.map(one_row, (jnp.arange(rows), q_rows))
    return jnp.moveaxis(o, 0, 2).reshape(b, h, n, d)


def diff_attend(q, k, v, lam):
    s = jnp.einsum('bhcqd,bhckd->bhcqk', q, k).astype(jnp.float32) * (q.shape[-1] ** -0.5)
    p = jax.nn.softmax(s, axis=-1)
    w = (p[:, :, 0] - lam * p[:, :, 1]).astype(v.dtype)
    return jnp.einsum('bhqk,bhkd->bhqd', w, v)


def diff_attention_blocks(q, k, v, kc, vc, lam):
    b, h, _, n, dq = q.shape
    nb = n // Q_BLOCK
    k_all = jnp.concatenate([kc, k], axis=3)
    v_all = jnp.concatenate([vc, v], axis=2)
    q_blocks = jnp.moveaxis(q.reshape(b, h, 2, nb, Q_BLOCK, dq), 3, 0)
    o = lax.map(lambda qb: diff_attend(qb, k_all, v_all, lam), q_blocks)
    return jnp.moveaxis(o, 0, 2).reshape(b, h, n, v.shape[-1])


def sliding_window_attention(q, k, v, kc, vc, sink):
    b, hq, n, d = q.shape
    hkv = k.shape[1]
    g = hq // hkv
    nb = n // SWA_BLOCK
    w3 = 3 * SWA_BLOCK
    lc = kc.shape[2]
    scale = d ** -0.5
    qb = q.reshape(b, hkv, g, nb, SWA_BLOCK, d)

    def band(t):
        tp = jnp.pad(t, ((0, 0), (0, 0), (SWA_BLOCK, SWA_BLOCK), (0, 0))).reshape(b, hkv, nb + 2, SWA_BLOCK, d)
        return jnp.concatenate([tp[:, :, :-2], tp[:, :, 1:-1], tp[:, :, 2:]], axis=3)

    kb, vb = band(k), band(v)
    qi = jnp.arange(SWA_BLOCK)[:, None]
    kj = jnp.arange(w3)[None, :]
    near = jnp.abs(kj - SWA_BLOCK - qi) <= SWA_WINDOW
    kpos = (jnp.arange(nb)[:, None] - 1) * SWA_BLOCK + jnp.arange(w3)[None, :]
    valid = (kpos >= 0) & (kpos < n)
    mask = near[None] & valid[:, None, :]
    s_w = jnp.where(mask, jnp.einsum('bkgnqd,bknjd->bkgnqj', qb, kb).astype(jnp.float32) * scale, NEG_INF)
    s_c = jnp.einsum('bkgnqd,bkld->bkgnql', qb, kc).astype(jnp.float32) * scale
    s_sink = jnp.broadcast_to(sink.astype(jnp.float32).reshape(hkv, g, 1, 1, 1), s_c.shape[:-1] + (1,))
    p = softmax_f32(jnp.concatenate([s_w, s_c, s_sink], axis=-1)).astype(v.dtype)
    o = jnp.einsum('bkgnqj,bknjd->bkgnqd', p[..., :w3], vb) + jnp.einsum('bkgnql,bkld->bkgnqd', p[..., w3:w3 + lc], vc)
    return o.reshape(b, hq, n, d)


def ctx_attention_with_sink(q, k, v, sink):
    b, hq, l, d = q.shape
    hkv = k.shape[1]
    g = hq // hkv
    qg = q.reshape(b, hkv, g, l, d)
    s = jnp.einsum('bkgqd,bkld->bkgql', qg, k).astype(jnp.float32) * (d ** -0.5)
    s_sink = jnp.broadcast_to(sink.astype(jnp.float32).reshape(hkv, g, 1, 1), s.shape[:-1] + (1,))
    p = softmax_f32(jnp.concatenate([s, s_sink], axis=-1)).astype(v.dtype)
    return jnp.einsum('bkgql,bkld->bkgqd', p[..., :-1], v).reshape(b, hq, l, d)


def centred_depthwise_conv(x, w, bias):
    n = x.shape[1]
    xp = jnp.pad(x, ((0, 0), (LRU_CONV_LEFT, LRU_CONV - 1 - LRU_CONV_LEFT), (0, 0)))
    out = bias
    for i in range(LRU_CONV):
        out = out + xp[:, i:i + n] * w[i]
    return out


def block_diag_linear(x, w, bias):
    xb = x.reshape(x.shape[:-1] + (w.shape[0], w.shape[1]))
    return jnp.einsum('bnhi,hij->bnhj', xb, w).reshape(x.shape) + bias


def rglru_coeffs(x, gate_w, gate_b, lam):
    r = jax.nn.sigmoid(block_diag_linear(x, gate_w[0], gate_b[0])).astype(jnp.float32)
    i = jax.nn.sigmoid(block_diag_linear(x, gate_w[1], gate_b[1]))
    log_a = LRU_C * r * jax.nn.log_sigmoid(lam.astype(jnp.float32))
    a = jnp.exp(log_a)
    bterm = jnp.sqrt(-jnp.expm1(2.0 * log_a)) * (i * x).astype(jnp.float32)
    return a, bterm


def linear_scan(a, bterm, h0, reverse):
    def combine(left, right):
        return left[0] * right[0], right[0] * left[1] + right[1]
    a_cum, h = lax.associative_scan(combine, (a, bterm), axis=1, reverse=reverse)
    return h + a_cum * h0[:, None, :]


def rglru_bidirectional(xr, gr, xrc, grc, conv_w, conv_b, gate_w, gate_b, lam, need_ctx):
    xl = centred_depthwise_conv(xr, conv_w, conv_b)
    xc = centred_depthwise_conv(xrc, conv_w, conv_b)
    h0 = jnp.zeros((xc.shape[0], xc.shape[2]), jnp.float32)

    def direction(d, reverse):
        a_c, b_c = rglru_coeffs(xc, gate_w[d], gate_b[d], lam[d])
        h_c = linear_scan(a_c, b_c, h0, reverse)
        h_last = h_c[:, 0] if reverse else h_c[:, -1]
        a_l, b_l = rglru_coeffs(xl, gate_w[d], gate_b[d], lam[d])
        return linear_scan(a_l, b_l, h_last, reverse), h_c

    hl_f, hc_f = direction(0, False)
    hl_b, hc_b = direction(1, True)
    y_l = (hl_f + hl_b).astype(xr.dtype) * jax.nn.gelu(gr)
    y_c = (hc_f + hc_b).astype(xr.dtype) * jax.nn.gelu(grc) if need_ctx else None
    return y_l, y_c


def token_mixers(u, uc, ang_diff, ang_swa, layer, qk_g_na, na_rpb, qk_g_diff, diff_lambda, diff_subln_g,
                 qk_g_swa, swa_sink, lru_conv_w, lru_conv_b, lru_gate_w, lru_gate_b, lru_lam, need_ctx):
    qa, ka, va, qd, kd, vd, qs, ks, vs, xr, gr = jnp.split(u, SPLIT_OFFSETS, axis=-1)
    qac, kac, vac, qdc, kdc, vdc, qsc, ksc, vsc, xrc, grc = jnp.split(uc, SPLIT_OFFSETS, axis=-1)

    na_q = lambda t: rmsnorm(split_heads(t, NA_HEADS), qk_g_na[0])
    na_k = lambda t: rmsnorm(split_heads(t, NA_HEADS), qk_g_na[1])
    k_ctx_a, v_ctx_a = na_k(kac), split_heads(vac, NA_HEADS)
    o_a = neighborhood_attention(na_q(qa), na_k(ka), split_heads(va, NA_HEADS), k_ctx_a, v_ctx_a, na_rpb)

    lambda_init = 0.8 - 0.6 * math.exp(-0.3 * layer)
    dl = diff_lambda.astype(jnp.float32)
    lam = jnp.exp(jnp.sum(dl[0] * dl[1])) - jnp.exp(jnp.sum(dl[2] * dl[3])) + lambda_init
    d_q = lambda t: rmsnorm(split_diff_heads(t), qk_g_diff[0])
    d_k = lambda t: rmsnorm(split_diff_heads(t), qk_g_diff[1])
    post = lambda o: rmsnorm(o, diff_subln_g) * (1.0 - lambda_init)
    k_ctx_d, v_ctx_d = d_k(kdc), split_heads(vdc, DIFF_HEADS)
    o_d = diff_attention_blocks(apply_axial_rope(d_q(qd), ang_diff), apply_axial_rope(d_k(kd), ang_diff),
                                split_heads(vd, DIFF_HEADS), k_ctx_d, v_ctx_d, lam)

    s_q = lambda t: rmsnorm(split_heads(t, SWA_Q_HEADS), qk_g_swa[0])
    s_k = lambda t: rmsnorm(split_heads(t, SWA_KV_HEADS), qk_g_swa[1])
    k_ctx_s, v_ctx_s = s_k(ksc), split_heads(vsc, SWA_KV_HEADS)
    o_s = sliding_window_attention(apply_axial_rope(s_q(qs), ang_swa), apply_axial_rope(s_k(ks), ang_swa),
                                   split_heads(vs, SWA_KV_HEADS), k_ctx_s, v_ctx_s, swa_sink)

    y_r, y_rc = rglru_bidirectional(xr, gr, xrc, grc, lru_conv_w, lru_conv_b, lru_gate_w, lru_gate_b, lru_lam, need_ctx)

    o_lat = jnp.concatenate([merge_heads(o_a), merge_heads(post(o_d)), merge_heads(o_s), y_r], axis=-1)
    if not need_ctx:
        return o_lat, None
    o_ctx = jnp.concatenate([
        merge_heads(dense_attention(na_q(qac), k_ctx_a, v_ctx_a)),
        merge_heads(post(diff_attend(d_q(qdc), k_ctx_d, v_ctx_d, lam))),
        merge_heads(ctx_attention_with_sink(s_q(qsc), k_ctx_s, v_ctx_s, swa_sink)),
        y_rc], axis=-1)
    return o_lat, o_ctx


def expert_choice_ffn(x, w_router, w1, w3, w2):
    b, n, d = x.shape
    cap = CAPACITY_FACTOR * n // N_EXPERTS
    aff = jax.nn.softmax((x @ w_router).astype(jnp.float32), axis=-1)
    gate, idx = lax.top_k(jnp.swapaxes(aff, 1, 2), cap)
    xs = jax.vmap(lambda xb, ib: xb[ib])(x, idx)
    hdn = jax.nn.silu(jnp.einsum('becd,edf->becf', xs, w1)) * jnp.einsum('becd,edf->becf', xs, w3)
    y = jnp.einsum('becf,efd->becd', hdn, w2) * gate[..., None].astype(x.dtype)
    return jax.vmap(lambda ib, yb: jnp.zeros((n, d), x.dtype).at[ib.reshape(-1)].add(yb.reshape(-1, d)))(idx, y)


def setup_inputs(seed: int = 0) -> dict:
    key = jax.random.key(seed)
    k = jax.random.split(key, 25)
    nrm = lambda kk, shape, s: jax.random.normal(kk, shape, jnp.float32) * s
    gain = lambda kk, shape: 1.0 + 0.1 * jax.random.normal(kk, shape, jnp.float32)
    bs = LRU_WIDTH // LRU_BLOCKS
    u = jax.random.uniform(k[20], (DEPTH, 2, LRU_WIDTH), jnp.float32, 0.9, 0.999)
    s = u ** (1.0 / LRU_C)
    return {
        "x": nrm(k[0], (BATCH, SEQ, D_MODEL), 1.0),
        "c": nrm(k[1], (BATCH, D_MODEL), 1.0),
        "ctx": nrm(k[2], (BATCH, CTX_LEN, D_MODEL), 1.0),
        "c_ctx": nrm(k[3], (D_MODEL,), 1.0),
        "w_ada": nrm(k[4], (DEPTH, D_MODEL, 6 * D_MODEL), 0.5 * D_MODEL ** -0.5),
        "b_ada": nrm(k[5], (DEPTH, 6 * D_MODEL), 0.01),
        "norm_g": gain(k[6], (DEPTH, 2, D_MODEL)),
        "w_in": nrm(k[7], (DEPTH, D_MODEL, D_IN), D_MODEL ** -0.5),
        "w_out": nrm(k[8], (DEPTH, D_MIX, D_MODEL), D_MIX ** -0.5),
        "qk_g_na": gain(k[9], (DEPTH, 2, HEAD_DIM)),
        "na_rpb": nrm(k[10], (DEPTH, NA_HEADS, 2 * NA_KH - 1, 2 * NA_KW - 1), 0.02),
        "qk_g_diff": gain(k[11], (DEPTH, 2, DIFF_QK_DIM)),
        "diff_lambda": nrm(k[12], (DEPTH, 4, DIFF_QK_DIM), 0.1),
        "diff_subln_g": gain(k[13], (DEPTH, DIFF_V_DIM)),
        "qk_g_swa": gain(k[14], (DEPTH, 2, HEAD_DIM)),
        "swa_sink": nrm(k[15], (DEPTH, SWA_Q_HEADS), 0.5),
        "lru_conv_w": nrm(k[16], (DEPTH, LRU_CONV, LRU_WIDTH), LRU_CONV ** -0.5),
        "lru_conv_b": nrm(k[17], (DEPTH, LRU_WIDTH), 0.01),
        "lru_gate_w": nrm(k[18], (DEPTH, 2, 2, LRU_BLOCKS, bs, bs), bs ** -0.5),
        "lru_gate_b": nrm(k[19], (DEPTH, 2, 2, LRU_WIDTH), 0.01),
        "lru_lam": jnp.log(s) - jnp.log1p(-s),
        "w_router": nrm(k[21], (DEPTH, D_MODEL, N_EXPERTS), D_MODEL ** -0.5),
        "w1": nrm(k[22], (DEPTH, N_EXPERTS, D_MODEL, EXPERT_FF), D_MODEL ** -0.5),
        "w3": nrm(k[23], (DEPTH, N_EXPERTS, D_MODEL, EXPERT_FF), D_MODEL ** -0.5),
        "w2": nrm(k[24], (DEPTH, N_EXPERTS, EXPERT_FF, D_MODEL), EXPERT_FF ** -0.5),
    }


def reference(x, c, ctx, c_ctx, w_ada, b_ada, norm_g, w_in, w_out, qk_g_na, na_rpb, qk_g_diff, diff_lambda,
              diff_subln_g, qk_g_swa, swa_sink, lru_conv_w, lru_conv_b, lru_gate_w, lru_gate_b, lru_lam,
              w_router, w1, w3, w2):
    n = x.shape[1]
    ang_diff = axial_rope_angles(n, DIFF_QK_DIM)
    ang_swa = axial_rope_angles(n, HEAD_DIM)
    silu_c = jax.nn.silu(c)
    silu_cc = jax.nn.silu(c_ctx)
    for l in range(DEPTH):
        need_ctx = l < DEPTH - 1
        mod = (silu_c @ w_ada[l] + b_ada[l])[:, None, :]
        mod_c = silu_cc @ w_ada[l] + b_ada[l]
        sh1, sc1, g1, sh2, sc2, g2 = jnp.split(mod, 6, axis=-1)
        csh1, csc1, cg1, csh2, csc2, cg2 = jnp.split(mod_c, 6, axis=-1)
        xn = modulate(rmsnorm(x, norm_g[l, 0]), sh1, sc1)
        cn = modulate(rmsnorm(ctx, norm_g[l, 0]), csh1, csc1)
        o_lat, o_ctx = token_mixers(xn @ w_in[l], cn @ w_in[l], ang_diff, ang_swa, l, qk_g_na[l], na_rpb[l],
                                    qk_g_diff[l], diff_lambda[l], diff_subln_g[l], qk_g_swa[l], swa_sink[l],
                                    lru_conv_w[l], lru_conv_b[l], lru_gate_w[l], lru_gate_b[l], lru_lam[l], need_ctx)
        x = x + g1 * (o_lat @ w_out[l])
        x = x + g2 * expert_choice_ffn(modulate(rmsnorm(x, norm_g[l, 1]), sh2, sc2), w_router[l], w1[l], w3[l], w2[l])
        if need_ctx:
            ctx = ctx + cg1 * (o_ctx @ w_out[l])
            ctx = ctx + cg2 * expert_choice_ffn(modulate(rmsnorm(ctx, norm_g[l, 1]), csh2, csc2),
                                                w_router[l], w1[l], w3[l], w2[l])
    return x
```

```python
import functools
import math

import numpy as np
import jax
import jax.numpy as jnp
from jax import lax
from jax.experimental import pallas as pl
from jax.experimental.pallas import tpu as pltpu

GRID_W = 64
HEAD_DIM = 128
NA_HEADS = 8
NA_KH = 8
NA_KW = 16
DIFF_HEADS = 8
DIFF_QK_DIM = 64
SWA_Q_HEADS = 8
SWA_KV_HEADS = 2
SWA_GROUP = SWA_Q_HEADS // SWA_KV_HEADS
SWA_WINDOW = 128
LRU_WIDTH = 1024
LRU_BLOCKS = 8
LRU_C = 8.0
N_EXPERTS = 16
EXPERT_FF = 256
CAPACITY_FACTOR = 2
ROPE_THETA = 10000.0
NEG_INF = -1e30
EPS = 1e-6

OFF_QA, OFF_KA, OFF_VA = 0, 1024, 2048
OFF_QD, OFF_KD, OFF_VD = 3072, 4096, 5120
OFF_QS, OFF_KS, OFF_VS = 6144, 7168, 7424
OFF_XR, OFF_GR = 7680, 8704
D_ATT = 7680
D_IN = 9728
D_MIX = 4096

LANE = 128
MOD_ROWS = 8
V7X_VMEM_BYTES = 64 * 1024 * 1024

F32 = jnp.float32
BF16 = jnp.bfloat16
I32 = jnp.int32


def _tile(n, target, mult=LANE):
    best = None
    for t in range(mult, min(n, target) + 1, mult):
        if n % t == 0:
            best = t
    assert best is not None, (n, target, mult)
    return best


def _params(sem, vmem_mb):
    return pltpu.CompilerParams(dimension_semantics=sem, vmem_limit_bytes=min(vmem_mb, 60) * 1024 * 1024)


def _dot_nt(a, b):
    return lax.dot_general(a, b, (((1,), (1,)), ((), ())), preferred_element_type=F32)


def _dot(a, b):
    return jnp.dot(a, b, preferred_element_type=F32)


def _ada_kernel(c_ref, w_ref, b_ref, o_ref):
    cv = c_ref[...]
    s = cv * jax.nn.sigmoid(cv)
    o_ref[...] = jnp.dot(s, w_ref[...], preferred_element_type=F32,
                         precision=lax.Precision.HIGHEST) + b_ref[...]


def _ada(cvec, w_ada, b_ada):
    depth, d, n6 = w_ada.shape
    tn = _tile(n6, 512)
    return pl.pallas_call(
        _ada_kernel,
        grid=(depth, n6 // tn),
        in_specs=[pl.BlockSpec((MOD_ROWS, d), lambda l, j: (0, 0)),
                  pl.BlockSpec((None, d, tn), lambda l, j: (l, 0, j)),
                  pl.BlockSpec((None, 1, tn), lambda l, j: (l, 0, j))],
        out_specs=pl.BlockSpec((None, MOD_ROWS, tn), lambda l, j: (l, 0, j)),
        out_shape=jax.ShapeDtypeStruct((depth, MOD_ROWS, n6), F32),
        compiler_params=_params(("arbitrary", "arbitrary"), 40),
        name="ada_mod",
    )(cvec, w_ada, b_ada.reshape(depth, 1, n6))


ROW_CHUNK = 32


def _mod_row(ref, row0, b, n_batch, lc):
    return jnp.where(row0 < lc, ref[pl.ds(n_batch, 1), :], ref[pl.ds(b, 1), :])


def _norm_modulate(x, g, sh, sc):
    ms = jnp.mean(x * x, axis=-1, keepdims=True)
    return x * lax.rsqrt(ms + EPS) * g * (1.0 + sc) + sh


def _for_row_chunks(tm, fn):
    assert tm % ROW_CHUNK == 0

    def body(c, _):
        fn(pl.multiple_of(c * ROW_CHUNK, ROW_CHUNK))
        return 0

    lax.fori_loop(0, tm // ROW_CHUNK, body, 0)


def _inproj_kernel(x_ref, g_ref, sh_ref, sc_ref, w_ref, o_ref, xn_ref, *, n_batch, lc, tm):
    b = pl.program_id(0)
    i = pl.program_id(1)

    @pl.when(pl.program_id(2) == 0)
    def _():
        def chunk(r0):
            sh = _mod_row(sh_ref, i * tm + r0, b, n_batch, lc)
            sc = _mod_row(sc_ref, i * tm + r0, b, n_batch, lc)
            xn = _norm_modulate(x_ref[pl.ds(r0, ROW_CHUNK), :], g_ref[...], sh, sc)
            xn_ref[pl.ds(r0, ROW_CHUNK), :] = xn.astype(BF16)

        _for_row_chunks(tm, chunk)

    o_ref[...] = _dot(xn_ref[...], w_ref[...])


def _inproj(xs, g, modl, w_bf, lc):
    n_batch, nt, d = xs.shape
    d_in = w_bf.shape[1]
    tm = _tile(nt, 768)
    tn = _tile(d_in, 512)
    assert lc % ROW_CHUNK == 0
    kern = functools.partial(_inproj_kernel, n_batch=n_batch, lc=lc, tm=tm)
    return pl.pallas_call(
        kern,
        grid=(n_batch, nt // tm, d_in // tn),
        in_specs=[pl.BlockSpec((None, tm, d), lambda b, i, j: (b, i, 0)),
                  pl.BlockSpec((1, d), lambda b, i, j: (0, 0)),
                  pl.BlockSpec((MOD_ROWS, d), lambda b, i, j: (0, 0)),
                  pl.BlockSpec((MOD_ROWS, d), lambda b, i, j: (0, 1)),
                  pl.BlockSpec((d, tn), lambda b, i, j: (0, j))],
        out_specs=pl.BlockSpec((None, tm, tn), lambda b, i, j: (b, i, j)),
        out_shape=jax.ShapeDtypeStruct((n_batch, nt, d_in), F32),
        scratch_shapes=[pltpu.VMEM((tm, d), BF16)],
        compiler_params=_params(("arbitrary", "arbitrary", "arbitrary"), 56),
        name="in_proj",
    )(xs, g, modl, modl, w_bf)


def _rms_full(x, g):
    ms = jnp.mean(x * x, axis=-1, keepdims=True)
    return x * lax.rsqrt(ms + EPS) * g


def _rms_halves(x, g):
    lo = lax.broadcasted_iota(I32, (1, LANE), 1) < DIFF_QK_DIM
    x2 = x * x
    s_lo = jnp.sum(jnp.where(lo, x2, 0.0), axis=-1, keepdims=True)
    s_hi = jnp.sum(jnp.where(lo, 0.0, x2), axis=-1, keepdims=True)
    ms = jnp.where(lo, s_lo, s_hi) * (1.0 / DIFF_QK_DIM)
    return x * lax.rsqrt(ms + EPS) * g


def _rope(x, cos, sin_signed, half):
    lane = lax.broadcasted_iota(I32, (1, LANE), 1)
    up = pltpu.roll(x, LANE - half, axis=1)
    dn = pltpu.roll(x, half, axis=1)
    partner = jnp.where((lane & half) == 0, up, dn)
    return x * cos + partner * sin_signed


def _prep_kernel(u_ref, p_ref, cs_ref, ss_ref, cd_ref, sd_ref,
                 qa_ref, ka_ref, va_ref, qd_ref, kd_ref, vd_ref, qs_ref, ks_ref, vs_ref):
    g_na_q, g_na_k = p_ref[0:1, :], p_ref[1:2, :]
    g_d_q, g_d_k = p_ref[2:3, :], p_ref[3:4, :]
    g_s_q, g_s_k = p_ref[4:5, :], p_ref[5:6, :]
    cs, ss, cd, sd = cs_ref[...], ss_ref[...], cd_ref[...], sd_ref[...]
    lo = lax.broadcasted_iota(I32, (1, LANE), 1) < DIFF_QK_DIM

    def col(off, h):
        return u_ref[:, off + h * LANE: off + (h + 1) * LANE]

    def put(ref, h, val):
        ref[:, h * LANE:(h + 1) * LANE] = val.astype(BF16)

    for h in range(NA_HEADS):
        put(qa_ref, h, _rms_full(col(OFF_QA, h), g_na_q))
        put(ka_ref, h, _rms_full(col(OFF_KA, h), g_na_k))
        put(va_ref, h, col(OFF_VA, h))
    for h in range(DIFF_HEADS):
        q = _rope(_rms_halves(col(OFF_QD, h), g_d_q), cd, sd, DIFF_QK_DIM // 4) * (DIFF_QK_DIM ** -0.5)
        qd_ref[0, :, h * LANE:(h + 1) * LANE] = jnp.where(lo, q, 0.0).astype(BF16)
        qd_ref[1, :, h * LANE:(h + 1) * LANE] = jnp.where(lo, 0.0, q).astype(BF16)
        put(kd_ref, h, _rope(_rms_halves(col(OFF_KD, h), g_d_k), cd, sd, DIFF_QK_DIM // 4))
        put(vd_ref, h, col(OFF_VD, h))
    for h in range(SWA_Q_HEADS):
        put(qs_ref, h, _rope(_rms_full(col(OFF_QS, h), g_s_q), cs, ss, HEAD_DIM // 4))
    for h in range(SWA_KV_HEADS):
        put(ks_ref, h, _rope(_rms_full(col(OFF_KS, h), g_s_k), cs, ss, HEAD_DIM // 4))
        put(vs_ref, h, col(OFF_VS, h))


def _prep(u, pvec, cs, ss, cd, sd):
    n_batch, nt, _ = u.shape
    tr = _tile(nt, 256)
    row = lambda w: pl.BlockSpec((None, tr, w), lambda b, i: (b, i, 0))
    tab = pl.BlockSpec((tr, LANE), lambda b, i: (i, 0))
    sd_ = lambda w: jax.ShapeDtypeStruct((n_batch, nt, w), BF16)
    return pl.pallas_call(
        _prep_kernel,
        grid=(n_batch, nt // tr),
        in_specs=[row(D_ATT), pl.BlockSpec((8, LANE), lambda b, i: (0, 0)), tab, tab, tab, tab],
        out_specs=[row(1024), row(1024), row(1024),
                   pl.BlockSpec((None, 2, tr, 1024), lambda b, i: (b, 0, i, 0)),
                   row(1024), row(1024), row(1024), row(256), row(256)],
        out_shape=[sd_(1024), sd_(1024), sd_(1024),
                   jax.ShapeDtypeStruct((n_batch, 2, nt, 1024), BF16),
                   sd_(1024), sd_(1024), sd_(1024), sd_(256), sd_(256)],
        compiler_params=_params(("arbitrary", "arbitrary"), 48),
        name="qkv_prep",
    )(u, pvec, cs, ss, cd, sd)


NA_QBLK = 2 * GRID_W
NA_KROWS = 10
NA_KBLK = NA_KROWS * GRID_W


def _na_kernel(q_ref, k_ref, v_ref, bias_ref, o_ref, *, lc, nlb):
    i = pl.program_id(2)
    li = i - lc // NA_QBLK
    kb = jnp.clip(li - 2, 0, nlb - NA_KROWS // 2)
    start = pl.multiple_of(lc + kb * NA_QBLK, NA_QBLK)
    scale = HEAD_DIM ** -0.5
    q = q_ref[...]
    kw = k_ref[pl.ds(start, NA_KBLK), :]
    vw = v_ref[pl.ds(start, NA_KBLK), :]
    kc = k_ref[0:lc, :]
    vc = v_ref[0:lc, :]
    s_n = _dot_nt(q, kw) * scale + bias_ref[...]
    s_c = _dot_nt(q, kc) * scale
    m = jnp.maximum(jnp.max(s_n, axis=-1, keepdims=True), jnp.max(s_c, axis=-1, keepdims=True))
    p_n = jnp.exp(s_n - m)
    p_c = jnp.exp(s_c - m)
    l = jnp.sum(p_n, axis=-1, keepdims=True) + jnp.sum(p_c, axis=-1, keepdims=True)
    o = _dot(p_n.astype(BF16), vw) + _dot(p_c.astype(BF16), vc)
    o_ref[...] = (o / l).astype(BF16)


def _na_bias_tables(rpb, n):
    rows = n // GRID_W
    nlb = rows // 2
    reps = [0, 1, 2, nlb - 2, nlb - 1]
    dy_i = np.zeros((5, NA_QBLK, NA_KBLK), np.int32)
    dx_i = np.zeros((5, NA_QBLK, NA_KBLK), np.int32)
    ok = np.zeros((5, NA_QBLK, NA_KBLK), bool)
    qrow = np.arange(NA_QBLK) // GRID_W
    qcol = np.arange(NA_QBLK) % GRID_W
    krow = np.arange(NA_KBLK) // GRID_W
    kcol = np.arange(NA_KBLK) % GRID_W
    for c, li in enumerate(reps):
        kb = int(np.clip(li - 2, 0, nlb - NA_KROWS // 2))
        qr = 2 * li + qrow
        kr = 2 * kb + krow
        rs = np.clip(qr - NA_KH // 2, 0, rows - NA_KH)
        cstart = np.clip(qcol - NA_KW // 2, 0, GRID_W - NA_KW)
        row_ok = (kr[None, :] >= rs[:, None]) & (kr[None, :] < rs[:, None] + NA_KH)
        col_ok = (kcol[None, :] >= cstart[:, None]) & (kcol[None, :] < cstart[:, None] + NA_KW)
        ok[c] = row_ok & col_ok
        dy_i[c] = np.clip(kr[None, :] - qr[:, None] + (NA_KH - 1), 0, 2 * NA_KH - 2)
        dx_i[c] = np.clip(kcol[None, :] - qcol[:, None] + (NA_KW - 1), 0, 2 * NA_KW - 2)
    vals = rpb[:, dy_i, dx_i]
    tbl = jnp.where(ok[None], vals, NEG_INF).transpose(1, 0, 2, 3)
    dead = jnp.full((1,) + tbl.shape[1:], NEG_INF, F32)
    return jnp.concatenate([tbl, dead], axis=0)


def _na(qa, ka, va, bias, lc):
    n_batch, nt, _ = qa.shape
    n = nt - lc
    nlb = n // NA_QBLK
    assert nlb >= 5 and lc % NA_QBLK == 0
    ncb = lc // NA_QBLK

    def case(i):
        li = i - ncb
        return jnp.where(li < 0, 5, jnp.where(li < 2, li, jnp.where(li >= nlb - 2, li - (nlb - 5), 2)))

    kern = functools.partial(_na_kernel, lc=lc, nlb=nlb)
    kv = pl.BlockSpec((None, nt, LANE), lambda b, h, i: (b, 0, h))
    return pl.pallas_call(
        kern,
        grid=(n_batch, NA_HEADS, nt // NA_QBLK),
        in_specs=[pl.BlockSpec((None, NA_QBLK, LANE), lambda b, h, i: (b, i, h)), kv, kv,
                  pl.BlockSpec((None, None, NA_QBLK, NA_KBLK), lambda b, h, i: (case(i), h, 0, 0))],
        out_specs=pl.BlockSpec((None, NA_QBLK, LANE), lambda b, h, i: (b, i, h)),
        out_shape=jax.ShapeDtypeStruct((n_batch, nt, NA_HEADS * HEAD_DIM), BF16),
        compiler_params=_params(("arbitrary",) * 3, 32),
        name="na_attn",
    )(qa, ka, va, bias)


SWA_BLK = 128
SWA_WIN = 3 * SWA_BLK


def _swa_kernel(q_ref, k_ref, v_ref, sink_ref, o_ref, *, lc, n):
    i = pl.program_id(2)
    li = i - lc // SWA_BLK
    ws = jnp.clip((li - 1) * SWA_BLK, 0, n - SWA_WIN)
    start = pl.multiple_of(lc + ws, SWA_BLK)
    scale = HEAD_DIM ** -0.5
    kw = k_ref[pl.ds(start, SWA_WIN), :]
    vw = v_ref[pl.ds(start, SWA_WIN), :]
    kc = k_ref[0:lc, :]
    vc = v_ref[0:lc, :]
    r = lax.broadcasted_iota(I32, (SWA_BLK, SWA_WIN), 0)
    j = lax.broadcasted_iota(I32, (SWA_BLK, SWA_WIN), 1)
    mask = (jnp.abs(ws + j - (li * SWA_BLK + r)) <= SWA_WINDOW) & (li >= 0)
    for g in range(SWA_GROUP):
        q = q_ref[:, g * LANE:(g + 1) * LANE]
        s_w = jnp.where(mask, _dot_nt(q, kw) * scale, NEG_INF)
        s_c = _dot_nt(q, kc) * scale
        sink = sink_ref[g * SWA_BLK:(g + 1) * SWA_BLK, 0:1]
        m = jnp.maximum(jnp.maximum(jnp.max(s_w, axis=-1, keepdims=True), jnp.max(s_c, axis=-1, keepdims=True)),
                        sink)
        p_w = jnp.exp(s_w - m)
        p_c = jnp.exp(s_c - m)
        l = jnp.sum(p_w, axis=-1, keepdims=True) + jnp.sum(p_c, axis=-1, keepdims=True) + jnp.exp(sink - m)
        o = (_dot(p_w.astype(BF16), vw) + _dot(p_c.astype(BF16), vc)) / l
        o_ref[:, g * LANE:(g + 1) * LANE] = o.astype(BF16)


def _swa(qs, ks, vs, sink, lc):
    n_batch, nt, _ = qs.shape
    n = nt - lc
    assert n >= SWA_WIN and lc % SWA_BLK == 0
    rows = SWA_GROUP * SWA_BLK
    sink_rows = jnp.broadcast_to(
        jnp.repeat(sink.astype(F32).reshape(SWA_KV_HEADS, SWA_GROUP), SWA_BLK, axis=1)[:, :, None],
        (SWA_KV_HEADS, rows, LANE))
    kern = functools.partial(_swa_kernel, lc=lc, n=n)
    kv = pl.BlockSpec((None, nt, LANE), lambda b, k, i: (b, 0, k))
    qo = pl.BlockSpec((None, SWA_BLK, SWA_GROUP * LANE), lambda b, k, i: (b, i, k))
    return pl.pallas_call(
        kern,
        grid=(n_batch, SWA_KV_HEADS, nt // SWA_BLK),
        in_specs=[qo, kv, kv, pl.BlockSpec((None, rows, LANE), lambda b, k, i: (k, 0, 0))],
        out_specs=qo,
        out_shape=jax.ShapeDtypeStruct((n_batch, nt, SWA_Q_HEADS * HEAD_DIM), BF16),
        compiler_params=_params(("arbitrary",) * 3, 32),
        name="swa_attn",
    )(qs, ks, vs, sink_rows)


def _diff_kernel(lam_ref, q_ref, k_ref, v_ref, g_ref, o_ref, *, lc, nt, tq, tk, post_scale):
    qi = pl.program_id(2)
    q = q_ref[...].reshape(2 * tq, LANE)
    nk = jnp.where(qi < lc // tq, lc // tk, nt // tk)

    def body(kt, carry):
        m, l, acc = carry
        k0 = pl.multiple_of(kt * tk, tk)
        k = k_ref[pl.ds(k0, tk), :]
        v = v_ref[pl.ds(k0, tk), :]
        s = _dot_nt(q, k)
        m_new = jnp.maximum(m, jnp.max(s, axis=-1, keepdims=True))
        alpha = jnp.exp(m - m_new)
        p = jnp.exp(s - m_new)
        l = alpha * l + jnp.sum(p, axis=-1, keepdims=True)
        acc = alpha * acc + _dot(p.astype(BF16), v)
        return m_new, l, acc

    init = (jnp.full((2 * tq, 1), NEG_INF, F32), jnp.zeros((2 * tq, 1), F32), jnp.zeros((2 * tq, LANE), F32))
    _, l, acc = lax.fori_loop(0, nk, body, init)
    o2 = acc / l
    o = o2[0:tq, :] - lam_ref[0] * o2[tq:2 * tq, :]
    o_ref[...] = (_rms_full(o, g_ref[6:7, :]) * post_scale).astype(BF16)


def _diff(lam, qd, kd, vd, pvec, lc, post_scale):
    n_batch, _, nt, _ = qd.shape
    tq = _tile(math.gcd(lc, nt), 256)
    tk = _tile(math.gcd(lc, nt), 256)
    kern = functools.partial(_diff_kernel, lc=lc, nt=nt, tq=tq, tk=tk, post_scale=post_scale)
    kv = pl.BlockSpec((None, nt, LANE), lambda b, h, i: (b, 0, h))
    return pl.pallas_call(
        kern,
        grid=(n_batch, DIFF_HEADS, nt // tq),
        in_specs=[pl.BlockSpec(memory_space=pltpu.SMEM),
                  pl.BlockSpec((None, 2, tq, LANE), lambda b, h, i: (b, 0, i, h)), kv, kv,
                  pl.BlockSpec((8, LANE), lambda b, h, i: (0, 0))],
        out_specs=pl.BlockSpec((None, tq, LANE), lambda b, h, i: (b, i, h)),
        out_shape=jax.ShapeDtypeStruct((n_batch, nt, DIFF_HEADS * HEAD_DIM), BF16),
        compiler_params=_params(("arbitrary",) * 3, 32),
        name="diff_attn",
    )(lam, qd, kd, vd, pvec)


LRU_HALO = 8


def _lru_gate_kernel(xp_ref, x_ref, xn_ref, cw_ref, cb_ref, gw_ref, gb_ref, lam_ref,
                     af_ref, bf_ref, ab_ref, bb_ref, *, t, seg_starts, seg_ends):
    k = pl.program_id(2)
    is_start = functools.reduce(jnp.logical_or, [k == s for s in seg_starts])
    is_end = functools.reduce(jnp.logical_or, [k == s for s in seg_ends])
    cur = x_ref[...]
    prev = jnp.where(is_start, 0.0, xp_ref[...])
    nxt = jnp.where(is_end, 0.0, xn_ref[...])
    ext = jnp.concatenate([prev, cur, nxt], axis=0)
    h = LRU_HALO
    xl = (cb_ref[...] + cw_ref[0:1, :] * ext[h - 2:h - 2 + t, :] + cw_ref[1:2, :] * ext[h - 1:h - 1 + t, :]
          + cw_ref[2:3, :] * cur + cw_ref[3:4, :] * ext[h + 1:h + 1 + t, :])
    xb = xl.astype(BF16)
    outs = ((af_ref, bf_ref), (ab_ref, bb_ref))
    for d in range(2):
        r = jax.nn.sigmoid(_dot(xb, gw_ref[d, 0].astype(BF16)) + gb_ref[2 * d:2 * d + 1, :])
        ig = jax.nn.sigmoid(_dot(xb, gw_ref[d, 1].astype(BF16)) + gb_ref[2 * d + 1:2 * d + 2, :])
        log_a = LRU_C * r * jax.nn.log_sigmoid(lam_ref[d:d + 1, :])
        a = jnp.exp(log_a)
        outs[d][0][...] = a
        outs[d][1][...] = jnp.sqrt(1.0 - jnp.exp(2.0 * log_a)) * (ig * xl)


def _lru_gates(u, conv_w, conv_b, gate_w, gate_b, lam, lc):
    n_batch, nt, _ = u.shape
    t = _tile(math.gcd(lc, nt), 256)
    nk = nt // t
    hb = t // LRU_HALO
    seg_starts = (0, lc // t)
    seg_ends = (lc // t - 1, nk - 1)
    cb0 = OFF_XR // LANE
    kern = functools.partial(_lru_gate_kernel, t=t, seg_starts=seg_starts, seg_ends=seg_ends)
    out = pl.BlockSpec((None, t, LANE), lambda b, c, k: (b, k, c))
    osd = jax.ShapeDtypeStruct((n_batch, nt, LRU_WIDTH), F32)
    return pl.pallas_call(
        kern,
        grid=(n_batch, LRU_BLOCKS, nk),
        in_specs=[pl.BlockSpec((None, LRU_HALO, LANE), lambda b, c, k: (b, jnp.maximum(k * hb - 1, 0), cb0 + c)),
                  pl.BlockSpec((None, t, LANE), lambda b, c, k: (b, k, cb0 + c)),
                  pl.BlockSpec((None, LRU_HALO, LANE),
                               lambda b, c, k: (b, jnp.minimum((k + 1) * hb, nt // LRU_HALO - 1), cb0 + c)),
                  pl.BlockSpec((4, LANE), lambda b, c, k: (0, c)),
                  pl.BlockSpec((1, LANE), lambda b, c, k: (0, c)),
                  pl.BlockSpec((2, 2, None, LANE, LANE), lambda b, c, k: (0, 0, c, 0, 0)),
                  pl.BlockSpec((4, LANE), lambda b, c, k: (0, c)),
                  pl.BlockSpec((2, LANE), lambda b, c, k: (0, c))],
        out_specs=[out, out, out, out],
        out_shape=[osd, osd, osd, osd],
        compiler_params=_params(("arbitrary",) * 3, 24),
        name="lru_gates",
    )(u, u, u, conv_w, conv_b.reshape(1, LRU_WIDTH), gate_w, gate_b.reshape(4, LRU_WIDTH), lam)


def _lru_scan_kernel(af_ref, bf_ref, ab_ref, bb_ref, hf_ref, hb_ref, cf_ref, cbk_ref, *, t):
    @pl.when(pl.program_id(1) == 0)
    def _():
        cf_ref[...] = jnp.zeros_like(cf_ref)
        cbk_ref[...] = jnp.zeros_like(cbk_ref)

    ng = t // 8

    def body(g, carry):
        hf, hb = carry
        r0 = pl.multiple_of(g * 8, 8)
        a = af_ref[pl.ds(r0, 8), :]
        bt = bf_ref[pl.ds(r0, 8), :]
        rows = []
        for r in range(8):
            hf = a[r:r + 1, :] * hf + bt[r:r + 1, :]
            rows.append(hf)
        hf_ref[pl.ds(r0, 8), :] = jnp.concatenate(rows, axis=0)
        r1 = pl.multiple_of((ng - 1 - g) * 8, 8)
        a = ab_ref[pl.ds(r1, 8), :]
        bt = bb_ref[pl.ds(r1, 8), :]
        rows = []
        for r in range(7, -1, -1):
            hb = a[r:r + 1, :] * hb + bt[r:r + 1, :]
            rows.append(hb)
        hb_ref[pl.ds(r1, 8), :] = jnp.concatenate(rows[::-1], axis=0)
        return hf, hb

    hf, hb = lax.fori_loop(0, ng, body, (cf_ref[...], cbk_ref[...]))
    cf_ref[...] = hf
    cbk_ref[...] = hb


def _lru_scan(af, bf, ab, bb, lc):
    n_batch, nt, w = af.shape
    t = lc
    assert nt % t == 0
    nk = nt // t
    kern = functools.partial(_lru_scan_kernel, t=t)
    fwd = pl.BlockSpec((None, t, w), lambda b, k: (b, k, 0))
    bwd = pl.BlockSpec((None, t, w), lambda b, k: (b, jnp.where(k == 0, 0, nk - k), 0))
    osd = jax.ShapeDtypeStruct((n_batch, nt, w), F32)
    return pl.pallas_call(
        kern,
        grid=(n_batch, nk),
        in_specs=[fwd, fwd, bwd, bwd],
        out_specs=[fwd, bwd],
        out_shape=[osd, osd],
        scratch_shapes=[pltpu.VMEM((1, w), F32), pltpu.VMEM((1, w), F32)],
        compiler_params=_params(("arbitrary", "arbitrary"), 24),
        name="lru_scan",
    )(af, bf, ab, bb)


def _outproj_kernel(oa_ref, od_ref, os_ref, hf_ref, hb_ref, g0_ref, g1_ref, gate_ref, w_ref, x_ref,
                    o_ref, lhs_ref, *, n_batch, lc, tm):
    b = pl.program_id(0)
    i = pl.program_id(1)

    @pl.when(pl.program_id(2) == 0)
    def _():
        def chunk(r0):
            rs = pl.ds(r0, ROW_CHUNK)
            lhs_ref[rs, 0:1024] = oa_ref[rs, :]
            lhs_ref[rs, 1024:2048] = od_ref[rs, :]
            lhs_ref[rs, 2048:3072] = os_ref[rs, :]
            for c, g_ref in enumerate((g0_ref, g1_ref)):
                cs = slice(c * 512, (c + 1) * 512)
                y = (hf_ref[rs, cs] + hb_ref[rs, cs]) * jax.nn.gelu(g_ref[rs, :])
                lhs_ref[rs, 3072 + c * 512:3584 + c * 512] = y.astype(BF16)

        _for_row_chunks(tm, chunk)

    row = i * tm + lax.broadcasted_iota(I32, (tm, 1), 0)
    gate = jnp.where(row < lc, gate_ref[pl.ds(n_batch, 1), :], gate_ref[pl.ds(b, 1), :])
    o_ref[...] = x_ref[...] + gate * _dot(lhs_ref[...], w_ref[...])


def _outproj(o_a, o_d, o_s, hf, hb, u, modl, w_bf, xs, lc):
    n_batch, nt, d = xs.shape
    tm = _tile(nt, 384)
    tn = _tile(d, 512)
    ntile = d // tn
    kern = functools.partial(_outproj_kernel, n_batch=n_batch, lc=lc, tm=tm)
    mix = pl.BlockSpec((None, tm, 1024), lambda b, i, j: (b, i, 0))
    gr = lambda c: pl.BlockSpec((None, tm, 512), lambda b, i, j: (b, i, OFF_GR // 512 + c))
    xblk = pl.BlockSpec((None, tm, tn), lambda b, i, j: (b, i, j))
    return pl.pallas_call(
        kern,
        grid=(n_batch, nt // tm, ntile),
        in_specs=[mix, mix, mix, mix, mix, gr(0), gr(1),
                  pl.BlockSpec((MOD_ROWS, tn), lambda b, i, j: (0, 2 * ntile + j)),
                  pl.BlockSpec((D_MIX, tn), lambda b, i, j: (0, j)), xblk],
        out_specs=xblk,
        out_shape=jax.ShapeDtypeStruct(xs.shape, F32),
        scratch_shapes=[pltpu.VMEM((tm, D_MIX), BF16)],
        input_output_aliases={9: 0},
        compiler_params=_params(("arbitrary",) * 3, 56),
        name="out_proj",
    )(o_a, o_d, o_s, hf, hb, u, u, modl, w_bf, xs)


def _split3(x):
    hi = x.astype(BF16)
    r = x - hi.astype(F32)
    mid = r.astype(BF16)
    return hi, mid, (r - mid.astype(F32)).astype(BF16)


ROUTER_KC = 512


def _router_kernel(x_ref, g_ref, sh_ref, sc_ref, w0_ref, w1_ref, w2_ref, xm_ref, aff_ref, *, n_batch, lc, tm):
    b = pl.program_id(0)
    i = pl.program_id(1)

    def chunk(r0):
        sh = _mod_row(sh_ref, i * tm + r0, b, n_batch, lc)
        sc = _mod_row(sc_ref, i * tm + r0, b, n_batch, lc)
        xm_ref[pl.ds(r0, ROW_CHUNK), :] = _norm_modulate(x_ref[pl.ds(r0, ROW_CHUNK), :], g_ref[...], sh, sc)

    _for_row_chunks(tm, chunk)

    logits = jnp.zeros((tm, LANE), F32)
    tk = _tile(xm_ref.shape[1], ROUTER_KC)
    for kc in range(xm_ref.shape[1] // tk):
        ks = slice(kc * tk, (kc + 1) * tk)
        x0, x1, x2 = _split3(xm_ref[:, ks])
        w0, w1, w2 = w0_ref[ks, :], w1_ref[ks, :], w2_ref[ks, :]
        logits = logits + (_dot(x0, w0) + (_dot(x0, w1) + _dot(x1, w0))
                           + (_dot(x1, w1) + _dot(x0, w2) + _dot(x2, w0)))
    lane = lax.broadcasted_iota(I32, (1, LANE), 1)
    logits = jnp.where(lane < N_EXPERTS, logits, NEG_INF)
    m = jnp.max(logits, axis=-1, keepdims=True)
    p = jnp.exp(logits - m)
    aff_ref[...] = p / jnp.sum(p, axis=-1, keepdims=True)


def _router(xs, g, modl, wr_pad, lc):
    n_batch, nt, d = xs.shape
    tm = _tile(nt, 256)
    assert lc % ROW_CHUNK == 0
    kern = functools.partial(_router_kernel, n_batch=n_batch, lc=lc, tm=tm)
    wspec = pl.BlockSpec((d, LANE), lambda b, i: (0, 0))
    return pl.pallas_call(
        kern,
        grid=(n_batch, nt // tm),
        in_specs=[pl.BlockSpec((None, tm, d), lambda b, i: (b, i, 0)),
                  pl.BlockSpec((1, d), lambda b, i: (0, 0)),
                  pl.BlockSpec((MOD_ROWS, d), lambda b, i: (0, 3)),
                  pl.BlockSpec((MOD_ROWS, d), lambda b, i: (0, 4)),
                  wspec, wspec, wspec],
        out_specs=[pl.BlockSpec((None, tm, d), lambda b, i: (b, i, 0)),
                   pl.BlockSpec((None, tm, LANE), lambda b, i: (b, i, 0))],
        out_shape=[jax.ShapeDtypeStruct((n_batch, nt, d), F32),
                   jax.ShapeDtypeStruct((n_batch, nt, LANE), F32)],
        compiler_params=_params(("arbitrary", "arbitrary"), 48),
        name="norm2_router",
    )(xs, g, modl, modl, *_split3(wr_pad))


def _select_kernel(aff_ref, l_ref, cnt_ref, off_ref, bits_ref, *, row0, n, cap):
    nc = n // LANE
    bits_ref[...] = lax.bitcast_convert_type(aff_ref[row0:row0 + n, :], I32)

    def bit_step(i, thr):
        cand = thr | (jnp.int32(1) << (30 - i))
        cnt = jnp.sum((bits_ref[...] >= cand).astype(I32), axis=0, keepdims=True)
        return jnp.where(cnt >= cap, cand, thr)

    thr = lax.fori_loop(0, 31, bit_step, jnp.zeros((1, LANE), I32))
    n_gt = jnp.sum((bits_ref[...] > thr).astype(I32), axis=0, keepdims=True)
    need = (cap - n_gt).astype(F32)

    r_io = lax.broadcasted_iota(I32, (LANE, LANE), 0)
    c_io = lax.broadcasted_iota(I32, (LANE, LANE), 1)
    ltri = (c_io <= r_io).astype(BF16)
    t_col = r_io.astype(F32)
    j_row = c_io.astype(F32)

    def chunk(c, carry):
        run_eq, run_sel = carry
        r0 = pl.multiple_of(c * LANE, LANE)
        bc = bits_ref[pl.ds(r0, LANE), :]
        eq = bc == thr
        eq_f = eq.astype(F32)
        incl_eq = _dot(ltri, eq_f.astype(BF16))
        rank = run_eq + incl_eq - eq_f
        sel = (bc > thr) | (eq & (rank < need))
        sel_f = sel.astype(F32)
        incl = _dot(ltri, sel_f.astype(BF16))
        cnt_c = incl[LANE - 1:LANE, :]
        qm = jnp.where(sel, incl - sel_f, -1.0)
        for e in range(N_EXPERTS):
            pos = jnp.broadcast_to(qm[:, e:e + 1], (LANE, LANE))
            local = jnp.sum(jnp.where(pos == j_row, t_col, 0.0), axis=0, keepdims=True)
            l_ref[e, pl.ds(c, 1), :] = local.astype(I32)
        cnt_ref[pl.ds(c, 1), :] = cnt_c.astype(I32)
        off_ref[pl.ds(c, 1), :] = run_sel.astype(I32)
        return run_eq + incl_eq[LANE - 1:LANE, :], run_sel + cnt_c

    zero = jnp.zeros((1, LANE), F32)
    lax.fori_loop(0, nc, chunk, (zero, zero))


def _select(affp, row0, n, cap):
    n_batch, nt, _ = affp.shape
    nc = n // LANE
    kern = functools.partial(_select_kernel, row0=row0, n=n, cap=cap)
    return pl.pallas_call(
        kern,
        grid=(n_batch,),
        in_specs=[pl.BlockSpec((None, nt, LANE), lambda b: (b, 0, 0))],
        out_specs=[pl.BlockSpec((None, N_EXPERTS, nc, LANE), lambda b: (b, 0, 0, 0)),
                   pl.BlockSpec((None, nc, LANE), lambda b: (b, 0, 0)),
                   pl.BlockSpec((None, nc, LANE), lambda b: (b, 0, 0))],
        out_shape=[jax.ShapeDtypeStruct((n_batch, N_EXPERTS, nc, LANE), I32),
                   jax.ShapeDtypeStruct((n_batch, nc, LANE), I32),
                   jax.ShapeDtypeStruct((n_batch, nc, LANE), I32)],
        scratch_shapes=[pltpu.VMEM((n, LANE), I32)],
        compiler_params=_params(("arbitrary",), 40),
        name="ec_select",
    )(affp)


MOE_ROWS = 256


def _moe_kernel(l_hbm, cnt_hbm, off_hbm, xm_hbm, aff_hbm, xs_in_hbm, g2_ref, w1_ref, w3_ref, w2_ref, xs_hbm,
                l_s, cnt_s, off_s, xg, ag, rows, xb, sem, *, row0, nc, cap, g_row):
    del xs_in_hbm
    b = pl.program_id(0)
    e = pl.program_id(1)
    tables = (pltpu.make_async_copy(l_hbm.at[b, e], l_s, sem.at[0]),
              pltpu.make_async_copy(cnt_hbm.at[b], cnt_s, sem.at[0]),
              pltpu.make_async_copy(off_hbm.at[b], off_s, sem.at[0]))
    for cp in tables:
        cp.start()
    for cp in tables:
        cp.wait()

    def gather_copies(tok, slot):
        return (pltpu.make_async_copy(xm_hbm.at[b, pl.ds(tok, 1), :], xg.at[pl.ds(slot, 1), :], sem.at[1]),
                pltpu.make_async_copy(aff_hbm.at[b, pl.ds(tok, 1), :], ag.at[pl.ds(slot, 1), :], sem.at[2]),
                pltpu.make_async_copy(xs_hbm.at[b, pl.ds(tok, 1), :], rows.at[pl.ds(slot, 1), :], sem.at[3]))

    def scatter_copy(tok, slot):
        return pltpu.make_async_copy(rows.at[pl.ds(slot, 1), :], xs_hbm.at[b, pl.ds(tok, 1), :], sem.at[4])

    def for_selected(fn):
        def chunk(c, _):
            base = row0 + c * LANE
            off_c = off_s[c, e]

            def one(j, _):
                fn(base + l_s[c, j], off_c + j)
                return 0

            lax.fori_loop(0, cnt_s[c, e], one, 0)
            return 0

        lax.fori_loop(0, nc, chunk, 0)

    def start_gather(tok, slot):
        for cp in gather_copies(tok, slot):
            cp.start()

    for_selected(start_gather)

    def wait_gather(s, _):
        for cp in gather_copies(row0, 0):
            cp.wait()
        return 0

    lax.fori_loop(0, cap, wait_gather, 0)

    d = rows.shape[1]
    tn = _tile(d, 512)
    rc = min(cap, MOE_ROWS)
    lane = lax.broadcasted_iota(I32, (1, LANE), 1)
    g2_row = g_row if g_row is not None else b

    def row_block(r, _):
        rs = pl.ds(pl.multiple_of(r * rc, rc), rc)
        for kc in range(d // tn):
            ks = slice(kc * tn, (kc + 1) * tn)
            xb[:, ks] = xg[rs, ks].astype(BF16)
        h1 = _dot(xb[...], w1_ref[...])
        h3 = _dot(xb[...], w3_ref[...])
        gate = jnp.sum(jnp.where(lane == e, ag[rs, :], 0.0), axis=-1, keepdims=True)
        hdn = (h1 * jax.nn.sigmoid(h1) * h3 * gate).astype(BF16)
        for c in range(d // tn):
            sl = slice(c * tn, (c + 1) * tn)
            rows[rs, sl] = rows[rs, sl] + g2_ref[pl.ds(g2_row, 1), sl] * _dot(hdn, w2_ref[:, sl])
        return 0

    lax.fori_loop(0, cap // rc, row_block, 0)

    for_selected(lambda tok, slot: scatter_copy(tok, slot).start())

    def wait_scatter(s, _):
        scatter_copy(row0, 0).wait()
        return 0

    lax.fori_loop(0, cap, wait_scatter, 0)


def _moe(ltab, cnt, off, xm, affp, xs, modl, w1, w3, w2, row0, cap, g_row):
    n_batch, nt, d = xs.shape
    nc = ltab.shape[2]
    nd = 1
    rc = min(cap, MOE_ROWS)
    assert cap % rc == 0
    kern = functools.partial(_moe_kernel, row0=row0, nc=nc, cap=cap, g_row=g_row)
    any_ = pl.BlockSpec(memory_space=pl.ANY)
    return pl.pallas_call(
        kern,
        grid=(n_batch, N_EXPERTS),
        in_specs=[any_, any_, any_, any_, any_, any_,
                  pl.BlockSpec((MOD_ROWS, d), lambda b, e: (0, 5 * nd)),
                  pl.BlockSpec((None, d, EXPERT_FF), lambda b, e: (e, 0, 0)),
                  pl.BlockSpec((None, d, EXPERT_FF), lambda b, e: (e, 0, 0)),
                  pl.BlockSpec((None, EXPERT_FF, d), lambda b, e: (e, 0, 0))],
        out_specs=any_,
        out_shape=jax.ShapeDtypeStruct(xs.shape, F32),
        scratch_shapes=[pltpu.SMEM((nc, LANE), I32), pltpu.SMEM((nc, LANE), I32), pltpu.SMEM((nc, LANE), I32),
                        pltpu.VMEM((cap, d), F32), pltpu.VMEM((cap, LANE), F32), pltpu.VMEM((cap, d), F32),
                        pltpu.VMEM((rc, d), BF16), pltpu.SemaphoreType.DMA((5,))],
        input_output_aliases={5: 0},
        compiler_params=_params(("arbitrary", "arbitrary"), 56),
        name="moe_ffn",
    )(ltab, cnt, off, xm, affp, xs, modl, w1, w3, w2)


def _rope_tables(n, lc):
    t = jnp.arange(n)
    pos = jnp.stack([t // GRID_W, t % GRID_W], axis=-1).astype(F32)

    def tables(dh):
        quarter = dh // 4
        inv_freq = ROPE_THETA ** (-jnp.arange(quarter, dtype=F32) / quarter)
        ang = pos[:, :, None] * inv_freq
        cos, sin = jnp.cos(ang), jnp.sin(ang)
        c = jnp.concatenate([cos[:, 0], cos[:, 0], cos[:, 1], cos[:, 1]], axis=-1)
        s = jnp.concatenate([-sin[:, 0], sin[:, 0], -sin[:, 1], sin[:, 1]], axis=-1)
        reps = LANE // dh
        c, s = jnp.tile(c, (1, reps)), jnp.tile(s, (1, reps))
        return (jnp.concatenate([jnp.ones((lc, LANE), F32), c], axis=0),
                jnp.concatenate([jnp.zeros((lc, LANE), F32), s], axis=0))

    return tables(HEAD_DIM) + tables(DIFF_QK_DIM)


def kernel(x, c, ctx, c_ctx, w_ada, b_ada, norm_g, w_in, w_out, qk_g_na, na_rpb, qk_g_diff, diff_lambda,
           diff_subln_g, qk_g_swa, swa_sink, lru_conv_w, lru_conv_b, lru_gate_w, lru_gate_b, lru_lam,
           w_router, w1, w3, w2):
    n_batch, n, d = x.shape
    lc = ctx.shape[1]
    depth = w_ada.shape[0]
    assert n_batch < MOD_ROWS and w_in.shape[2] == D_IN

    cvec = jnp.zeros((MOD_ROWS, d), F32).at[:n_batch].set(c).at[n_batch].set(c_ctx)
    mod = _ada(cvec, w_ada, b_ada)
    cs, ss, cd, sd = _rope_tables(n, lc)
    xs = jnp.concatenate([ctx, x], axis=1)

    cap_lat = CAPACITY_FACTOR * n // N_EXPERTS
    cap_ctx = CAPACITY_FACTOR * lc // N_EXPERTS

    for l in range(depth):
        modl = mod[l]
        u = _inproj(xs, norm_g[l, 0].reshape(1, d), modl, w_in[l].astype(BF16), lc)

        pvec = jnp.concatenate([qk_g_na[l], jnp.tile(qk_g_diff[l], (1, 2)), qk_g_swa[l],
                                diff_subln_g[l].reshape(1, LANE), jnp.zeros((1, LANE), F32)], axis=0)
        qa, ka, va, qd, kd, vd, qs, ks, vs = _prep(u, pvec, cs, ss, cd, sd)

        o_a = _na(qa, ka, va, _na_bias_tables(na_rpb[l], n), lc)

        lambda_init = 0.8 - 0.6 * math.exp(-0.3 * l)
        dl = diff_lambda[l].astype(F32)
        lam = (jnp.exp(jnp.sum(dl[0] * dl[1])) - jnp.exp(jnp.sum(dl[2] * dl[3])) + lambda_init).reshape(1)
        o_d = _diff(lam, qd, kd, vd, pvec, lc, 1.0 - lambda_init)

        o_s = _swa(qs, ks, vs, swa_sink[l], lc)

        af, bf, ab, bb = _lru_gates(u, lru_conv_w[l], lru_conv_b[l], lru_gate_w[l], lru_gate_b[l], lru_lam[l], lc)
        hf, hb = _lru_scan(af, bf, ab, bb, lc)

        xs = _outproj(o_a, o_d, o_s, hf, hb, u, modl, w_out[l].astype(BF16), xs, lc)

        wr_pad = jnp.zeros((d, LANE), F32).at[:, :N_EXPERTS].set(w_router[l])
        xm, affp = _router(xs, norm_g[l, 1].reshape(1, d), modl, wr_pad, lc)
        w1b, w3b, w2b = w1[l].astype(BF16), w3[l].astype(BF16), w2[l].astype(BF16)
        lt, cnt, off = _select(affp, lc, n, cap_lat)
        xs = _moe(lt, cnt, off, xm, affp, xs, modl, w1b, w3b, w2b, lc, cap_lat, None)
        if l < depth - 1:
            lt, cnt, off = _select(affp, 0, lc, cap_ctx)
            xs = _moe(lt, cnt, off, xm, affp, xs, modl, w1b, w3b, w2b, 0, cap_ctx, n_batch)
    return xs[:, lc:, :]
```

```python
import functools
import math

import numpy as np
import jax
import jax.numpy as jnp
from jax import lax
from jax.experimental import pallas as pl
from jax.experimental.pallas import tpu as pltpu

GRID_W = 64
HEAD_DIM = 128
NA_HEADS = 8
NA_KH = 8
NA_KW = 16
DIFF_HEADS = 8
DIFF_QK_DIM = 64
SWA_Q_HEADS = 8
SWA_KV_HEADS = 2
SWA_GROUP = SWA_Q_HEADS // SWA_KV_HEADS
SWA_WINDOW = 128
LRU_WIDTH = 1024
LRU_BLOCKS = 8
LRU_C = 8.0
N_EXPERTS = 16
EXPERT_FF = 256
CAPACITY_FACTOR = 2
ROPE_THETA = 10000.0
NEG_INF = -1e30
EPS = 1e-6

OFF_QA, OFF_KA, OFF_VA = 0, 1024, 2048
OFF_QD, OFF_KD, OFF_VD = 3072, 4096, 5120
OFF_QS, OFF_KS, OFF_VS = 6144, 7168, 7424
OFF_XR, OFF_GR = 7680, 8704
D_ATT = 7680
D_IN = 9728
D_MIX = 4096

LOG2E = 1.4426950408889634
DIFF_Q_SCALE = DIFF_QK_DIM ** -0.5 * LOG2E
DIFF_FAST_BOUND = 60.0

LANE = 128
MOD_ROWS = 8
V7X_VMEM_BYTES = 64 * 1024 * 1024

F32 = jnp.float32
BF16 = jnp.bfloat16
I32 = jnp.int32


def _tile(n, target, mult=LANE):
    best = None
    for t in range(mult, min(n, target) + 1, mult):
        if n % t == 0:
            best = t
    assert best is not None, (n, target, mult)
    return best


def _params(sem, vmem_mb):
    return pltpu.CompilerParams(dimension_semantics=sem, vmem_limit_bytes=min(vmem_mb, 60) * 1024 * 1024)


def _dot_nt(a, b):
    return lax.dot_general(a, b, (((1,), (1,)), ((), ())), preferred_element_type=F32)


def _dot(a, b):
    return jnp.dot(a, b, preferred_element_type=F32)


def _ada_kernel(c_ref, w_ref, b_ref, o_ref):
    cv = c_ref[...]
    s = cv * jax.nn.sigmoid(cv)
    o_ref[...] = jnp.dot(s, w_ref[...], preferred_element_type=F32,
                         precision=lax.Precision.HIGHEST) + b_ref[...]


def _ada(cvec, w_ada, b_ada):
    depth, d, n6 = w_ada.shape
    tn = _tile(n6, 512)
    return pl.pallas_call(
        _ada_kernel,
        grid=(depth, n6 // tn),
        in_specs=[pl.BlockSpec((MOD_ROWS, d), lambda l, j: (0, 0)),
                  pl.BlockSpec((None, d, tn), lambda l, j: (l, 0, j)),
                  pl.BlockSpec((None, 1, tn), lambda l, j: (l, 0, j))],
        out_specs=pl.BlockSpec((None, MOD_ROWS, tn), lambda l, j: (l, 0, j)),
        out_shape=jax.ShapeDtypeStruct((depth, MOD_ROWS, n6), F32),
        compiler_params=_params(("arbitrary", "arbitrary"), 40),
        name="ada_mod",
    )(cvec, w_ada, b_ada.reshape(depth, 1, n6))


ROW_CHUNK = 32


def _mod_row(ref, row0, b, n_batch, lc):
    return jnp.where(row0 < lc, ref[pl.ds(n_batch, 1), :], ref[pl.ds(b, 1), :])


def _norm_modulate(x, g, sh, sc):
    ms = jnp.mean(x * x, axis=-1, keepdims=True)
    return x * lax.rsqrt(ms + EPS) * g * (1.0 + sc) + sh


def _for_row_chunks(tm, fn):
    assert tm % ROW_CHUNK == 0

    def body(c, _):
        fn(pl.multiple_of(c * ROW_CHUNK, ROW_CHUNK))
        return 0

    lax.fori_loop(0, tm // ROW_CHUNK, body, 0)


def _inproj_kernel(x_ref, g_ref, sh_ref, sc_ref, w_ref, o_ref, xn_ref, *, n_batch, lc, tm):
    b = pl.program_id(0)
    i = pl.program_id(1)

    @pl.when(pl.program_id(2) == 0)
    def _():
        def chunk(r0):
            sh = _mod_row(sh_ref, i * tm + r0, b, n_batch, lc)
            sc = _mod_row(sc_ref, i * tm + r0, b, n_batch, lc)
            xn = _norm_modulate(x_ref[pl.ds(r0, ROW_CHUNK), :], g_ref[...], sh, sc)
            xn_ref[pl.ds(r0, ROW_CHUNK), :] = xn.astype(BF16)

        _for_row_chunks(tm, chunk)

    o_ref[...] = _dot(xn_ref[...], w_ref[...])


def _inproj(xs, g, modl, w_bf, lc):
    n_batch, nt, d = xs.shape
    d_in = w_bf.shape[1]
    tm = _tile(nt, 768)
    tn = _tile(d_in, 512)
    assert lc % ROW_CHUNK == 0
    kern = functools.partial(_inproj_kernel, n_batch=n_batch, lc=lc, tm=tm)
    return pl.pallas_call(
        kern,
        grid=(n_batch, nt // tm, d_in // tn),
        in_specs=[pl.BlockSpec((None, tm, d), lambda b, i, j: (b, i, 0)),
                  pl.BlockSpec((1, d), lambda b, i, j: (0, 0)),
                  pl.BlockSpec((MOD_ROWS, d), lambda b, i, j: (0, 0)),
                  pl.BlockSpec((MOD_ROWS, d), lambda b, i, j: (0, 1)),
                  pl.BlockSpec((d, tn), lambda b, i, j: (0, j))],
        out_specs=pl.BlockSpec((None, tm, tn), lambda b, i, j: (b, i, j)),
        out_shape=jax.ShapeDtypeStruct((n_batch, nt, d_in), F32),
        scratch_shapes=[pltpu.VMEM((tm, d), BF16)],
        compiler_params=_params(("arbitrary", "arbitrary", "arbitrary"), 56),
        name="in_proj",
    )(xs, g, modl, modl, w_bf)


def _rms_full(x, g):
    ms = jnp.mean(x * x, axis=-1, keepdims=True)
    return x * lax.rsqrt(ms + EPS) * g


def _rms_halves(x, g):
    lo = lax.broadcasted_iota(I32, (1, LANE), 1) < DIFF_QK_DIM
    x2 = x * x
    s_lo = jnp.sum(jnp.where(lo, x2, 0.0), axis=-1, keepdims=True)
    s_hi = jnp.sum(jnp.where(lo, 0.0, x2), axis=-1, keepdims=True)
    ms = jnp.where(lo, s_lo, s_hi) * (1.0 / DIFF_QK_DIM)
    return x * lax.rsqrt(ms + EPS) * g


def _rope(x, cos, sin_signed, half):
    lane = lax.broadcasted_iota(I32, (1, LANE), 1)
    up = pltpu.roll(x, LANE - half, axis=1)
    dn = pltpu.roll(x, half, axis=1)
    partner = jnp.where((lane & half) == 0, up, dn)
    return x * cos + partner * sin_signed


def _prep_kernel(u_ref, p_ref, cs_ref, ss_ref, cd_ref, sd_ref,
                 qa_ref, ka_ref, va_ref, qd_ref, kd_ref, vd_ref, qs_ref, ks_ref, vs_ref):
    g_na_q, g_na_k = p_ref[0:1, :], p_ref[1:2, :]
    g_d_q, g_d_k = p_ref[2:3, :], p_ref[3:4, :]
    g_s_q, g_s_k = p_ref[4:5, :], p_ref[5:6, :]
    cs, ss, cd, sd = cs_ref[...], ss_ref[...], cd_ref[...], sd_ref[...]
    lo = lax.broadcasted_iota(I32, (1, LANE), 1) < DIFF_QK_DIM

    def col(off, h):
        return u_ref[:, off + h * LANE: off + (h + 1) * LANE]

    def put(ref, h, val):
        ref[:, h * LANE:(h + 1) * LANE] = val.astype(BF16)

    for h in range(NA_HEADS):
        put(qa_ref, h, _rms_full(col(OFF_QA, h), g_na_q))
        put(ka_ref, h, _rms_full(col(OFF_KA, h), g_na_k))
        put(va_ref, h, col(OFF_VA, h))
    ones_col = jnp.where(lax.broadcasted_iota(I32, (u_ref.shape[0], LANE), 1) == 0, 1.0, 0.0).astype(BF16)
    for h in range(DIFF_HEADS):
        q = _rope(_rms_halves(col(OFF_QD, h), g_d_q), cd, sd, DIFF_QK_DIM // 4) * DIFF_Q_SCALE
        qd_ref[0, :, h * LANE:(h + 1) * LANE] = jnp.where(lo, q, 0.0).astype(BF16)
        qd_ref[1, :, h * LANE:(h + 1) * LANE] = jnp.where(lo, 0.0, q).astype(BF16)
        put(kd_ref, h, _rope(_rms_halves(col(OFF_KD, h), g_d_k), cd, sd, DIFF_QK_DIM // 4))
        put(vd_ref, 2 * h, col(OFF_VD, h))
        vd_ref[:, (2 * h + 1) * LANE:(2 * h + 2) * LANE] = ones_col
    for h in range(SWA_Q_HEADS):
        put(qs_ref, h, _rope(_rms_full(col(OFF_QS, h), g_s_q), cs, ss, HEAD_DIM // 4))
    for h in range(SWA_KV_HEADS):
        put(ks_ref, h, _rope(_rms_full(col(OFF_KS, h), g_s_k), cs, ss, HEAD_DIM // 4))
        put(vs_ref, h, col(OFF_VS, h))


def _prep(u, pvec, cs, ss, cd, sd):
    n_batch, nt, _ = u.shape
    tr = _tile(nt, 256)
    row = lambda w: pl.BlockSpec((None, tr, w), lambda b, i: (b, i, 0))
    tab = pl.BlockSpec((tr, LANE), lambda b, i: (i, 0))
    sd_ = lambda w: jax.ShapeDtypeStruct((n_batch, nt, w), BF16)
    return pl.pallas_call(
        _prep_kernel,
        grid=(n_batch, nt // tr),
        in_specs=[row(D_ATT), pl.BlockSpec((8, LANE), lambda b, i: (0, 0)), tab, tab, tab, tab],
        out_specs=[row(1024), row(1024), row(1024),
                   pl.BlockSpec((None, 2, tr, 1024), lambda b, i: (b, 0, i, 0)),
                   row(1024), row(2048), row(1024), row(256), row(256)],
        out_shape=[sd_(1024), sd_(1024), sd_(1024),
                   jax.ShapeDtypeStruct((n_batch, 2, nt, 1024), BF16),
                   sd_(1024), sd_(2048), sd_(1024), sd_(256), sd_(256)],
        compiler_params=_params(("arbitrary", "arbitrary"), 48),
        name="qkv_prep",
    )(u, pvec, cs, ss, cd, sd)


NA_QBLK = 2 * GRID_W
NA_KROWS = 10
NA_KBLK = NA_KROWS * GRID_W


def _na_kernel(q_ref, k_ref, v_ref, bias_ref, o_ref, *, lc, nlb):
    i = pl.program_id(2)
    li = i - lc // NA_QBLK
    kb = jnp.clip(li - 2, 0, nlb - NA_KROWS // 2)
    start = pl.multiple_of(lc + kb * NA_QBLK, NA_QBLK)
    scale = HEAD_DIM ** -0.5
    q = q_ref[...]
    kw = k_ref[pl.ds(start, NA_KBLK), :]
    vw = v_ref[pl.ds(start, NA_KBLK), :]
    kc = k_ref[0:lc, :]
    vc = v_ref[0:lc, :]
    s_n = _dot_nt(q, kw) * scale + bias_ref[...]
    s_c = _dot_nt(q, kc) * scale
    m = jnp.maximum(jnp.max(s_n, axis=-1, keepdims=True), jnp.max(s_c, axis=-1, keepdims=True))
    p_n = jnp.exp(s_n - m)
    p_c = jnp.exp(s_c - m)
    l = jnp.sum(p_n, axis=-1, keepdims=True) + jnp.sum(p_c, axis=-1, keepdims=True)
    o = _dot(p_n.astype(BF16), vw) + _dot(p_c.astype(BF16), vc)
    o_ref[...] = (o / l).astype(BF16)


def _na_bias_tables(rpb, n):
    rows = n // GRID_W
    nlb = rows // 2
    reps = [0, 1, 2, nlb - 2, nlb - 1]
    h = rpb.shape[0]
    cols = np.arange(GRID_W)
    cstart = np.clip(cols - NA_KW // 2, 0, GRID_W - NA_KW)
    col_ok = (cols[None, :] >= cstart[:, None]) & (cols[None, :] < cstart[:, None] + NA_KW)
    dx = cols[None, :] - cols[:, None] + (NA_KW - 1)
    sel = (dx[None] == np.arange(2 * NA_KW - 1)[:, None, None]) & col_ok[None]
    tx = jnp.einsum('hyx,xqk->hyqk', rpb.astype(F32), jnp.asarray(sel, F32), precision=lax.Precision.HIGHEST)
    dy_i = np.zeros((5, 2, NA_KROWS), np.int32)
    row_ok = np.zeros((5, 2, NA_KROWS), bool)
    for c, li in enumerate(reps):
        kb = int(np.clip(li - 2, 0, nlb - NA_KROWS // 2))
        qr = 2 * li + np.arange(2)
        kr = 2 * kb + np.arange(NA_KROWS)
        rs = np.clip(qr - NA_KH // 2, 0, rows - NA_KH)
        row_ok[c] = (kr[None, :] >= rs[:, None]) & (kr[None, :] < rs[:, None] + NA_KH)
        dy_i[c] = np.clip(kr[None, :] - qr[:, None] + (NA_KH - 1), 0, 2 * NA_KH - 2)
    blocks = jnp.take(tx, jnp.asarray(dy_i.reshape(-1)), axis=1)
    blocks = blocks.reshape(h, 5, 2, NA_KROWS, GRID_W, GRID_W)
    ok = row_ok[:, :, :, None, None] & col_ok[None, None, None]
    tbl = jnp.where(ok[None], blocks, NEG_INF)
    tbl = tbl.transpose(1, 0, 2, 4, 3, 5).reshape(5, h, NA_QBLK, NA_KBLK)
    dead = jnp.full((1,) + tbl.shape[1:], NEG_INF, F32)
    return jnp.concatenate([tbl, dead], axis=0)


def _na(qa, ka, va, bias, lc):
    n_batch, nt, _ = qa.shape
    n = nt - lc
    nlb = n // NA_QBLK
    assert nlb >= 5 and lc % NA_QBLK == 0
    ncb = lc // NA_QBLK

    def case(i):
        li = i - ncb
        return jnp.where(li < 0, 5, jnp.where(li < 2, li, jnp.where(li >= nlb - 2, li - (nlb - 5), 2)))

    kern = functools.partial(_na_kernel, lc=lc, nlb=nlb)
    kv = pl.BlockSpec((None, nt, LANE), lambda b, h, i: (b, 0, h))
    return pl.pallas_call(
        kern,
        grid=(n_batch, NA_HEADS, nt // NA_QBLK),
        in_specs=[pl.BlockSpec((None, NA_QBLK, LANE), lambda b, h, i: (b, i, h)), kv, kv,
                  pl.BlockSpec((None, None, NA_QBLK, NA_KBLK), lambda b, h, i: (case(i), h, 0, 0))],
        out_specs=pl.BlockSpec((None, NA_QBLK, LANE), lambda b, h, i: (b, i, h)),
        out_shape=jax.ShapeDtypeStruct((n_batch, nt, NA_HEADS * HEAD_DIM), BF16),
        compiler_params=_params(("arbitrary",) * 3, 32),
        name="na_attn",
    )(qa, ka, va, bias)


SWA_BLK = 128
SWA_WIN = 3 * SWA_BLK


def _swa_kernel(q_ref, k_ref, v_ref, sink_ref, o_ref, *, lc, n):
    i = pl.program_id(2)
    li = i - lc // SWA_BLK
    ws = jnp.clip((li - 1) * SWA_BLK, 0, n - SWA_WIN)
    start = pl.multiple_of(lc + ws, SWA_BLK)
    scale = HEAD_DIM ** -0.5
    kw = k_ref[pl.ds(start, SWA_WIN), :]
    vw = v_ref[pl.ds(start, SWA_WIN), :]
    kc = k_ref[0:lc, :]
    vc = v_ref[0:lc, :]
    r = lax.broadcasted_iota(I32, (SWA_BLK, SWA_WIN), 0)
    j = lax.broadcasted_iota(I32, (SWA_BLK, SWA_WIN), 1)
    mask = (jnp.abs(ws + j - (li * SWA_BLK + r)) <= SWA_WINDOW) & (li >= 0)
    for g in range(SWA_GROUP):
        q = q_ref[:, g * LANE:(g + 1) * LANE]
        s_w = jnp.where(mask, _dot_nt(q, kw) * scale, NEG_INF)
        s_c = _dot_nt(q, kc) * scale
        sink = sink_ref[g * SWA_BLK:(g + 1) * SWA_BLK, 0:1]
        m = jnp.maximum(jnp.maximum(jnp.max(s_w, axis=-1, keepdims=True), jnp.max(s_c, axis=-1, keepdims=True)),
                        sink)
        p_w = jnp.exp(s_w - m)
        p_c = jnp.exp(s_c - m)
        l = jnp.sum(p_w, axis=-1, keepdims=True) + jnp.sum(p_c, axis=-1, keepdims=True) + jnp.exp(sink - m)
        o = (_dot(p_w.astype(BF16), vw) + _dot(p_c.astype(BF16), vc)) / l
        o_ref[:, g * LANE:(g + 1) * LANE] = o.astype(BF16)


def _swa(qs, ks, vs, sink, lc):
    n_batch, nt, _ = qs.shape
    n = nt - lc
    assert n >= SWA_WIN and lc % SWA_BLK == 0
    rows = SWA_GROUP * SWA_BLK
    sink_rows = jnp.broadcast_to(
        jnp.repeat(sink.astype(F32).reshape(SWA_KV_HEADS, SWA_GROUP), SWA_BLK, axis=1)[:, :, None],
        (SWA_KV_HEADS, rows, LANE))
    kern = functools.partial(_swa_kernel, lc=lc, n=n)
    kv = pl.BlockSpec((None, nt, LANE), lambda b, k, i: (b, 0, k))
    qo = pl.BlockSpec((None, SWA_BLK, SWA_GROUP * LANE), lambda b, k, i: (b, i, k))
    return pl.pallas_call(
        kern,
        grid=(n_batch, SWA_KV_HEADS, nt // SWA_BLK),
        in_specs=[qo, kv, kv, pl.BlockSpec((None, rows, LANE), lambda b, k, i: (k, 0, 0))],
        out_specs=qo,
        out_shape=jax.ShapeDtypeStruct((n_batch, nt, SWA_Q_HEADS * HEAD_DIM), BF16),
        compiler_params=_params(("arbitrary",) * 3, 32),
        name="swa_attn",
    )(qs, ks, vs, sink_rows)


def _diff_finish(sc_ref, g_ref, o_ref, num, den, tq, post_scale):
    o2 = num / den
    o = o2[0:tq, :] - sc_ref[0] * o2[tq:2 * tq, :]
    o_ref[...] = (_rms_full(o, g_ref[6:7, :]) * post_scale).astype(BF16)


def _diff_online_kernel(sc_ref, q_ref, k_ref, v_ref, g_ref, o_ref, *, lc, nt, tq, tk, post_scale):
    qi = pl.program_id(2)
    q = q_ref[...].reshape(2 * tq, LANE)
    nk = jnp.where(qi < lc // tq, lc // tk, nt // tk)

    def body(kt, carry):
        m, l, acc = carry
        k0 = pl.multiple_of(kt * tk, tk)
        k = k_ref[pl.ds(k0, tk), :]
        v = v_ref[pl.ds(k0, tk), :]
        s = _dot_nt(q, k)
        m_new = jnp.maximum(m, jnp.max(s, axis=-1, keepdims=True))
        alpha = jnp.exp2(m - m_new)
        p = jnp.exp2(s - m_new)
        l = alpha * l + jnp.sum(p, axis=-1, keepdims=True)
        acc = alpha * acc + _dot(p.astype(BF16), v)
        return m_new, l, acc

    init = (jnp.full((2 * tq, 1), NEG_INF, F32), jnp.zeros((2 * tq, 1), F32), jnp.zeros((2 * tq, LANE), F32))
    _, l, acc = lax.fori_loop(0, nk, body, init)
    _diff_finish(sc_ref, g_ref, o_ref, acc, l, tq, post_scale)


def _diff_offset_kernel(sc_ref, q_ref, k_ref, v_ref, g_ref, o_ref, acc_ref, p_ref, *, lc, nt, tq, tk, post_scale):
    qi = pl.program_id(2)
    rows2 = 2 * tq
    q = q_ref[...].reshape(rows2, LANE)
    off = sc_ref[1]

    def probs(k0, tkk):
        return jnp.exp2(_dot_nt(q, k_ref[k0:k0 + tkk, :]) - off).astype(BF16)

    @pl.when(qi < lc // tq)
    def _():
        acc_ref[...] = _dot(probs(0, lc), v_ref[0:lc, :])

    @pl.when(qi >= lc // tq)
    def _():
        nk = nt // tk
        p_ref[0] = probs(0, tk)
        tot = None
        for t in range(nk):
            pv = _dot(p_ref[t % 2], v_ref[t * tk:(t + 1) * tk, :])
            if t + 1 < nk:
                p_ref[(t + 1) % 2] = probs((t + 1) * tk, tk)
            tot = pv if tot is None else tot + pv
        acc_ref[...] = tot

    acc = acc_ref[...]
    _diff_finish(sc_ref, g_ref, o_ref, acc[:, 0:LANE], acc[:, LANE:LANE + 1], tq, post_scale)


def _diff(sc, qd, kd, vde, pvec, lc, post_scale):
    n_batch, _, nt, _ = qd.shape
    tq = _tile(math.gcd(lc, nt), 256)
    assert lc % tq == 0
    grid = (n_batch, DIFF_HEADS, nt // tq)
    smem = pl.BlockSpec(memory_space=pltpu.SMEM)
    qspec = pl.BlockSpec((None, 2, tq, LANE), lambda b, h, i: (b, 0, i, h))
    kspec = pl.BlockSpec((None, nt, LANE), lambda b, h, i: (b, 0, h))
    gspec = pl.BlockSpec((8, LANE), lambda b, h, i: (0, 0))
    ospec = pl.BlockSpec((None, tq, LANE), lambda b, h, i: (b, i, h))
    oshape = jax.ShapeDtypeStruct((n_batch, nt, DIFF_HEADS * HEAD_DIM), BF16)

    def offset(sc, qd, kd, vde, pvec):
        tk = _tile(nt, 768)
        kern = functools.partial(_diff_offset_kernel, lc=lc, nt=nt, tq=tq, tk=tk, post_scale=post_scale)
        return pl.pallas_call(
            kern, grid=grid,
            in_specs=[smem, qspec, kspec, pl.BlockSpec((None, nt, 2 * LANE), lambda b, h, i: (b, 0, h)), gspec],
            out_specs=ospec, out_shape=oshape,
            scratch_shapes=[pltpu.VMEM((2 * tq, 2 * LANE), F32), pltpu.VMEM((2, 2 * tq, tk), BF16)],
            compiler_params=_params(("arbitrary",) * 3, 40),
            name="diff_attn_offset",
        )(sc, qd, kd, vde, pvec)

    def online(sc, qd, kd, vde, pvec):
        tk = _tile(math.gcd(lc, nt), 256)
        kern = functools.partial(_diff_online_kernel, lc=lc, nt=nt, tq=tq, tk=tk, post_scale=post_scale)
        return pl.pallas_call(
            kern, grid=grid,
            in_specs=[smem, qspec, kspec, pl.BlockSpec((None, nt, LANE), lambda b, h, i: (b, 0, 2 * h)), gspec],
            out_specs=ospec, out_shape=oshape,
            compiler_params=_params(("arbitrary",) * 3, 32),
            name="diff_attn_online",
        )(sc, qd, kd, vde, pvec)

    return lax.cond(sc[1] <= DIFF_FAST_BOUND, offset, online, sc, qd, kd, vde, pvec)


LRU_HALO = 8


def _lru_gate_kernel(xp_ref, x_ref, xn_ref, cw_ref, cb_ref, gw_ref, gb_ref, lam_ref,
                     af_ref, bf_ref, ab_ref, bb_ref, *, t, seg_starts, seg_ends):
    k = pl.program_id(2)
    is_start = functools.reduce(jnp.logical_or, [k == s for s in seg_starts])
    is_end = functools.reduce(jnp.logical_or, [k == s for s in seg_ends])
    cur = x_ref[...]
    prev = jnp.where(is_start, 0.0, xp_ref[...])
    nxt = jnp.where(is_end, 0.0, xn_ref[...])
    ext = jnp.concatenate([prev, cur, nxt], axis=0)
    h = LRU_HALO
    xl = (cb_ref[...] + cw_ref[0:1, :] * ext[h - 2:h - 2 + t, :] + cw_ref[1:2, :] * ext[h - 1:h - 1 + t, :]
          + cw_ref[2:3, :] * cur + cw_ref[3:4, :] * ext[h + 1:h + 1 + t, :])
    xb = xl.astype(BF16)
    outs = ((af_ref, bf_ref), (ab_ref, bb_ref))
    for d in range(2):
        r = jax.nn.sigmoid(_dot(xb, gw_ref[d, 0].astype(BF16)) + gb_ref[2 * d:2 * d + 1, :])
        ig = jax.nn.sigmoid(_dot(xb, gw_ref[d, 1].astype(BF16)) + gb_ref[2 * d + 1:2 * d + 2, :])
        log_a = LRU_C * r * jax.nn.log_sigmoid(lam_ref[d:d + 1, :])
        a = jnp.exp(log_a)
        outs[d][0][...] = a
        outs[d][1][...] = jnp.sqrt(1.0 - jnp.exp(2.0 * log_a)) * (ig * xl)


def _lru_gates(u, conv_w, conv_b, gate_w, gate_b, lam, lc):
    n_batch, nt, _ = u.shape
    t = _tile(math.gcd(lc, nt), 256)
    nk = nt // t
    hb = t // LRU_HALO
    seg_starts = (0, lc // t)
    seg_ends = (lc // t - 1, nk - 1)
    cb0 = OFF_XR // LANE
    kern = functools.partial(_lru_gate_kernel, t=t, seg_starts=seg_starts, seg_ends=seg_ends)
    out = pl.BlockSpec((None, t, LANE), lambda b, c, k: (b, k, c))
    osd = jax.ShapeDtypeStruct((n_batch, nt, LRU_WIDTH), F32)
    return pl.pallas_call(
        kern,
        grid=(n_batch, LRU_BLOCKS, nk),
        in_specs=[pl.BlockSpec((None, LRU_HALO, LANE), lambda b, c, k: (b, jnp.maximum(k * hb - 1, 0), cb0 + c)),
                  pl.BlockSpec((None, t, LANE), lambda b, c, k: (b, k, cb0 + c)),
                  pl.BlockSpec((None, LRU_HALO, LANE),
                               lambda b, c, k: (b, jnp.minimum((k + 1) * hb, nt // LRU_HALO - 1), cb0 + c)),
                  pl.BlockSpec((4, LANE), lambda b, c, k: (0, c)),
                  pl.BlockSpec((1, LANE), lambda b, c, k: (0, c)),
                  pl.BlockSpec((2, 2, None, LANE, LANE), lambda b, c, k: (0, 0, c, 0, 0)),
                  pl.BlockSpec((4, LANE), lambda b, c, k: (0, c)),
                  pl.BlockSpec((2, LANE), lambda b, c, k: (0, c))],
        out_specs=[out, out, out, out],
        out_shape=[osd, osd, osd, osd],
        compiler_params=_params(("arbitrary",) * 3, 24),
        name="lru_gates",
    )(u, u, u, conv_w, conv_b.reshape(1, LRU_WIDTH), gate_w, gate_b.reshape(4, LRU_WIDTH), lam)


def _lru_scan_kernel(af_ref, bf_ref, ab_ref, bb_ref, hf_ref, hb_ref, cf_ref, cbk_ref, *, t):
    @pl.when(pl.program_id(1) == 0)
    def _():
        cf_ref[...] = jnp.zeros_like(cf_ref)
        cbk_ref[...] = jnp.zeros_like(cbk_ref)

    ng = t // 8

    def body(g, carry):
        hf, hb = carry
        r0 = pl.multiple_of(g * 8, 8)
        a = af_ref[pl.ds(r0, 8), :]
        bt = bf_ref[pl.ds(r0, 8), :]
        rows = []
        for r in range(8):
            hf = a[r:r + 1, :] * hf + bt[r:r + 1, :]
            rows.append(hf)
        hf_ref[pl.ds(r0, 8), :] = jnp.concatenate(rows, axis=0)
        r1 = pl.multiple_of((ng - 1 - g) * 8, 8)
        a = ab_ref[pl.ds(r1, 8), :]
        bt = bb_ref[pl.ds(r1, 8), :]
        rows = []
        for r in range(7, -1, -1):
            hb = a[r:r + 1, :] * hb + bt[r:r + 1, :]
            rows.append(hb)
        hb_ref[pl.ds(r1, 8), :] = jnp.concatenate(rows[::-1], axis=0)
        return hf, hb

    hf, hb = lax.fori_loop(0, ng, body, (cf_ref[...], cbk_ref[...]))
    cf_ref[...] = hf
    cbk_ref[...] = hb


def _lru_scan(af, bf, ab, bb, lc):
    n_batch, nt, w = af.shape
    t = lc
    assert nt % t == 0
    nk = nt // t
    kern = functools.partial(_lru_scan_kernel, t=t)
    fwd = pl.BlockSpec((None, t, w), lambda b, k: (b, k, 0))
    bwd = pl.BlockSpec((None, t, w), lambda b, k: (b, jnp.where(k == 0, 0, nk - k), 0))
    osd = jax.ShapeDtypeStruct((n_batch, nt, w), F32)
    return pl.pallas_call(
        kern,
        grid=(n_batch, nk),
        in_specs=[fwd, fwd, bwd, bwd],
        out_specs=[fwd, bwd],
        out_shape=[osd, osd],
        scratch_shapes=[pltpu.VMEM((1, w), F32), pltpu.VMEM((1, w), F32)],
        compiler_params=_params(("arbitrary", "arbitrary"), 24),
        name="lru_scan",
    )(af, bf, ab, bb)


def _outproj_kernel(oa_ref, od_ref, os_ref, hf_ref, hb_ref, g0_ref, g1_ref, gate_ref, w_ref, x_ref,
                    o_ref, lhs_ref, *, n_batch, lc, tm):
    b = pl.program_id(0)
    i = pl.program_id(1)

    @pl.when(pl.program_id(2) == 0)
    def _():
        def chunk(r0):
            rs = pl.ds(r0, ROW_CHUNK)
            lhs_ref[rs, 0:1024] = oa_ref[rs, :]
            lhs_ref[rs, 1024:2048] = od_ref[rs, :]
            lhs_ref[rs, 2048:3072] = os_ref[rs, :]
            for c, g_ref in enumerate((g0_ref, g1_ref)):
                cs = slice(c * 512, (c + 1) * 512)
                y = (hf_ref[rs, cs] + hb_ref[rs, cs]) * jax.nn.gelu(g_ref[rs, :])
                lhs_ref[rs, 3072 + c * 512:3584 + c * 512] = y.astype(BF16)

        _for_row_chunks(tm, chunk)

    row = i * tm + lax.broadcasted_iota(I32, (tm, 1), 0)
    gate = jnp.where(row < lc, gate_ref[pl.ds(n_batch, 1), :], gate_ref[pl.ds(b, 1), :])
    o_ref[...] = x_ref[...] + gate * _dot(lhs_ref[...], w_ref[...])


def _outproj(o_a, o_d, o_s, hf, hb, u, modl, w_bf, xs, lc):
    n_batch, nt, d = xs.shape
    tm = _tile(nt, 768)
    tn = _tile(d, 512)
    ntile = d // tn
    kern = functools.partial(_outproj_kernel, n_batch=n_batch, lc=lc, tm=tm)
    mix = pl.BlockSpec((None, tm, 1024), lambda b, i, j: (b, i, 0))
    gr = lambda c: pl.BlockSpec((None, tm, 512), lambda b, i, j: (b, i, OFF_GR // 512 + c))
    xblk = pl.BlockSpec((None, tm, tn), lambda b, i, j: (b, i, j))
    return pl.pallas_call(
        kern,
        grid=(n_batch, nt // tm, ntile),
        in_specs=[mix, mix, mix, mix, mix, gr(0), gr(1),
                  pl.BlockSpec((MOD_ROWS, tn), lambda b, i, j: (0, 2 * ntile + j)),
                  pl.BlockSpec((D_MIX, tn), lambda b, i, j: (0, j)), xblk],
        out_specs=xblk,
        out_shape=jax.ShapeDtypeStruct(xs.shape, F32),
        scratch_shapes=[pltpu.VMEM((tm, D_MIX), BF16)],
        input_output_aliases={9: 0},
        compiler_params=_params(("arbitrary",) * 3, 56),
        name="out_proj",
    )(o_a, o_d, o_s, hf, hb, u, u, modl, w_bf, xs)


def _split3(x):
    hi = x.astype(BF16)
    r = x - hi.astype(F32)
    mid = r.astype(BF16)
    return hi, mid, (r - mid.astype(F32)).astype(BF16)


ROUTER_KC = 512


def _router_kernel(x_ref, g_ref, sh_ref, sc_ref, w0_ref, w1_ref, w2_ref, xm_ref, aff_ref, *, n_batch, lc, tm):
    b = pl.program_id(0)
    i = pl.program_id(1)
    d = x_ref.shape[1]

    def chunk(r0):
        sh = _mod_row(sh_ref, i * tm + r0, b, n_batch, lc)
        sc = _mod_row(sc_ref, i * tm + r0, b, n_batch, lc)
        xm_ref[pl.ds(r0, ROW_CHUNK), 0:d] = _norm_modulate(x_ref[pl.ds(r0, ROW_CHUNK), :], g_ref[...], sh, sc)

    _for_row_chunks(tm, chunk)

    logits = jnp.zeros((tm, LANE), F32)
    tk = _tile(d, ROUTER_KC)
    for kc in range(d // tk):
        ks = slice(kc * tk, (kc + 1) * tk)
        x0, x1, x2 = _split3(xm_ref[:, ks])
        w0, w1, w2 = w0_ref[ks, :], w1_ref[ks, :], w2_ref[ks, :]
        logits = logits + (_dot(x0, w0) + (_dot(x0, w1) + _dot(x1, w0))
                           + (_dot(x1, w1) + _dot(x0, w2) + _dot(x2, w0)))
    lane = lax.broadcasted_iota(I32, (1, LANE), 1)
    logits = jnp.where(lane < N_EXPERTS, logits, NEG_INF)
    m = jnp.max(logits, axis=-1, keepdims=True)
    p = jnp.exp(logits - m)
    aff = p / jnp.sum(p, axis=-1, keepdims=True)
    aff_ref[...] = aff
    xm_ref[:, d:d + LANE] = aff


def _router(xs, g, modl, wr_pad, lc):
    n_batch, nt, d = xs.shape
    tm = _tile(nt, 256)
    assert lc % ROW_CHUNK == 0
    kern = functools.partial(_router_kernel, n_batch=n_batch, lc=lc, tm=tm)
    wspec = pl.BlockSpec((d, LANE), lambda b, i: (0, 0))
    return pl.pallas_call(
        kern,
        grid=(n_batch, nt // tm),
        in_specs=[pl.BlockSpec((None, tm, d), lambda b, i: (b, i, 0)),
                  pl.BlockSpec((1, d), lambda b, i: (0, 0)),
                  pl.BlockSpec((MOD_ROWS, d), lambda b, i: (0, 3)),
                  pl.BlockSpec((MOD_ROWS, d), lambda b, i: (0, 4)),
                  wspec, wspec, wspec],
        out_specs=[pl.BlockSpec((None, tm, d + LANE), lambda b, i: (b, i, 0)),
                   pl.BlockSpec((None, tm, LANE), lambda b, i: (b, i, 0))],
        out_shape=[jax.ShapeDtypeStruct((n_batch, nt, d + LANE), F32),
                   jax.ShapeDtypeStruct((n_batch, nt, LANE), F32)],
        compiler_params=_params(("arbitrary", "arbitrary"), 48),
        name="norm2_router",
    )(xs, g, modl, modl, *_split3(wr_pad))


def _select_kernel(aff_ref, l_ref, cnt_ref, off_ref, bits_ref, *, row0, n, cap):
    nc = n // LANE
    bits_ref[...] = lax.bitcast_convert_type(aff_ref[row0:row0 + n, :], I32)

    def bit_step(i, thr):
        cand = thr | (jnp.int32(1) << (30 - i))
        cnt = jnp.sum((bits_ref[...] >= cand).astype(I32), axis=0, keepdims=True)
        return jnp.where(cnt >= cap, cand, thr)

    thr = lax.fori_loop(0, 31, bit_step, jnp.zeros((1, LANE), I32))
    n_gt = jnp.sum((bits_ref[...] > thr).astype(I32), axis=0, keepdims=True)
    need = (cap - n_gt).astype(F32)

    r_io = lax.broadcasted_iota(I32, (LANE, LANE), 0)
    c_io = lax.broadcasted_iota(I32, (LANE, LANE), 1)
    ltri = (c_io <= r_io).astype(BF16)
    t_col = r_io.astype(F32)
    j_row = c_io.astype(F32)

    def chunk(c, carry):
        run_eq, run_sel = carry
        r0 = pl.multiple_of(c * LANE, LANE)
        bc = bits_ref[pl.ds(r0, LANE), :]
        eq = bc == thr
        eq_f = eq.astype(F32)
        incl_eq = _dot(ltri, eq_f.astype(BF16))
        rank = run_eq + incl_eq - eq_f
        sel = (bc > thr) | (eq & (rank < need))
        sel_f = sel.astype(F32)
        incl = _dot(ltri, sel_f.astype(BF16))
        cnt_c = incl[LANE - 1:LANE, :]
        qm = jnp.where(sel, incl - sel_f, -1.0)
        for e in range(N_EXPERTS):
            pos = jnp.broadcast_to(qm[:, e:e + 1], (LANE, LANE))
            local = jnp.sum(jnp.where(pos == j_row, t_col, 0.0), axis=0, keepdims=True)
            l_ref[e, pl.ds(c, 1), :] = local.astype(I32)
        cnt_ref[pl.ds(c, 1), :] = cnt_c.astype(I32)
        off_ref[pl.ds(c, 1), :] = run_sel.astype(I32)
        return run_eq + incl_eq[LANE - 1:LANE, :], run_sel + cnt_c

    zero = jnp.zeros((1, LANE), F32)
    lax.fori_loop(0, nc, chunk, (zero, zero))


def _select(affp, row0, n, cap):
    n_batch, nt, _ = affp.shape
    nc = n // LANE
    kern = functools.partial(_select_kernel, row0=row0, n=n, cap=cap)
    return pl.pallas_call(
        kern,
        grid=(n_batch,),
        in_specs=[pl.BlockSpec((None, nt, LANE), lambda b: (b, 0, 0))],
        out_specs=[pl.BlockSpec((None, N_EXPERTS, nc, LANE), lambda b: (b, 0, 0, 0)),
                   pl.BlockSpec((None, nc, LANE), lambda b: (b, 0, 0)),
                   pl.BlockSpec((None, nc, LANE), lambda b: (b, 0, 0))],
        out_shape=[jax.ShapeDtypeStruct((n_batch, N_EXPERTS, nc, LANE), I32),
                   jax.ShapeDtypeStruct((n_batch, nc, LANE), I32),
                   jax.ShapeDtypeStruct((n_batch, nc, LANE), I32)],
        scratch_shapes=[pltpu.VMEM((n, LANE), I32)],
        compiler_params=_params(("arbitrary",), 40),
        name="ec_select",
    )(affp)


MOE_ROWS = 256


def _moe_kernel(l_hbm, cnt_hbm, off_hbm, xm_hbm, xs_in_hbm, g2_ref, w1_ref, w3_ref, w2_ref, xs_hbm,
                l_s, cnt_s, off_s, xg, rows, xb, sem, *, row0, nc, cap, g_row):
    del xs_in_hbm
    b = pl.program_id(0)
    e = pl.program_id(1)
    tables = (pltpu.make_async_copy(l_hbm.at[b, e], l_s, sem.at[0]),
              pltpu.make_async_copy(cnt_hbm.at[b], cnt_s, sem.at[1]),
              pltpu.make_async_copy(off_hbm.at[b], off_s, sem.at[2]))
    for cp in tables:
        cp.start()
    for cp in tables:
        cp.wait()

    def gather_copies(tok, slot, n_rows=1):
        return (pltpu.make_async_copy(xm_hbm.at[b, pl.ds(tok, n_rows), :], xg.at[pl.ds(slot, n_rows), :], sem.at[0]),
                pltpu.make_async_copy(xs_hbm.at[b, pl.ds(tok, n_rows), :], rows.at[pl.ds(slot, n_rows), :],
                                      sem.at[1]))

    def scatter_copy(tok, slot, n_rows=1):
        return pltpu.make_async_copy(rows.at[pl.ds(slot, n_rows), :], xs_hbm.at[b, pl.ds(tok, n_rows), :], sem.at[2])

    def for_selected(fn):
        def chunk(c, _):
            base = row0 + c * LANE
            off_c = off_s[c, e]

            def one(j, _):
                fn(base + l_s[c, j], off_c + j)
                return 0

            lax.fori_loop(0, cnt_s[c, e], one, 0)
            return 0

        lax.fori_loop(0, nc, chunk, 0)

    def start_gather(tok, slot):
        for cp in gather_copies(tok, slot):
            cp.start()

    for_selected(start_gather)
    for cp in gather_copies(row0, 0, cap):
        cp.wait()

    d = rows.shape[1]
    tn = _tile(d, 512)
    rc = min(cap, MOE_ROWS)
    lane = lax.broadcasted_iota(I32, (1, LANE), 1)
    g2_row = g_row if g_row is not None else b

    def row_block(r, _):
        rs = pl.ds(pl.multiple_of(r * rc, rc), rc)
        for kc in range(d // tn):
            ks = slice(kc * tn, (kc + 1) * tn)
            xb[:, ks] = xg[rs, ks].astype(BF16)
        h1 = _dot(xb[...], w1_ref[...])
        h3 = _dot(xb[...], w3_ref[...])
        gate = jnp.sum(jnp.where(lane == e, xg[rs, d:d + LANE], 0.0), axis=-1, keepdims=True)
        hdn = (h1 * jax.nn.sigmoid(h1) * h3 * gate).astype(BF16)
        for c in range(d // tn):
            sl = slice(c * tn, (c + 1) * tn)
            rows[rs, sl] = rows[rs, sl] + g2_ref[pl.ds(g2_row, 1), sl] * _dot(hdn, w2_ref[:, sl])
        return 0

    lax.fori_loop(0, cap // rc, row_block, 0)

    for_selected(lambda tok, slot: scatter_copy(tok, slot).start())
    scatter_copy(row0, 0, cap).wait()


def _moe(ltab, cnt, off, xma, xs, modl, w1, w3, w2, row0, cap, g_row):
    n_batch, nt, d = xs.shape
    nc = ltab.shape[2]
    nd = 1
    rc = min(cap, MOE_ROWS)
    assert cap % rc == 0
    kern = functools.partial(_moe_kernel, row0=row0, nc=nc, cap=cap, g_row=g_row)
    any_ = pl.BlockSpec(memory_space=pl.ANY)
    return pl.pallas_call(
        kern,
        grid=(n_batch, N_EXPERTS),
        in_specs=[any_, any_, any_, any_, any_,
                  pl.BlockSpec((MOD_ROWS, d), lambda b, e: (0, 5 * nd)),
                  pl.BlockSpec((None, d, EXPERT_FF), lambda b, e: (e, 0, 0)),
                  pl.BlockSpec((None, d, EXPERT_FF), lambda b, e: (e, 0, 0)),
                  pl.BlockSpec((None, EXPERT_FF, d), lambda b, e: (e, 0, 0))],
        out_specs=any_,
        out_shape=jax.ShapeDtypeStruct(xs.shape, F32),
        scratch_shapes=[pltpu.SMEM((nc, LANE), I32), pltpu.SMEM((nc, LANE), I32), pltpu.SMEM((nc, LANE), I32),
                        pltpu.VMEM((cap, d + LANE), F32), pltpu.VMEM((cap, d), F32),
                        pltpu.VMEM((rc, d), BF16), pltpu.SemaphoreType.DMA((3,))],
        input_output_aliases={4: 0},
        compiler_params=_params(("arbitrary", "arbitrary"), 56),
        name="moe_ffn",
    )(ltab, cnt, off, xma, xs, modl, w1, w3, w2)


def _rope_tables(n, lc):
    t = jnp.arange(n)
    pos = jnp.stack([t // GRID_W, t % GRID_W], axis=-1).astype(F32)

    def tables(dh):
        quarter = dh // 4
        inv_freq = ROPE_THETA ** (-jnp.arange(quarter, dtype=F32) / quarter)
        ang = pos[:, :, None] * inv_freq
        cos, sin = jnp.cos(ang), jnp.sin(ang)
        c = jnp.concatenate([cos[:, 0], cos[:, 0], cos[:, 1], cos[:, 1]], axis=-1)
        s = jnp.concatenate([-sin[:, 0], sin[:, 0], -sin[:, 1], sin[:, 1]], axis=-1)
        reps = LANE // dh
        c, s = jnp.tile(c, (1, reps)), jnp.tile(s, (1, reps))
        return (jnp.concatenate([jnp.ones((lc, LANE), F32), c], axis=0),
                jnp.concatenate([jnp.zeros((lc, LANE), F32), s], axis=0))

    return tables(HEAD_DIM) + tables(DIFF_QK_DIM)


def kernel(x, c, ctx, c_ctx, w_ada, b_ada, norm_g, w_in, w_out, qk_g_na, na_rpb, qk_g_diff, diff_lambda,
           diff_subln_g, qk_g_swa, swa_sink, lru_conv_w, lru_conv_b, lru_gate_w, lru_gate_b, lru_lam,
           w_router, w1, w3, w2):
    n_batch, n, d = x.shape
    lc = ctx.shape[1]
    depth = w_ada.shape[0]
    assert n_batch < MOD_ROWS and w_in.shape[2] == D_IN

    cvec = jnp.zeros((MOD_ROWS, d), F32).at[:n_batch].set(c).at[n_batch].set(c_ctx)
    mod = _ada(cvec, w_ada, b_ada)
    cs, ss, cd, sd = _rope_tables(n, lc)
    xs = jnp.concatenate([ctx, x], axis=1)

    cap_lat = CAPACITY_FACTOR * n // N_EXPERTS
    cap_ctx = CAPACITY_FACTOR * lc // N_EXPERTS

    for l in range(depth):
        modl = mod[l]
        u = _inproj(xs, norm_g[l, 0].reshape(1, d), modl, w_in[l].astype(BF16), lc)

        pvec = jnp.concatenate([qk_g_na[l], jnp.tile(qk_g_diff[l], (1, 2)), qk_g_swa[l],
                                diff_subln_g[l].reshape(1, LANE), jnp.zeros((1, LANE), F32)], axis=0)
        qa, ka, va, qd, kd, vd, qs, ks, vs = _prep(u, pvec, cs, ss, cd, sd)

        o_a = _na(qa, ka, va, _na_bias_tables(na_rpb[l], n), lc)

        lambda_init = 0.8 - 0.6 * math.exp(-0.3 * l)
        dl = diff_lambda[l].astype(F32)
        lam = jnp.exp(jnp.sum(dl[0] * dl[1])) - jnp.exp(jnp.sum(dl[2] * dl[3])) + lambda_init
        bound = 1.02 * DIFF_QK_DIM * DIFF_Q_SCALE * jnp.max(jnp.abs(qk_g_diff[l, 0])) * jnp.max(jnp.abs(qk_g_diff[l, 1]))
        o_d = _diff(jnp.stack([lam, bound]).astype(F32), qd, kd, vd, pvec, lc, 1.0 - lambda_init)

        o_s = _swa(qs, ks, vs, swa_sink[l], lc)

        af, bf, ab, bb = _lru_gates(u, lru_conv_w[l], lru_conv_b[l], lru_gate_w[l], lru_gate_b[l], lru_lam[l], lc)
        hf, hb = _lru_scan(af, bf, ab, bb, lc)

        xs = _outproj(o_a, o_d, o_s, hf, hb, u, modl, w_out[l].astype(BF16), xs, lc)

        wr_pad = jnp.zeros((d, LANE), F32).at[:, :N_EXPERTS].set(w_router[l])
        xma, affp = _router(xs, norm_g[l, 1].reshape(1, d), modl, wr_pad, lc)
        w1b, w3b, w2b = w1[l].astype(BF16), w3[l].astype(BF16), w2[l].astype(BF16)
        lt, cnt, off = _select(affp, lc, n, cap_lat)
        xs = _moe(lt, cnt, off, xma, xs, modl, w1b, w3b, w2b, lc, cap_lat, None)
        if l < depth - 1:
            lt, cnt, off = _select(affp, 0, lc, cap_ctx)
            xs = _moe(lt, cnt, off, xma, xs, modl, w1b, w3b, w2b, 0, cap_ctx, n_batch)
    return xs[:, lc:, :]
```

```python
import functools
import math

import numpy as np
import jax
import jax.numpy as jnp
from jax import lax
from jax.experimental import pallas as pl
from jax.experimental.pallas import tpu as pltpu

GRID_W = 64
HEAD_DIM = 128
NA_HEADS = 8
NA_KH = 8
NA_KW = 16
DIFF_HEADS = 8
DIFF_QK_DIM = 64
SWA_Q_HEADS = 8
SWA_KV_HEADS = 2
SWA_GROUP = SWA_Q_HEADS // SWA_KV_HEADS
SWA_WINDOW = 128
LRU_WIDTH = 1024
LRU_BLOCKS = 8
LRU_C = 8.0
N_EXPERTS = 16
EXPERT_FF = 256
CAPACITY_FACTOR = 2
ROPE_THETA = 10000.0
NEG_INF = -1e30
EPS = 1e-6

OFF_QA, OFF_KA, OFF_VA = 0, 1024, 2048
OFF_QD, OFF_KD, OFF_VD = 3072, 4096, 5120
OFF_QS, OFF_KS, OFF_VS = 6144, 7168, 7424
OFF_XR, OFF_GR = 7680, 8704
D_ATT = 7680
D_IN = 9728
D_MIX = 4096

LOG2E = 1.4426950408889634
DIFF_Q_SCALE = DIFF_QK_DIM ** -0.5 * LOG2E
ATTN_FAST_BOUND = 60.0
ATTN_Q_SCALE = HEAD_DIM ** -0.5 * LOG2E
BOUND_SLACK = 1.02

LANE = 128
MOD_ROWS = 8
V7X_VMEM_BYTES = 64 * 1024 * 1024

F32 = jnp.float32
BF16 = jnp.bfloat16
I32 = jnp.int32


def _tile(n, target, mult=LANE):
    best = None
    for t in range(mult, min(n, target) + 1, mult):
        if n % t == 0:
            best = t
    assert best is not None, (n, target, mult)
    return best


def _params(sem, vmem_mb):
    return pltpu.CompilerParams(dimension_semantics=sem, vmem_limit_bytes=min(vmem_mb, 60) * 1024 * 1024)


def _dot_nt(a, b):
    return lax.dot_general(a, b, (((1,), (1,)), ((), ())), preferred_element_type=F32)


def _dot(a, b):
    return jnp.dot(a, b, preferred_element_type=F32)


def _ada_kernel(c_ref, w_ref, b_ref, o_ref):
    cv = c_ref[...]
    s = cv * jax.nn.sigmoid(cv)
    o_ref[...] = jnp.dot(s, w_ref[...], preferred_element_type=F32,
                         precision=lax.Precision.HIGHEST) + b_ref[...]


def _ada(cvec, w_ada, b_ada):
    depth, d, n6 = w_ada.shape
    tn = _tile(n6, 512)
    return pl.pallas_call(
        _ada_kernel,
        grid=(depth, n6 // tn),
        in_specs=[pl.BlockSpec((MOD_ROWS, d), lambda l, j: (0, 0)),
                  pl.BlockSpec((None, d, tn), lambda l, j: (l, 0, j)),
                  pl.BlockSpec((None, 1, tn), lambda l, j: (l, 0, j))],
        out_specs=pl.BlockSpec((None, MOD_ROWS, tn), lambda l, j: (l, 0, j)),
        out_shape=jax.ShapeDtypeStruct((depth, MOD_ROWS, n6), F32),
        compiler_params=_params(("arbitrary", "arbitrary"), 40),
        name="ada_mod",
    )(cvec, w_ada, b_ada.reshape(depth, 1, n6))


ROW_CHUNK = 32


def _mod_row(ref, row0, b, n_batch, lc):
    return jnp.where(row0 < lc, ref[pl.ds(n_batch, 1), :], ref[pl.ds(b, 1), :])


def _norm_modulate(x, g, sh, sc):
    ms = jnp.mean(x * x, axis=-1, keepdims=True)
    return x * lax.rsqrt(ms + EPS) * g * (1.0 + sc) + sh


def _for_row_chunks(tm, fn):
    assert tm % ROW_CHUNK == 0

    def body(c, _):
        fn(pl.multiple_of(c * ROW_CHUNK, ROW_CHUNK))
        return 0

    lax.fori_loop(0, tm // ROW_CHUNK, body, 0)


def _inproj_kernel(x_ref, g_ref, sh_ref, sc_ref, w_ref, o_ref, xn_ref, *, n_batch, lc, tm):
    b = pl.program_id(0)
    i = pl.program_id(1)

    @pl.when(pl.program_id(2) == 0)
    def _():
        def chunk(r0):
            sh = _mod_row(sh_ref, i * tm + r0, b, n_batch, lc)
            sc = _mod_row(sc_ref, i * tm + r0, b, n_batch, lc)
            xn = _norm_modulate(x_ref[pl.ds(r0, ROW_CHUNK), :], g_ref[...], sh, sc)
            xn_ref[pl.ds(r0, ROW_CHUNK), :] = xn.astype(BF16)

        _for_row_chunks(tm, chunk)

    o_ref[...] = _dot(xn_ref[...], w_ref[...])


def _inproj(xs, g, modl, w_bf, lc):
    n_batch, nt, d = xs.shape
    d_in = w_bf.shape[1]
    tm = _tile(nt, 768)
    tn = _tile(d_in, 512)
    assert lc % ROW_CHUNK == 0
    kern = functools.partial(_inproj_kernel, n_batch=n_batch, lc=lc, tm=tm)
    return pl.pallas_call(
        kern,
        grid=(n_batch, nt // tm, d_in // tn),
        in_specs=[pl.BlockSpec((None, tm, d), lambda b, i, j: (b, i, 0)),
                  pl.BlockSpec((1, d), lambda b, i, j: (0, 0)),
                  pl.BlockSpec((MOD_ROWS, d), lambda b, i, j: (0, 0)),
                  pl.BlockSpec((MOD_ROWS, d), lambda b, i, j: (0, 1)),
                  pl.BlockSpec((d, tn), lambda b, i, j: (0, j))],
        out_specs=pl.BlockSpec((None, tm, tn), lambda b, i, j: (b, i, j)),
        out_shape=jax.ShapeDtypeStruct((n_batch, nt, d_in), F32),
        scratch_shapes=[pltpu.VMEM((tm, d), BF16)],
        compiler_params=_params(("arbitrary", "arbitrary", "arbitrary"), 56),
        name="in_proj",
    )(xs, g, modl, modl, w_bf)


def _rms_full(x, g):
    ms = jnp.mean(x * x, axis=-1, keepdims=True)
    return x * lax.rsqrt(ms + EPS) * g


def _rms_halves(x, g):
    lo = lax.broadcasted_iota(I32, (1, LANE), 1) < DIFF_QK_DIM
    x2 = x * x
    s_lo = jnp.sum(jnp.where(lo, x2, 0.0), axis=-1, keepdims=True)
    s_hi = jnp.sum(jnp.where(lo, 0.0, x2), axis=-1, keepdims=True)
    ms = jnp.where(lo, s_lo, s_hi) * (1.0 / DIFF_QK_DIM)
    return x * lax.rsqrt(ms + EPS) * g


def _rope(x, cos, sin_signed, half):
    lane = lax.broadcasted_iota(I32, (1, LANE), 1)
    up = pltpu.roll(x, LANE - half, axis=1)
    dn = pltpu.roll(x, half, axis=1)
    partner = jnp.where((lane & half) == 0, up, dn)
    return x * cos + partner * sin_signed


def _prep_kernel(u_ref, p_ref, cs_ref, ss_ref, cd_ref, sd_ref,
                 qa_ref, ka_ref, va_ref, qd_ref, kd_ref, vd_ref, qs_ref, ks_ref, vs_ref):
    g_na_q, g_na_k = p_ref[0:1, :], p_ref[1:2, :]
    g_d_q, g_d_k = p_ref[2:3, :], p_ref[3:4, :]
    g_s_q, g_s_k = p_ref[4:5, :], p_ref[5:6, :]
    cs, ss, cd, sd = cs_ref[...], ss_ref[...], cd_ref[...], sd_ref[...]
    lo = lax.broadcasted_iota(I32, (1, LANE), 1) < DIFF_QK_DIM

    def col(off, h):
        return u_ref[:, off + h * LANE: off + (h + 1) * LANE]

    def put(ref, h, val):
        ref[:, h * LANE:(h + 1) * LANE] = val.astype(BF16)

    for h in range(NA_HEADS):
        put(qa_ref, h, _rms_full(col(OFF_QA, h), g_na_q) * ATTN_Q_SCALE)
        put(ka_ref, h, _rms_full(col(OFF_KA, h), g_na_k))
        put(va_ref, h, col(OFF_VA, h))
    ones_col = jnp.where(lax.broadcasted_iota(I32, (u_ref.shape[0], LANE), 1) == 0, 1.0, 0.0).astype(BF16)
    for h in range(DIFF_HEADS):
        q = _rope(_rms_halves(col(OFF_QD, h), g_d_q), cd, sd, DIFF_QK_DIM // 4) * DIFF_Q_SCALE
        qd_ref[0, :, h * LANE:(h + 1) * LANE] = jnp.where(lo, q, 0.0).astype(BF16)
        qd_ref[1, :, h * LANE:(h + 1) * LANE] = jnp.where(lo, 0.0, q).astype(BF16)
        put(kd_ref, h, _rope(_rms_halves(col(OFF_KD, h), g_d_k), cd, sd, DIFF_QK_DIM // 4))
        put(vd_ref, 2 * h, col(OFF_VD, h))
        vd_ref[:, (2 * h + 1) * LANE:(2 * h + 2) * LANE] = ones_col
    for h in range(SWA_Q_HEADS):
        put(qs_ref, h, _rope(_rms_full(col(OFF_QS, h), g_s_q), cs, ss, HEAD_DIM // 4) * ATTN_Q_SCALE)
    for h in range(SWA_KV_HEADS):
        put(ks_ref, h, _rope(_rms_full(col(OFF_KS, h), g_s_k), cs, ss, HEAD_DIM // 4))
        put(vs_ref, h, col(OFF_VS, h))


def _prep(u, pvec, cs, ss, cd, sd):
    n_batch, nt, _ = u.shape
    tr = _tile(nt, 256)
    row = lambda w: pl.BlockSpec((None, tr, w), lambda b, i: (b, i, 0))
    tab = pl.BlockSpec((tr, LANE), lambda b, i: (i, 0))
    sd_ = lambda w: jax.ShapeDtypeStruct((n_batch, nt, w), BF16)
    return pl.pallas_call(
        _prep_kernel,
        grid=(n_batch, nt // tr),
        in_specs=[row(D_ATT), pl.BlockSpec((8, LANE), lambda b, i: (0, 0)), tab, tab, tab, tab],
        out_specs=[row(1024), row(1024), row(1024),
                   pl.BlockSpec((None, 2, tr, 1024), lambda b, i: (b, 0, i, 0)),
                   row(1024), row(2048), row(1024), row(256), row(256)],
        out_shape=[sd_(1024), sd_(1024), sd_(1024),
                   jax.ShapeDtypeStruct((n_batch, 2, nt, 1024), BF16),
                   sd_(1024), sd_(2048), sd_(1024), sd_(256), sd_(256)],
        compiler_params=_params(("arbitrary", "arbitrary"), 48),
        name="qkv_prep",
    )(u, pvec, cs, ss, cd, sd)


NA_QBLK = 2 * GRID_W
NA_KROWS = 10
NA_KBLK = NA_KROWS * GRID_W


def _na_kernel(sc_ref, q_ref, k_ref, v_ref, *rest, lc, nlb, group, online):
    bias_refs, o_ref = rest[:group], rest[group]
    i0 = pl.program_id(2) * group
    kc = k_ref[0:lc, :]
    vc = v_ref[0:lc, :]
    for g in range(group):
        li = i0 + g - lc // NA_QBLK
        kb = jnp.clip(li - 2, 0, nlb - NA_KROWS // 2)
        start = pl.multiple_of(lc + kb * NA_QBLK, NA_QBLK)
        rows = slice(g * NA_QBLK, (g + 1) * NA_QBLK)
        q = q_ref[rows, :]
        vw = v_ref[pl.ds(start, NA_KBLK), :]
        s_n = _dot_nt(q, k_ref[pl.ds(start, NA_KBLK), :]) + bias_refs[g][...]
        s_c = _dot_nt(q, kc)
        if online:
            m = jnp.maximum(jnp.max(s_n, axis=-1, keepdims=True), jnp.max(s_c, axis=-1, keepdims=True))
        else:
            m = sc_ref[0]
        p_n = jnp.exp2(s_n - m)
        p_c = jnp.exp2(s_c - m)
        l = jnp.sum(p_n, axis=-1, keepdims=True) + jnp.sum(p_c, axis=-1, keepdims=True)
        o = _dot(p_n.astype(BF16), vw) + _dot(p_c.astype(BF16), vc)
        o_ref[rows, :] = (o / l).astype(BF16)


def _na_bias_tables(rpb, n):
    rows = n // GRID_W
    nlb = rows // 2
    reps = [0, 1, 2, nlb - 2, nlb - 1]
    h = rpb.shape[0]
    cols = np.arange(GRID_W)
    cstart = np.clip(cols - NA_KW // 2, 0, GRID_W - NA_KW)
    col_ok = (cols[None, :] >= cstart[:, None]) & (cols[None, :] < cstart[:, None] + NA_KW)
    dx = cols[None, :] - cols[:, None] + (NA_KW - 1)
    sel = (dx[None] == np.arange(2 * NA_KW - 1)[:, None, None]) & col_ok[None]
    tx = jnp.einsum('hyx,xqk->hyqk', rpb.astype(F32), jnp.asarray(sel, F32), precision=lax.Precision.HIGHEST)
    dy_i = np.zeros((5, 2, NA_KROWS), np.int32)
    row_ok = np.zeros((5, 2, NA_KROWS), bool)
    for c, li in enumerate(reps):
        kb = int(np.clip(li - 2, 0, nlb - NA_KROWS // 2))
        qr = 2 * li + np.arange(2)
        kr = 2 * kb + np.arange(NA_KROWS)
        rs = np.clip(qr - NA_KH // 2, 0, rows - NA_KH)
        row_ok[c] = (kr[None, :] >= rs[:, None]) & (kr[None, :] < rs[:, None] + NA_KH)
        dy_i[c] = np.clip(kr[None, :] - qr[:, None] + (NA_KH - 1), 0, 2 * NA_KH - 2)
    blocks = jnp.take(tx, jnp.asarray(dy_i.reshape(-1)), axis=1)
    blocks = blocks.reshape(h, 5, 2, NA_KROWS, GRID_W, GRID_W)
    ok = row_ok[:, :, :, None, None] & col_ok[None, None, None]
    tbl = jnp.where(ok[None], blocks * LOG2E, NEG_INF)
    tbl = tbl.transpose(1, 0, 2, 4, 3, 5).reshape(5, h, NA_QBLK, NA_KBLK)
    dead = jnp.full((1,) + tbl.shape[1:], NEG_INF, F32)
    return jnp.concatenate([tbl, dead], axis=0)


def _group(nblocks, target):
    return max(g for g in range(1, target + 1) if nblocks % g == 0)


def _softmax_dispatch(bound, call):
    sc = jnp.reshape(bound, (1,)).astype(F32)
    return lax.cond(bound <= ATTN_FAST_BOUND, lambda s: call(False)(s), lambda s: call(True)(s), sc)


def _na(bound, qa, ka, va, bias, lc):
    n_batch, nt, _ = qa.shape
    n = nt - lc
    nlb = n // NA_QBLK
    assert nlb >= 5 and lc % NA_QBLK == 0
    ncb = lc // NA_QBLK
    group = _group(nt // NA_QBLK, 6)

    def case(i):
        li = i - ncb
        return jnp.where(li < 0, 5, jnp.where(li < 2, li, jnp.where(li >= nlb - 2, li - (nlb - 5), 2)))

    kv = pl.BlockSpec((None, nt, LANE), lambda b, h, i: (b, 0, h))
    qo = pl.BlockSpec((None, group * NA_QBLK, LANE), lambda b, h, i: (b, i, h))
    bias_specs = [pl.BlockSpec((None, None, NA_QBLK, NA_KBLK),
                               functools.partial(lambda b, h, i, g: (case(i * group + g), h, 0, 0), g=g))
                  for g in range(group)]

    def call(online):
        kern = functools.partial(_na_kernel, lc=lc, nlb=nlb, group=group, online=online)
        return lambda sc: pl.pallas_call(
            kern,
            grid=(n_batch, NA_HEADS, nt // (group * NA_QBLK)),
            in_specs=[pl.BlockSpec(memory_space=pltpu.SMEM), qo, kv, kv] + bias_specs,
            out_specs=qo,
            out_shape=jax.ShapeDtypeStruct((n_batch, nt, NA_HEADS * HEAD_DIM), BF16),
            compiler_params=_params(("arbitrary",) * 3, 40),
            name="na_attn_online" if online else "na_attn_offset",
        )(sc, qa, ka, va, *([bias] * group))

    return _softmax_dispatch(bound, call)


SWA_BLK = 128
SWA_WIN = 3 * SWA_BLK


def _swa_kernel(sc_ref, q_ref, k_ref, v_ref, sink_ref, *rest, lc, n, group, online):
    mask_refs, o_ref = rest[:group], rest[group]
    i0 = pl.program_id(2) * group
    kc = k_ref[0:lc, :]
    vc = v_ref[0:lc, :]
    for gq in range(group):
        li = i0 + gq - lc // SWA_BLK
        ws = jnp.clip((li - 1) * SWA_BLK, 0, n - SWA_WIN)
        start = pl.multiple_of(lc + ws, SWA_BLK)
        kw = k_ref[pl.ds(start, SWA_WIN), :]
        vw = v_ref[pl.ds(start, SWA_WIN), :]
        rows = slice(gq * SWA_BLK, (gq + 1) * SWA_BLK)
        mask = mask_refs[gq][...]
        for g in range(SWA_GROUP):
            q = q_ref[rows, g * LANE:(g + 1) * LANE]
            s_w = _dot_nt(q, kw)
            s_c = _dot_nt(q, kc)
            sink = sink_ref[g * SWA_BLK:(g + 1) * SWA_BLK, 0:1]
            if online:
                s_w = jnp.where(mask > 0.0, s_w, NEG_INF)
                m = jnp.maximum(jnp.maximum(jnp.max(s_w, axis=-1, keepdims=True),
                                            jnp.max(s_c, axis=-1, keepdims=True)), sink)
                p_w = jnp.exp2(s_w - m)
            else:
                m = sc_ref[0]
                p_w = jnp.exp2(s_w - m) * mask
            p_c = jnp.exp2(s_c - m)
            l = jnp.sum(p_w, axis=-1, keepdims=True) + jnp.sum(p_c, axis=-1, keepdims=True) + jnp.exp2(sink - m)
            o = (_dot(p_w.astype(BF16), vw) + _dot(p_c.astype(BF16), vc)) / l
            o_ref[rows, g * LANE:(g + 1) * LANE] = o.astype(BF16)


def _swa_masks():
    r = np.arange(SWA_BLK)[:, None]
    j = np.arange(SWA_WIN)[None, :]
    rel = [j - r, j - r - SWA_BLK, j - r - 2 * SWA_BLK]
    tabs = [(np.abs(d) <= SWA_WINDOW) for d in rel] + [np.zeros((SWA_BLK, SWA_WIN), bool)]
    return jnp.asarray(np.stack(tabs), F32)


def _swa(bound, qs, ks, vs, sink, lc):
    n_batch, nt, _ = qs.shape
    n = nt - lc
    nb = n // SWA_BLK
    assert nb >= 3 and lc % SWA_BLK == 0
    ncb = lc // SWA_BLK
    rows = SWA_GROUP * SWA_BLK
    group = _group(nt // SWA_BLK, 3)
    sink_rows = jnp.broadcast_to(
        jnp.repeat((sink.astype(F32) * LOG2E).reshape(SWA_KV_HEADS, SWA_GROUP), SWA_BLK, axis=1)[:, :, None],
        (SWA_KV_HEADS, rows, LANE))
    masks = _swa_masks()

    def case(i):
        li = i - ncb
        return jnp.where(li < 0, 3, jnp.where(li == 0, 0, jnp.where(li == nb - 1, 2, 1)))

    kv = pl.BlockSpec((None, nt, LANE), lambda b, k, i: (b, 0, k))
    qo = pl.BlockSpec((None, group * SWA_BLK, SWA_GROUP * LANE), lambda b, k, i: (b, i, k))
    mask_specs = [pl.BlockSpec((None, SWA_BLK, SWA_WIN),
                               functools.partial(lambda b, k, i, g: (case(i * group + g), 0, 0), g=g))
                  for g in range(group)]

    def call(online):
        kern = functools.partial(_swa_kernel, lc=lc, n=n, group=group, online=online)
        return lambda sc: pl.pallas_call(
            kern,
            grid=(n_batch, SWA_KV_HEADS, nt // (group * SWA_BLK)),
            in_specs=[pl.BlockSpec(memory_space=pltpu.SMEM), qo, kv, kv,
                      pl.BlockSpec((None, rows, LANE), lambda b, k, i: (k, 0, 0))] + mask_specs,
            out_specs=qo,
            out_shape=jax.ShapeDtypeStruct((n_batch, nt, SWA_Q_HEADS * HEAD_DIM), BF16),
            compiler_params=_params(("arbitrary",) * 3, 32),
            name="swa_attn_online" if online else "swa_attn_offset",
        )(sc, qs, ks, vs, sink_rows, *([masks] * group))

    return _softmax_dispatch(bound, call)


def _diff_finish(sc_ref, g_ref, o_ref, num, den, tq, post_scale):
    o2 = num / den
    o = o2[0:tq, :] - sc_ref[0] * o2[tq:2 * tq, :]
    o_ref[...] = (_rms_full(o, g_ref[6:7, :]) * post_scale).astype(BF16)


def _diff_online_kernel(sc_ref, q_ref, k_ref, v_ref, g_ref, o_ref, *, lc, nt, tq, tk, post_scale):
    qi = pl.program_id(2)
    q = q_ref[...].reshape(2 * tq, LANE)
    nk = jnp.where(qi < lc // tq, lc // tk, nt // tk)

    def body(kt, carry):
        m, l, acc = carry
        k0 = pl.multiple_of(kt * tk, tk)
        k = k_ref[pl.ds(k0, tk), :]
        v = v_ref[pl.ds(k0, tk), :]
        s = _dot_nt(q, k)
        m_new = jnp.maximum(m, jnp.max(s, axis=-1, keepdims=True))
        alpha = jnp.exp2(m - m_new)
        p = jnp.exp2(s - m_new)
        l = alpha * l + jnp.sum(p, axis=-1, keepdims=True)
        acc = alpha * acc + _dot(p.astype(BF16), v)
        return m_new, l, acc

    init = (jnp.full((2 * tq, 1), NEG_INF, F32), jnp.zeros((2 * tq, 1), F32), jnp.zeros((2 * tq, LANE), F32))
    _, l, acc = lax.fori_loop(0, nk, body, init)
    _diff_finish(sc_ref, g_ref, o_ref, acc, l, tq, post_scale)


def _diff_offset_kernel(sc_ref, q_ref, k_ref, v_ref, g_ref, o_ref, acc_ref, p_ref, *, lc, nt, tq, tk, post_scale):
    qi = pl.program_id(2)
    rows2 = 2 * tq
    q = q_ref[...].reshape(rows2, LANE)
    off = sc_ref[1]

    def probs(k0, tkk):
        return jnp.exp2(_dot_nt(q, k_ref[k0:k0 + tkk, :]) - off).astype(BF16)

    @pl.when(qi < lc // tq)
    def _():
        acc_ref[...] = _dot(probs(0, lc), v_ref[0:lc, :])

    @pl.when(qi >= lc // tq)
    def _():
        nk = nt // tk
        p_ref[0] = probs(0, tk)
        tot = None
        for t in range(nk):
            pv = _dot(p_ref[t % 2], v_ref[t * tk:(t + 1) * tk, :])
            if t + 1 < nk:
                p_ref[(t + 1) % 2] = probs((t + 1) * tk, tk)
            tot = pv if tot is None else tot + pv
        acc_ref[...] = tot

    acc = acc_ref[...]
    _diff_finish(sc_ref, g_ref, o_ref, acc[:, 0:LANE], acc[:, LANE:LANE + 1], tq, post_scale)


def _diff(sc, qd, kd, vde, pvec, lc, post_scale):
    n_batch, _, nt, _ = qd.shape
    tq = _tile(math.gcd(lc, nt), 256)
    assert lc % tq == 0
    grid = (n_batch, DIFF_HEADS, nt // tq)
    smem = pl.BlockSpec(memory_space=pltpu.SMEM)
    qspec = pl.BlockSpec((None, 2, tq, LANE), lambda b, h, i: (b, 0, i, h))
    kspec = pl.BlockSpec((None, nt, LANE), lambda b, h, i: (b, 0, h))
    gspec = pl.BlockSpec((8, LANE), lambda b, h, i: (0, 0))
    ospec = pl.BlockSpec((None, tq, LANE), lambda b, h, i: (b, i, h))
    oshape = jax.ShapeDtypeStruct((n_batch, nt, DIFF_HEADS * HEAD_DIM), BF16)

    def offset(sc, qd, kd, vde, pvec):
        tk = _tile(nt, 768)
        kern = functools.partial(_diff_offset_kernel, lc=lc, nt=nt, tq=tq, tk=tk, post_scale=post_scale)
        return pl.pallas_call(
            kern, grid=grid,
            in_specs=[smem, qspec, kspec, pl.BlockSpec((None, nt, 2 * LANE), lambda b, h, i: (b, 0, h)), gspec],
            out_specs=ospec, out_shape=oshape,
            scratch_shapes=[pltpu.VMEM((2 * tq, 2 * LANE), F32), pltpu.VMEM((2, 2 * tq, tk), BF16)],
            compiler_params=_params(("arbitrary",) * 3, 40),
            name="diff_attn_offset",
        )(sc, qd, kd, vde, pvec)

    def online(sc, qd, kd, vde, pvec):
        tk = _tile(math.gcd(lc, nt), 256)
        kern = functools.partial(_diff_online_kernel, lc=lc, nt=nt, tq=tq, tk=tk, post_scale=post_scale)
        return pl.pallas_call(
            kern, grid=grid,
            in_specs=[smem, qspec, kspec, pl.BlockSpec((None, nt, LANE), lambda b, h, i: (b, 0, 2 * h)), gspec],
            out_specs=ospec, out_shape=oshape,
            compiler_params=_params(("arbitrary",) * 3, 32),
            name="diff_attn_online",
        )(sc, qd, kd, vde, pvec)

    return lax.cond(sc[1] <= ATTN_FAST_BOUND, offset, online, sc, qd, kd, vde, pvec)


LRU_HALO = 8


def _lru_gate_kernel(xp_ref, x_ref, xn_ref, cw_ref, cb_ref, gw_ref, gb_ref, lam_ref,
                     af_ref, bf_ref, ab_ref, bb_ref, *, t, seg_starts, seg_ends):
    k = pl.program_id(2)
    is_start = functools.reduce(jnp.logical_or, [k == s for s in seg_starts])
    is_end = functools.reduce(jnp.logical_or, [k == s for s in seg_ends])
    cur = x_ref[...]
    prev = jnp.where(is_start, 0.0, xp_ref[...])
    nxt = jnp.where(is_end, 0.0, xn_ref[...])
    ext = jnp.concatenate([prev, cur, nxt], axis=0)
    h = LRU_HALO
    xl = (cb_ref[...] + cw_ref[0:1, :] * ext[h - 2:h - 2 + t, :] + cw_ref[1:2, :] * ext[h - 1:h - 1 + t, :]
          + cw_ref[2:3, :] * cur + cw_ref[3:4, :] * ext[h + 1:h + 1 + t, :])
    xb = xl.astype(BF16)
    outs = ((af_ref, bf_ref), (ab_ref, bb_ref))
    sigmoid = lambda z: 0.5 + 0.5 * jnp.tanh(0.5 * z)
    for d in range(2):
        r = sigmoid(_dot(xb, gw_ref[d, 0].astype(BF16)) + gb_ref[2 * d:2 * d + 1, :])
        ig = sigmoid(_dot(xb, gw_ref[d, 1].astype(BF16)) + gb_ref[2 * d + 1:2 * d + 2, :])
        a = jnp.exp(LRU_C * r * jax.nn.log_sigmoid(lam_ref[d:d + 1, :]))
        outs[d][0][...] = a
        outs[d][1][...] = jnp.sqrt(1.0 - a * a) * (ig * xl)


def _lru_gates(u, conv_w, conv_b, gate_w, gate_b, lam, lc):
    n_batch, nt, _ = u.shape
    t = _tile(math.gcd(lc, nt), 256)
    nk = nt // t
    hb = t // LRU_HALO
    seg_starts = (0, lc // t)
    seg_ends = (lc // t - 1, nk - 1)
    cb0 = OFF_XR // LANE
    kern = functools.partial(_lru_gate_kernel, t=t, seg_starts=seg_starts, seg_ends=seg_ends)
    out = pl.BlockSpec((None, t, LANE), lambda b, c, k: (b, k, c))
    osd = jax.ShapeDtypeStruct((n_batch, nt, LRU_WIDTH), F32)
    return pl.pallas_call(
        kern,
        grid=(n_batch, LRU_BLOCKS, nk),
        in_specs=[pl.BlockSpec((None, LRU_HALO, LANE), lambda b, c, k: (b, jnp.maximum(k * hb - 1, 0), cb0 + c)),
                  pl.BlockSpec((None, t, LANE), lambda b, c, k: (b, k, cb0 + c)),
                  pl.BlockSpec((None, LRU_HALO, LANE),
                               lambda b, c, k: (b, jnp.minimum((k + 1) * hb, nt // LRU_HALO - 1), cb0 + c)),
                  pl.BlockSpec((4, LANE), lambda b, c, k: (0, c)),
                  pl.BlockSpec((1, LANE), lambda b, c, k: (0, c)),
                  pl.BlockSpec((2, 2, None, LANE, LANE), lambda b, c, k: (0, 0, c, 0, 0)),
                  pl.BlockSpec((4, LANE), lambda b, c, k: (0, c)),
                  pl.BlockSpec((2, LANE), lambda b, c, k: (0, c))],
        out_specs=[out, out, out, out],
        out_shape=[osd, osd, osd, osd],
        compiler_params=_params(("arbitrary",) * 3, 24),
        name="lru_gates",
    )(u, u, u, conv_w, conv_b.reshape(1, LRU_WIDTH), gate_w, gate_b.reshape(4, LRU_WIDTH), lam)


def _lru_scan_kernel(af_ref, bf_ref, ab_ref, bb_ref, hf_ref, hb_ref, cf_ref, cbk_ref, *, t):
    @pl.when(pl.program_id(1) == 0)
    def _():
        cf_ref[...] = jnp.zeros_like(cf_ref)
        cbk_ref[...] = jnp.zeros_like(cbk_ref)

    ng = t // 8

    def body(g, carry):
        hf, hb = carry
        r0 = pl.multiple_of(g * 8, 8)
        a = af_ref[pl.ds(r0, 8), :]
        bt = bf_ref[pl.ds(r0, 8), :]
        rows = []
        for r in range(8):
            hf = a[r:r + 1, :] * hf + bt[r:r + 1, :]
            rows.append(hf)
        hf_ref[pl.ds(r0, 8), :] = jnp.concatenate(rows, axis=0)
        r1 = pl.multiple_of((ng - 1 - g) * 8, 8)
        a = ab_ref[pl.ds(r1, 8), :]
        bt = bb_ref[pl.ds(r1, 8), :]
        rows = []
        for r in range(7, -1, -1):
            hb = a[r:r + 1, :] * hb + bt[r:r + 1, :]
            rows.append(hb)
        hb_ref[pl.ds(r1, 8), :] = jnp.concatenate(rows[::-1], axis=0)
        return hf, hb

    hf, hb = lax.fori_loop(0, ng, body, (cf_ref[...], cbk_ref[...]))
    cf_ref[...] = hf
    cbk_ref[...] = hb


def _lru_scan(af, bf, ab, bb, lc):
    n_batch, nt, w = af.shape
    t = lc
    assert nt % t == 0
    nk = nt // t
    kern = functools.partial(_lru_scan_kernel, t=t)
    fwd = pl.BlockSpec((None, t, w), lambda b, k: (b, k, 0))
    bwd = pl.BlockSpec((None, t, w), lambda b, k: (b, jnp.where(k == 0, 0, nk - k), 0))
    osd = jax.ShapeDtypeStruct((n_batch, nt, w), F32)
    return pl.pallas_call(
        kern,
        grid=(n_batch, nk),
        in_specs=[fwd, fwd, bwd, bwd],
        out_specs=[fwd, bwd],
        out_shape=[osd, osd],
        scratch_shapes=[pltpu.VMEM((1, w), F32), pltpu.VMEM((1, w), F32)],
        compiler_params=_params(("arbitrary", "arbitrary"), 24),
        name="lru_scan",
    )(af, bf, ab, bb)


def _outproj_kernel(oa_ref, od_ref, os_ref, hf_ref, hb_ref, g0_ref, g1_ref, gate_ref, w_ref, x_ref,
                    o_ref, lhs_ref, *, n_batch, lc, tm):
    b = pl.program_id(0)
    i = pl.program_id(1)

    @pl.when(pl.program_id(2) == 0)
    def _():
        def chunk(r0):
            rs = pl.ds(r0, ROW_CHUNK)
            lhs_ref[rs, 0:1024] = oa_ref[rs, :]
            lhs_ref[rs, 1024:2048] = od_ref[rs, :]
            lhs_ref[rs, 2048:3072] = os_ref[rs, :]
            for c, g_ref in enumerate((g0_ref, g1_ref)):
                cs = slice(c * 512, (c + 1) * 512)
                y = (hf_ref[rs, cs] + hb_ref[rs, cs]) * jax.nn.gelu(g_ref[rs, :])
                lhs_ref[rs, 3072 + c * 512:3584 + c * 512] = y.astype(BF16)

        _for_row_chunks(tm, chunk)

    row = i * tm + lax.broadcasted_iota(I32, (tm, 1), 0)
    gate = jnp.where(row < lc, gate_ref[pl.ds(n_batch, 1), :], gate_ref[pl.ds(b, 1), :])
    o_ref[...] = x_ref[...] + gate * _dot(lhs_ref[...], w_ref[...])


def _outproj(o_a, o_d, o_s, hf, hb, u, modl, w_bf, xs, lc):
    n_batch, nt, d = xs.shape
    tm = _tile(nt, 768)
    tn = _tile(d, 512)
    ntile = d // tn
    kern = functools.partial(_outproj_kernel, n_batch=n_batch, lc=lc, tm=tm)
    mix = pl.BlockSpec((None, tm, 1024), lambda b, i, j: (b, i, 0))
    gr = lambda c: pl.BlockSpec((None, tm, 512), lambda b, i, j: (b, i, OFF_GR // 512 + c))
    xblk = pl.BlockSpec((None, tm, tn), lambda b, i, j: (b, i, j))
    return pl.pallas_call(
        kern,
        grid=(n_batch, nt // tm, ntile),
        in_specs=[mix, mix, mix, mix, mix, gr(0), gr(1),
                  pl.BlockSpec((MOD_ROWS, tn), lambda b, i, j: (0, 2 * ntile + j)),
                  pl.BlockSpec((D_MIX, tn), lambda b, i, j: (0, j)), xblk],
        out_specs=xblk,
        out_shape=jax.ShapeDtypeStruct(xs.shape, F32),
        scratch_shapes=[pltpu.VMEM((tm, D_MIX), BF16)],
        input_output_aliases={9: 0},
        compiler_params=_params(("arbitrary",) * 3, 56),
        name="out_proj",
    )(o_a, o_d, o_s, hf, hb, u, u, modl, w_bf, xs)


def _split3(x):
    hi = x.astype(BF16)
    r = x - hi.astype(F32)
    mid = r.astype(BF16)
    return hi, mid, (r - mid.astype(F32)).astype(BF16)


ROUTER_KC = 512


def _router_kernel(x_ref, g_ref, sh_ref, sc_ref, w0_ref, w1_ref, w2_ref, xm_ref, aff_ref, *, n_batch, lc, tm):
    b = pl.program_id(0)
    i = pl.program_id(1)
    d = x_ref.shape[1]

    def chunk(r0):
        sh = _mod_row(sh_ref, i * tm + r0, b, n_batch, lc)
        sc = _mod_row(sc_ref, i * tm + r0, b, n_batch, lc)
        xm_ref[pl.ds(r0, ROW_CHUNK), 0:d] = _norm_modulate(x_ref[pl.ds(r0, ROW_CHUNK), :], g_ref[...], sh, sc)

    _for_row_chunks(tm, chunk)

    logits = jnp.zeros((tm, LANE), F32)
    tk = _tile(d, ROUTER_KC)
    for kc in range(d // tk):
        ks = slice(kc * tk, (kc + 1) * tk)
        x0, x1, x2 = _split3(xm_ref[:, ks])
        w0, w1, w2 = w0_ref[ks, :], w1_ref[ks, :], w2_ref[ks, :]
        logits = logits + (_dot(x0, w0) + (_dot(x0, w1) + _dot(x1, w0))
                           + (_dot(x1, w1) + _dot(x0, w2) + _dot(x2, w0)))
    lane = lax.broadcasted_iota(I32, (1, LANE), 1)
    logits = jnp.where(lane < N_EXPERTS, logits, NEG_INF)
    m = jnp.max(logits, axis=-1, keepdims=True)
    p = jnp.exp(logits - m)
    aff = p / jnp.sum(p, axis=-1, keepdims=True)
    aff_ref[...] = aff
    xm_ref[:, d:d + LANE] = aff


def _router(xs, g, modl, wr_pad, lc):
    n_batch, nt, d = xs.shape
    tm = _tile(nt, 256)
    assert lc % ROW_CHUNK == 0
    kern = functools.partial(_router_kernel, n_batch=n_batch, lc=lc, tm=tm)
    wspec = pl.BlockSpec((d, LANE), lambda b, i: (0, 0))
    return pl.pallas_call(
        kern,
        grid=(n_batch, nt // tm),
        in_specs=[pl.BlockSpec((None, tm, d), lambda b, i: (b, i, 0)),
                  pl.BlockSpec((1, d), lambda b, i: (0, 0)),
                  pl.BlockSpec((MOD_ROWS, d), lambda b, i: (0, 3)),
                  pl.BlockSpec((MOD_ROWS, d), lambda b, i: (0, 4)),
                  wspec, wspec, wspec],
        out_specs=[pl.BlockSpec((None, tm, d + LANE), lambda b, i: (b, i, 0)),
                   pl.BlockSpec((None, tm, LANE), lambda b, i: (b, i, 0))],
        out_shape=[jax.ShapeDtypeStruct((n_batch, nt, d + LANE), F32),
                   jax.ShapeDtypeStruct((n_batch, nt, LANE), F32)],
        compiler_params=_params(("arbitrary", "arbitrary"), 48),
        name="norm2_router",
    )(xs, g, modl, modl, *_split3(wr_pad))


def _select_kernel(aff_ref, l_ref, cnt_ref, off_ref, bits_ref, *, row0, n, cap):
    nc = n // LANE
    bits_ref[...] = lax.bitcast_convert_type(aff_ref[row0:row0 + n, :], I32)

    def bit_step(i, thr):
        cand = thr | (jnp.int32(1) << (30 - i))
        cnt = jnp.sum((bits_ref[...] >= cand).astype(I32), axis=0, keepdims=True)
        return jnp.where(cnt >= cap, cand, thr)

    thr = lax.fori_loop(0, 31, bit_step, jnp.zeros((1, LANE), I32))
    n_gt = jnp.sum((bits_ref[...] > thr).astype(I32), axis=0, keepdims=True)
    need = (cap - n_gt).astype(F32)

    r_io = lax.broadcasted_iota(I32, (LANE, LANE), 0)
    c_io = lax.broadcasted_iota(I32, (LANE, LANE), 1)
    ltri = (c_io <= r_io).astype(BF16)
    t_col = r_io.astype(F32)
    j_row = c_io.astype(F32)

    def chunk(c, carry):
        run_eq, run_sel = carry
        r0 = pl.multiple_of(c * LANE, LANE)
        bc = bits_ref[pl.ds(r0, LANE), :]
        eq = bc == thr
        eq_f = eq.astype(F32)
        incl_eq = _dot(ltri, eq_f.astype(BF16))
        rank = run_eq + incl_eq - eq_f
        sel = (bc > thr) | (eq & (rank < need))
        sel_f = sel.astype(F32)
        incl = _dot(ltri, sel_f.astype(BF16))
        cnt_c = incl[LANE - 1:LANE, :]
        qm = jnp.where(sel, incl - sel_f, -1.0)
        for e in range(N_EXPERTS):
            pos = jnp.broadcast_to(qm[:, e:e + 1], (LANE, LANE))
            local = jnp.sum(jnp.where(pos == j_row, t_col, 0.0), axis=0, keepdims=True)
            l_ref[e, pl.ds(c, 1), :] = local.astype(I32)
        cnt_ref[pl.ds(c, 1), :] = cnt_c.astype(I32)
        off_ref[pl.ds(c, 1), :] = run_sel.astype(I32)
        return run_eq + incl_eq[LANE - 1:LANE, :], run_sel + cnt_c

    zero = jnp.zeros((1, LANE), F32)
    lax.fori_loop(0, nc, chunk, (zero, zero))


def _select(affp, row0, n, cap):
    n_batch, nt, _ = affp.shape
    nc = n // LANE
    kern = functools.partial(_select_kernel, row0=row0, n=n, cap=cap)
    return pl.pallas_call(
        kern,
        grid=(n_batch,),
        in_specs=[pl.BlockSpec((None, nt, LANE), lambda b: (b, 0, 0))],
        out_specs=[pl.BlockSpec((None, N_EXPERTS, nc, LANE), lambda b: (b, 0, 0, 0)),
                   pl.BlockSpec((None, nc, LANE), lambda b: (b, 0, 0)),
                   pl.BlockSpec((None, nc, LANE), lambda b: (b, 0, 0))],
        out_shape=[jax.ShapeDtypeStruct((n_batch, N_EXPERTS, nc, LANE), I32),
                   jax.ShapeDtypeStruct((n_batch, nc, LANE), I32),
                   jax.ShapeDtypeStruct((n_batch, nc, LANE), I32)],
        scratch_shapes=[pltpu.VMEM((n, LANE), I32)],
        compiler_params=_params(("arbitrary",), 40),
        name="ec_select",
    )(affp)


MOE_ROWS = 256


def _moe_kernel(l_hbm, cnt_hbm, off_hbm, xm_hbm, xs_in_hbm, g2_ref, w1_ref, w3_ref, w2_ref, xs_hbm,
                l_s, cnt_s, off_s, idx_s, xg, rows, xb, sem, *, row0, nc, cap, g_row):
    del xs_in_hbm
    b = pl.program_id(0)
    e = pl.program_id(1)
    tables = (pltpu.make_async_copy(l_hbm.at[b, e], l_s, sem.at[0]),
              pltpu.make_async_copy(cnt_hbm.at[b], cnt_s, sem.at[1]),
              pltpu.make_async_copy(off_hbm.at[b], off_s, sem.at[2]))
    for cp in tables:
        cp.start()
    for cp in tables:
        cp.wait()

    def gather_copies(tok, slot, n_rows=1):
        return (pltpu.make_async_copy(xm_hbm.at[b, pl.ds(tok, n_rows), :], xg.at[pl.ds(slot, n_rows), :], sem.at[0]),
                pltpu.make_async_copy(xs_hbm.at[b, pl.ds(tok, n_rows), :], rows.at[pl.ds(slot, n_rows), :],
                                      sem.at[1]))

    def scatter_copy(tok, slot, n_rows=1):
        return pltpu.make_async_copy(rows.at[pl.ds(slot, n_rows), :], xs_hbm.at[b, pl.ds(tok, n_rows), :], sem.at[2])

    def chunk(c, _):
        base = row0 + c * LANE
        off_c = off_s[c, e]

        def one(j, _):
            idx_s[off_c + j] = base + l_s[c, j]
            return 0

        lax.fori_loop(0, cnt_s[c, e], one, 0)
        return 0

    lax.fori_loop(0, nc, chunk, 0)

    def for_selected(fn):
        def one(s, _):
            fn(idx_s[s], s)
            return 0

        lax.fori_loop(0, cap, one, 0, unroll=8)

    def start_gather(tok, slot):
        for cp in gather_copies(tok, slot):
            cp.start()

    for_selected(start_gather)
    for cp in gather_copies(row0, 0, cap):
        cp.wait()

    d = rows.shape[1]
    tn = _tile(d, 512)
    rc = min(cap, MOE_ROWS)
    lane = lax.broadcasted_iota(I32, (1, LANE), 1)
    g2_row = g_row if g_row is not None else b

    def row_block(r, _):
        rs = pl.ds(pl.multiple_of(r * rc, rc), rc)
        for kc in range(d // tn):
            ks = slice(kc * tn, (kc + 1) * tn)
            xb[:, ks] = xg[rs, ks].astype(BF16)
        h1 = _dot(xb[...], w1_ref[...])
        h3 = _dot(xb[...], w3_ref[...])
        gate = jnp.sum(jnp.where(lane == e, xg[rs, d:d + LANE], 0.0), axis=-1, keepdims=True)
        hdn = (h1 * jax.nn.sigmoid(h1) * h3 * gate).astype(BF16)
        for c in range(d // tn):
            sl = slice(c * tn, (c + 1) * tn)
            rows[rs, sl] = rows[rs, sl] + g2_ref[pl.ds(g2_row, 1), sl] * _dot(hdn, w2_ref[:, sl])
        return 0

    lax.fori_loop(0, cap // rc, row_block, 0)

    for_selected(lambda tok, slot: scatter_copy(tok, slot).start())
    scatter_copy(row0, 0, cap).wait()


def _moe(ltab, cnt, off, xma, xs, modl, w1, w3, w2, row0, cap, g_row):
    n_batch, nt, d = xs.shape
    nc = ltab.shape[2]
    nd = 1
    rc = min(cap, MOE_ROWS)
    assert cap % rc == 0
    kern = functools.partial(_moe_kernel, row0=row0, nc=nc, cap=cap, g_row=g_row)
    any_ = pl.BlockSpec(memory_space=pl.ANY)
    return pl.pallas_call(
        kern,
        grid=(n_batch, N_EXPERTS),
        in_specs=[any_, any_, any_, any_, any_,
                  pl.BlockSpec((MOD_ROWS, d), lambda b, e: (0, 5 * nd)),
                  pl.BlockSpec((None, d, EXPERT_FF), lambda b, e: (e, 0, 0)),
                  pl.BlockSpec((None, d, EXPERT_FF), lambda b, e: (e, 0, 0)),
                  pl.BlockSpec((None, EXPERT_FF, d), lambda b, e: (e, 0, 0))],
        out_specs=any_,
        out_shape=jax.ShapeDtypeStruct(xs.shape, F32),
        scratch_shapes=[pltpu.SMEM((nc, LANE), I32), pltpu.SMEM((nc, LANE), I32), pltpu.SMEM((nc, LANE), I32),
                        pltpu.SMEM((cap,), I32), pltpu.VMEM((cap, d + LANE), F32), pltpu.VMEM((cap, d), F32),
                        pltpu.VMEM((rc, d), BF16), pltpu.SemaphoreType.DMA((3,))],
        input_output_aliases={4: 0},
        compiler_params=_params(("arbitrary", "arbitrary"), 56),
        name="moe_ffn",
    )(ltab, cnt, off, xma, xs, modl, w1, w3, w2)


def _rope_tables(n, lc):
    t = jnp.arange(n)
    pos = jnp.stack([t // GRID_W, t % GRID_W], axis=-1).astype(F32)

    def tables(dh):
        quarter = dh // 4
        inv_freq = ROPE_THETA ** (-jnp.arange(quarter, dtype=F32) / quarter)
        ang = pos[:, :, None] * inv_freq
        cos, sin = jnp.cos(ang), jnp.sin(ang)
        c = jnp.concatenate([cos[:, 0], cos[:, 0], cos[:, 1], cos[:, 1]], axis=-1)
        s = jnp.concatenate([-sin[:, 0], sin[:, 0], -sin[:, 1], sin[:, 1]], axis=-1)
        reps = LANE // dh
        c, s = jnp.tile(c, (1, reps)), jnp.tile(s, (1, reps))
        return (jnp.concatenate([jnp.ones((lc, LANE), F32), c], axis=0),
                jnp.concatenate([jnp.zeros((lc, LANE), F32), s], axis=0))

    return tables(HEAD_DIM) + tables(DIFF_QK_DIM)


def kernel(x, c, ctx, c_ctx, w_ada, b_ada, norm_g, w_in, w_out, qk_g_na, na_rpb, qk_g_diff, diff_lambda,
           diff_subln_g, qk_g_swa, swa_sink, lru_conv_w, lru_conv_b, lru_gate_w, lru_gate_b, lru_lam,
           w_router, w1, w3, w2):
    n_batch, n, d = x.shape
    lc = ctx.shape[1]
    depth = w_ada.shape[0]
    assert n_batch < MOD_ROWS and w_in.shape[2] == D_IN

    cvec = jnp.zeros((MOD_ROWS, d), F32).at[:n_batch].set(c).at[n_batch].set(c_ctx)
    mod = _ada(cvec, w_ada, b_ada)
    cs, ss, cd, sd = _rope_tables(n, lc)
    xs = jnp.concatenate([ctx, x], axis=1)

    cap_lat = CAPACITY_FACTOR * n // N_EXPERTS
    cap_ctx = CAPACITY_FACTOR * lc // N_EXPERTS

    for l in range(depth):
        modl = mod[l]
        u = _inproj(xs, norm_g[l, 0].reshape(1, d), modl, w_in[l].astype(BF16), lc)

        pvec = jnp.concatenate([qk_g_na[l], jnp.tile(qk_g_diff[l], (1, 2)), qk_g_swa[l],
                                diff_subln_g[l].reshape(1, LANE), jnp.zeros((1, LANE), F32)], axis=0)
        qa, ka, va, qd, kd, vd, qs, ks, vs = _prep(u, pvec, cs, ss, cd, sd)

        gmax = lambda g: jnp.max(jnp.abs(g[0])) * jnp.max(jnp.abs(g[1]))
        b_na = (BOUND_SLACK * HEAD_DIM * ATTN_Q_SCALE * gmax(qk_g_na[l])
                + LOG2E * jnp.max(jnp.abs(na_rpb[l])))
        b_swa = jnp.maximum(BOUND_SLACK * HEAD_DIM * ATTN_Q_SCALE * gmax(qk_g_swa[l]),
                            LOG2E * jnp.max(jnp.abs(swa_sink[l])))
        b_diff = BOUND_SLACK * DIFF_QK_DIM * DIFF_Q_SCALE * gmax(qk_g_diff[l])

        o_a = _na(b_na, qa, ka, va, _na_bias_tables(na_rpb[l], n), lc)

        lambda_init = 0.8 - 0.6 * math.exp(-0.3 * l)
        dl = diff_lambda[l].astype(F32)
        lam = jnp.exp(jnp.sum(dl[0] * dl[1])) - jnp.exp(jnp.sum(dl[2] * dl[3])) + lambda_init
        o_d = _diff(jnp.stack([lam, b_diff]).astype(F32), qd, kd, vd, pvec, lc, 1.0 - lambda_init)

        o_s = _swa(b_swa, qs, ks, vs, swa_sink[l], lc)

        af, bf, ab, bb = _lru_gates(u, lru_conv_w[l], lru_conv_b[l], lru_gate_w[l], lru_gate_b[l], lru_lam[l], lc)
        hf, hb = _lru_scan(af, bf, ab, bb, lc)

        xs = _outproj(o_a, o_d, o_s, hf, hb, u, modl, w_out[l].astype(BF16), xs, lc)

        wr_pad = jnp.zeros((d, LANE), F32).at[:, :N_EXPERTS].set(w_router[l])
        xma, affp = _router(xs, norm_g[l, 1].reshape(1, d), modl, wr_pad, lc)
        w1b, w3b, w2b = w1[l].astype(BF16), w3[l].astype(BF16), w2[l].astype(BF16)
        lt, cnt, off = _select(affp, lc, n, cap_lat)
        xs = _moe(lt, cnt, off, xma, xs, modl, w1b, w3b, w2b, lc, cap_lat, None)
        if l < depth - 1:
            lt, cnt, off = _select(affp, 0, lc, cap_ctx)
            xs = _moe(lt, cnt, off, xma, xs, modl, w1b, w3b, w2b, 0, cap_ctx, n_batch)
    return xs[:, lc:, :]
```

```python
import functools
import math

import numpy as np
import jax
import jax.numpy as jnp
from jax import lax
from jax.experimental import pallas as pl
from jax.experimental.pallas import tpu as pltpu

GRID_W = 64
HEAD_DIM = 128
NA_HEADS = 8
NA_KH = 8
NA_KW = 16
DIFF_HEADS = 8
DIFF_QK_DIM = 64
SWA_Q_HEADS = 8
SWA_KV_HEADS = 2
SWA_GROUP = SWA_Q_HEADS // SWA_KV_HEADS
SWA_WINDOW = 128
LRU_WIDTH = 1024
LRU_BLOCKS = 8
LRU_C = 8.0
N_EXPERTS = 16
EXPERT_FF = 256
CAPACITY_FACTOR = 2
ROPE_THETA = 10000.0
NEG_INF = -1e30
EPS = 1e-6

OFF_QA, OFF_KA, OFF_VA = 0, 1024, 2048
OFF_QD, OFF_KD, OFF_VD = 3072, 4096, 5120
OFF_QS, OFF_KS, OFF_VS = 6144, 7168, 7424
OFF_XR, OFF_GR = 7680, 8704
D_ATT = 7680
D_IN = 9728
D_MIX = 4096

LOG2E = 1.4426950408889634
DIFF_Q_SCALE = DIFF_QK_DIM ** -0.5 * LOG2E
ATTN_FAST_BOUND = 60.0
ATTN_Q_SCALE = HEAD_DIM ** -0.5 * LOG2E
BOUND_SLACK = 1.02

LANE = 128
MOD_ROWS = 8
V7X_VMEM_BYTES = 64 * 1024 * 1024

F32 = jnp.float32
BF16 = jnp.bfloat16
I32 = jnp.int32


def _tile(n, target, mult=LANE):
    best = None
    for t in range(mult, min(n, target) + 1, mult):
        if n % t == 0:
            best = t
    assert best is not None, (n, target, mult)
    return best


def _params(sem, vmem_mb):
    return pltpu.CompilerParams(dimension_semantics=sem, vmem_limit_bytes=min(vmem_mb, 60) * 1024 * 1024)


def _dot_nt(a, b):
    return lax.dot_general(a, b, (((1,), (1,)), ((), ())), preferred_element_type=F32)


def _dot(a, b):
    return jnp.dot(a, b, preferred_element_type=F32)


def _ada_kernel(c_ref, w_ref, b_ref, o_ref):
    cv = c_ref[...]
    s = cv * jax.nn.sigmoid(cv)
    o_ref[...] = jnp.dot(s, w_ref[...], preferred_element_type=F32,
                         precision=lax.Precision.HIGHEST) + b_ref[...]


def _ada(cvec, w_ada, b_ada):
    depth, d, n6 = w_ada.shape
    tn = _tile(n6, 512)
    return pl.pallas_call(
        _ada_kernel,
        grid=(depth, n6 // tn),
        in_specs=[pl.BlockSpec((MOD_ROWS, d), lambda l, j: (0, 0)),
                  pl.BlockSpec((None, d, tn), lambda l, j: (l, 0, j)),
                  pl.BlockSpec((None, 1, tn), lambda l, j: (l, 0, j))],
        out_specs=pl.BlockSpec((None, MOD_ROWS, tn), lambda l, j: (l, 0, j)),
        out_shape=jax.ShapeDtypeStruct((depth, MOD_ROWS, n6), F32),
        compiler_params=_params(("arbitrary", "arbitrary"), 40),
        name="ada_mod",
    )(cvec, w_ada, b_ada.reshape(depth, 1, n6))


ROW_CHUNK = 32


def _mod_row(ref, row0, b, n_batch, lc):
    return jnp.where(row0 < lc, ref[pl.ds(n_batch, 1), :], ref[pl.ds(b, 1), :])


def _norm_modulate(x, g, sh, sc):
    ms = jnp.mean(x * x, axis=-1, keepdims=True)
    return x * lax.rsqrt(ms + EPS) * g * (1.0 + sc) + sh


def _for_row_chunks(tm, fn):
    assert tm % ROW_CHUNK == 0

    def body(c, _):
        fn(pl.multiple_of(c * ROW_CHUNK, ROW_CHUNK))
        return 0

    lax.fori_loop(0, tm // ROW_CHUNK, body, 0)


def _inproj_kernel(x_ref, g_ref, sh_ref, sc_ref, w_ref, o_ref, xn_ref, *, n_batch, lc, tm):
    b = pl.program_id(0)
    i = pl.program_id(1)

    @pl.when(pl.program_id(2) == 0)
    def _():
        def chunk(r0):
            sh = _mod_row(sh_ref, i * tm + r0, b, n_batch, lc)
            sc = _mod_row(sc_ref, i * tm + r0, b, n_batch, lc)
            xn = _norm_modulate(x_ref[pl.ds(r0, ROW_CHUNK), :], g_ref[...], sh, sc)
            xn_ref[pl.ds(r0, ROW_CHUNK), :] = xn.astype(BF16)

        _for_row_chunks(tm, chunk)

    o_ref[...] = _dot(xn_ref[...], w_ref[...])


def _inproj(xs, g, modl, w_bf, lc):
    n_batch, nt, d = xs.shape
    d_in = w_bf.shape[1]
    tm = _tile(nt, 768)
    tn = _tile(d_in, 512)
    assert lc % ROW_CHUNK == 0
    kern = functools.partial(_inproj_kernel, n_batch=n_batch, lc=lc, tm=tm)
    return pl.pallas_call(
        kern,
        grid=(n_batch, nt // tm, d_in // tn),
        in_specs=[pl.BlockSpec((None, tm, d), lambda b, i, j: (b, i, 0)),
                  pl.BlockSpec((1, d), lambda b, i, j: (0, 0)),
                  pl.BlockSpec((MOD_ROWS, d), lambda b, i, j: (0, 0)),
                  pl.BlockSpec((MOD_ROWS, d), lambda b, i, j: (0, 1)),
                  pl.BlockSpec((d, tn), lambda b, i, j: (0, j))],
        out_specs=pl.BlockSpec((None, tm, tn), lambda b, i, j: (b, i, j)),
        out_shape=jax.ShapeDtypeStruct((n_batch, nt, d_in), F32),
        scratch_shapes=[pltpu.VMEM((tm, d), BF16)],
        compiler_params=_params(("arbitrary", "arbitrary", "arbitrary"), 56),
        name="in_proj",
    )(xs, g, modl, modl, w_bf)


def _rms_full(x, g):
    ms = jnp.mean(x * x, axis=-1, keepdims=True)
    return x * lax.rsqrt(ms + EPS) * g


def _rms_halves(x, g):
    lo = lax.broadcasted_iota(I32, (1, LANE), 1) < DIFF_QK_DIM
    x2 = x * x
    s_lo = jnp.sum(jnp.where(lo, x2, 0.0), axis=-1, keepdims=True)
    s_hi = jnp.sum(jnp.where(lo, 0.0, x2), axis=-1, keepdims=True)
    ms = jnp.where(lo, s_lo, s_hi) * (1.0 / DIFF_QK_DIM)
    return x * lax.rsqrt(ms + EPS) * g


def _rope(x, cos, sin_signed, half):
    lane = lax.broadcasted_iota(I32, (1, LANE), 1)
    up = pltpu.roll(x, LANE - half, axis=1)
    dn = pltpu.roll(x, half, axis=1)
    partner = jnp.where((lane & half) == 0, up, dn)
    return x * cos + partner * sin_signed


def _prep_kernel(u_ref, p_ref, cs_ref, ss_ref, cd_ref, sd_ref,
                 qa_ref, ka_ref, va_ref, qd_ref, kd_ref, vd_ref, qs_ref, ks_ref, vs_ref):
    g_na_q, g_na_k = p_ref[0:1, :], p_ref[1:2, :]
    g_d_q, g_d_k = p_ref[2:3, :], p_ref[3:4, :]
    g_s_q, g_s_k = p_ref[4:5, :], p_ref[5:6, :]
    cs, ss, cd, sd = cs_ref[...], ss_ref[...], cd_ref[...], sd_ref[...]
    lo = lax.broadcasted_iota(I32, (1, LANE), 1) < DIFF_QK_DIM

    def col(off, h):
        return u_ref[:, off + h * LANE: off + (h + 1) * LANE]

    def put(ref, h, val):
        ref[:, h * LANE:(h + 1) * LANE] = val.astype(BF16)

    for h in range(NA_HEADS):
        put(qa_ref, h, _rms_full(col(OFF_QA, h), g_na_q) * ATTN_Q_SCALE)
        put(ka_ref, h, _rms_full(col(OFF_KA, h), g_na_k))
        put(va_ref, h, col(OFF_VA, h))
    ones_col = jnp.where(lax.broadcasted_iota(I32, (u_ref.shape[0], LANE), 1) == 0, 1.0, 0.0).astype(BF16)
    for h in range(DIFF_HEADS):
        q = _rope(_rms_halves(col(OFF_QD, h), g_d_q), cd, sd, DIFF_QK_DIM // 4) * DIFF_Q_SCALE
        qd_ref[0, :, h * LANE:(h + 1) * LANE] = jnp.where(lo, q, 0.0).astype(BF16)
        qd_ref[1, :, h * LANE:(h + 1) * LANE] = jnp.where(lo, 0.0, q).astype(BF16)
        put(kd_ref, h, _rope(_rms_halves(col(OFF_KD, h), g_d_k), cd, sd, DIFF_QK_DIM // 4))
        put(vd_ref, 2 * h, col(OFF_VD, h))
        vd_ref[:, (2 * h + 1) * LANE:(2 * h + 2) * LANE] = ones_col
    for h in range(SWA_Q_HEADS):
        put(qs_ref, h, _rope(_rms_full(col(OFF_QS, h), g_s_q), cs, ss, HEAD_DIM // 4) * ATTN_Q_SCALE)
    for h in range(SWA_KV_HEADS):
        put(ks_ref, h, _rope(_rms_full(col(OFF_KS, h), g_s_k), cs, ss, HEAD_DIM // 4))
        put(vs_ref, h, col(OFF_VS, h))


def _prep(u, pvec, cs, ss, cd, sd):
    n_batch, nt, _ = u.shape
    tr = _tile(nt, 256)
    row = lambda w: pl.BlockSpec((None, tr, w), lambda b, i: (b, i, 0))
    tab = pl.BlockSpec((tr, LANE), lambda b, i: (i, 0))
    sd_ = lambda w: jax.ShapeDtypeStruct((n_batch, nt, w), BF16)
    return pl.pallas_call(
        _prep_kernel,
        grid=(n_batch, nt // tr),
        in_specs=[row(D_ATT), pl.BlockSpec((8, LANE), lambda b, i: (0, 0)), tab, tab, tab, tab],
        out_specs=[row(1024), row(1024), row(1024),
                   pl.BlockSpec((None, 2, tr, 1024), lambda b, i: (b, 0, i, 0)),
                   row(1024), row(2048), row(1024), row(256), row(256)],
        out_shape=[sd_(1024), sd_(1024), sd_(1024),
                   jax.ShapeDtypeStruct((n_batch, 2, nt, 1024), BF16),
                   sd_(1024), sd_(2048), sd_(1024), sd_(256), sd_(256)],
        compiler_params=_params(("arbitrary", "arbitrary"), 48),
        name="qkv_prep",
    )(u, pvec, cs, ss, cd, sd)


NA_QBLK = 2 * GRID_W
NA_KROWS = 10
NA_KBLK = NA_KROWS * GRID_W


def _na_kernel(sc_ref, q_ref, k_ref, v_ref, *rest, lc, nlb, group, online):
    bias_refs, o_ref = rest[:group], rest[group]
    i0 = pl.program_id(2) * group
    kc = k_ref[0:lc, :]
    vc = v_ref[0:lc, :]
    for g in range(group):
        li = i0 + g - lc // NA_QBLK
        kb = jnp.clip(li - 2, 0, nlb - NA_KROWS // 2)
        start = pl.multiple_of(lc + kb * NA_QBLK, NA_QBLK)
        rows = slice(g * NA_QBLK, (g + 1) * NA_QBLK)
        q = q_ref[rows, :]
        vw = v_ref[pl.ds(start, NA_KBLK), :]
        s_n = _dot_nt(q, k_ref[pl.ds(start, NA_KBLK), :]) + bias_refs[g][...]
        s_c = _dot_nt(q, kc)
        if online:
            m = jnp.maximum(jnp.max(s_n, axis=-1, keepdims=True), jnp.max(s_c, axis=-1, keepdims=True))
        else:
            m = sc_ref[0]
        p_n = jnp.exp2(s_n - m)
        p_c = jnp.exp2(s_c - m)
        l = jnp.sum(p_n, axis=-1, keepdims=True) + jnp.sum(p_c, axis=-1, keepdims=True)
        o = _dot(p_n.astype(BF16), vw) + _dot(p_c.astype(BF16), vc)
        o_ref[rows, :] = (o / l).astype(BF16)


def _na_bias_tables(rpb, n):
    rows = n // GRID_W
    nlb = rows // 2
    reps = [0, 1, 2, nlb - 2, nlb - 1]
    h = rpb.shape[0]
    cols = np.arange(GRID_W)
    cstart = np.clip(cols - NA_KW // 2, 0, GRID_W - NA_KW)
    col_ok = (cols[None, :] >= cstart[:, None]) & (cols[None, :] < cstart[:, None] + NA_KW)
    dx = cols[None, :] - cols[:, None] + (NA_KW - 1)
    sel = (dx[None] == np.arange(2 * NA_KW - 1)[:, None, None]) & col_ok[None]
    tx = jnp.einsum('hyx,xqk->hyqk', rpb.astype(F32), jnp.asarray(sel, F32), precision=lax.Precision.HIGHEST)
    dy_i = np.zeros((5, 2, NA_KROWS), np.int32)
    row_ok = np.zeros((5, 2, NA_KROWS), bool)
    for c, li in enumerate(reps):
        kb = int(np.clip(li - 2, 0, nlb - NA_KROWS // 2))
        qr = 2 * li + np.arange(2)
        kr = 2 * kb + np.arange(NA_KROWS)
        rs = np.clip(qr - NA_KH // 2, 0, rows - NA_KH)
        row_ok[c] = (kr[None, :] >= rs[:, None]) & (kr[None, :] < rs[:, None] + NA_KH)
        dy_i[c] = np.clip(kr[None, :] - qr[:, None] + (NA_KH - 1), 0, 2 * NA_KH - 2)
    blocks = jnp.take(tx, jnp.asarray(dy_i.reshape(-1)), axis=1)
    blocks = blocks.reshape(h, 5, 2, NA_KROWS, GRID_W, GRID_W)
    ok = row_ok[:, :, :, None, None] & col_ok[None, None, None]
    tbl = jnp.where(ok[None], blocks * LOG2E, NEG_INF)
    tbl = tbl.transpose(1, 0, 2, 4, 3, 5).reshape(5, h, NA_QBLK, NA_KBLK)
    dead = jnp.full((1,) + tbl.shape[1:], NEG_INF, F32)
    return jnp.concatenate([tbl, dead], axis=0)


def _group(nblocks, target):
    return max(g for g in range(1, target + 1) if nblocks % g == 0)


def _softmax_dispatch(bound, call):
    sc = jnp.reshape(bound, (1,)).astype(F32)
    return lax.cond(bound <= ATTN_FAST_BOUND, lambda s: call(False)(s), lambda s: call(True)(s), sc)


def _na(bound, qa, ka, va, bias, lc):
    n_batch, nt, _ = qa.shape
    n = nt - lc
    nlb = n // NA_QBLK
    assert nlb >= 5 and lc % NA_QBLK == 0
    ncb = lc // NA_QBLK
    group = _group(nt // NA_QBLK, 6)

    def case(i):
        li = i - ncb
        return jnp.where(li < 0, 5, jnp.where(li < 2, li, jnp.where(li >= nlb - 2, li - (nlb - 5), 2)))

    kv = pl.BlockSpec((None, nt, LANE), lambda b, h, i: (b, 0, h))
    qo = pl.BlockSpec((None, group * NA_QBLK, LANE), lambda b, h, i: (b, i, h))
    bias_specs = [pl.BlockSpec((None, None, NA_QBLK, NA_KBLK),
                               functools.partial(lambda b, h, i, g: (case(i * group + g), h, 0, 0), g=g))
                  for g in range(group)]

    def call(online):
        kern = functools.partial(_na_kernel, lc=lc, nlb=nlb, group=group, online=online)
        return lambda sc: pl.pallas_call(
            kern,
            grid=(n_batch, NA_HEADS, nt // (group * NA_QBLK)),
            in_specs=[pl.BlockSpec(memory_space=pltpu.SMEM), qo, kv, kv] + bias_specs,
            out_specs=qo,
            out_shape=jax.ShapeDtypeStruct((n_batch, nt, NA_HEADS * HEAD_DIM), BF16),
            compiler_params=_params(("arbitrary",) * 3, 40),
            name="na_attn_online" if online else "na_attn_offset",
        )(sc, qa, ka, va, *([bias] * group))

    return _softmax_dispatch(bound, call)


SWA_BLK = 128
SWA_WIN = 3 * SWA_BLK


def _swa_kernel(sc_ref, q_ref, k_ref, v_ref, sink_ref, *rest, lc, n, group, online):
    mask_refs, o_ref = rest[:group], rest[group]
    i0 = pl.program_id(2) * group
    kc = k_ref[0:lc, :]
    vc = v_ref[0:lc, :]
    for gq in range(group):
        li = i0 + gq - lc // SWA_BLK
        ws = jnp.clip((li - 1) * SWA_BLK, 0, n - SWA_WIN)
        start = pl.multiple_of(lc + ws, SWA_BLK)
        kw = k_ref[pl.ds(start, SWA_WIN), :]
        vw = v_ref[pl.ds(start, SWA_WIN), :]
        rows = slice(gq * SWA_BLK, (gq + 1) * SWA_BLK)
        mask = mask_refs[gq][...]
        for g in range(SWA_GROUP):
            q = q_ref[rows, g * LANE:(g + 1) * LANE]
            s_w = _dot_nt(q, kw)
            s_c = _dot_nt(q, kc)
            sink = sink_ref[g * SWA_BLK:(g + 1) * SWA_BLK, 0:1]
            if online:
                s_w = jnp.where(mask > 0.0, s_w, NEG_INF)
                m = jnp.maximum(jnp.maximum(jnp.max(s_w, axis=-1, keepdims=True),
                                            jnp.max(s_c, axis=-1, keepdims=True)), sink)
                p_w = jnp.exp2(s_w - m)
            else:
                m = sc_ref[0]
                p_w = jnp.exp2(s_w - m) * mask
            p_c = jnp.exp2(s_c - m)
            l = jnp.sum(p_w, axis=-1, keepdims=True) + jnp.sum(p_c, axis=-1, keepdims=True) + jnp.exp2(sink - m)
            o = (_dot(p_w.astype(BF16), vw) + _dot(p_c.astype(BF16), vc)) / l
            o_ref[rows, g * LANE:(g + 1) * LANE] = o.astype(BF16)


def _swa_masks():
    r = np.arange(SWA_BLK)[:, None]
    j = np.arange(SWA_WIN)[None, :]
    rel = [j - r, j - r - SWA_BLK, j - r - 2 * SWA_BLK]
    tabs = [(np.abs(d) <= SWA_WINDOW) for d in rel] + [np.zeros((SWA_BLK, SWA_WIN), bool)]
    return jnp.asarray(np.stack(tabs), F32)


def _swa(bound, qs, ks, vs, sink, lc):
    n_batch, nt, _ = qs.shape
    n = nt - lc
    nb = n // SWA_BLK
    assert nb >= 3 and lc % SWA_BLK == 0
    ncb = lc // SWA_BLK
    rows = SWA_GROUP * SWA_BLK
    group = _group(nt // SWA_BLK, 3)
    sink_rows = jnp.broadcast_to(
        jnp.repeat((sink.astype(F32) * LOG2E).reshape(SWA_KV_HEADS, SWA_GROUP), SWA_BLK, axis=1)[:, :, None],
        (SWA_KV_HEADS, rows, LANE))
    masks = _swa_masks()

    def case(i):
        li = i - ncb
        return jnp.where(li < 0, 3, jnp.where(li == 0, 0, jnp.where(li == nb - 1, 2, 1)))

    kv = pl.BlockSpec((None, nt, LANE), lambda b, k, i: (b, 0, k))
    qo = pl.BlockSpec((None, group * SWA_BLK, SWA_GROUP * LANE), lambda b, k, i: (b, i, k))
    mask_specs = [pl.BlockSpec((None, SWA_BLK, SWA_WIN),
                               functools.partial(lambda b, k, i, g: (case(i * group + g), 0, 0), g=g))
                  for g in range(group)]

    def call(online):
        kern = functools.partial(_swa_kernel, lc=lc, n=n, group=group, online=online)
        return lambda sc: pl.pallas_call(
            kern,
            grid=(n_batch, SWA_KV_HEADS, nt // (group * SWA_BLK)),
            in_specs=[pl.BlockSpec(memory_space=pltpu.SMEM), qo, kv, kv,
                      pl.BlockSpec((None, rows, LANE), lambda b, k, i: (k, 0, 0))] + mask_specs,
            out_specs=qo,
            out_shape=jax.ShapeDtypeStruct((n_batch, nt, SWA_Q_HEADS * HEAD_DIM), BF16),
            compiler_params=_params(("arbitrary",) * 3, 32),
            name="swa_attn_online" if online else "swa_attn_offset",
        )(sc, qs, ks, vs, sink_rows, *([masks] * group))

    return _softmax_dispatch(bound, call)


def _diff_finish(sc_ref, g_ref, o_ref, num, den, tq, post_scale):
    o2 = num / den
    o = o2[0:tq, :] - sc_ref[0] * o2[tq:2 * tq, :]
    o_ref[...] = (_rms_full(o, g_ref[6:7, :]) * post_scale).astype(BF16)


def _diff_online_kernel(sc_ref, q_ref, k_ref, v_ref, g_ref, o_ref, *, lc, nt, tq, tk, post_scale):
    qi = pl.program_id(2)
    q = q_ref[...].reshape(2 * tq, LANE)
    nk = jnp.where(qi < lc // tq, lc // tk, nt // tk)

    def body(kt, carry):
        m, l, acc = carry
        k0 = pl.multiple_of(kt * tk, tk)
        k = k_ref[pl.ds(k0, tk), :]
        v = v_ref[pl.ds(k0, tk), :]
        s = _dot_nt(q, k)
        m_new = jnp.maximum(m, jnp.max(s, axis=-1, keepdims=True))
        alpha = jnp.exp2(m - m_new)
        p = jnp.exp2(s - m_new)
        l = alpha * l + jnp.sum(p, axis=-1, keepdims=True)
        acc = alpha * acc + _dot(p.astype(BF16), v)
        return m_new, l, acc

    init = (jnp.full((2 * tq, 1), NEG_INF, F32), jnp.zeros((2 * tq, 1), F32), jnp.zeros((2 * tq, LANE), F32))
    _, l, acc = lax.fori_loop(0, nk, body, init)
    _diff_finish(sc_ref, g_ref, o_ref, acc, l, tq, post_scale)


def _diff_offset_kernel(sc_ref, q_ref, qn_ref, k_ref, v_ref, g_ref, o_ref, acc_ref, p_ref,
                        *, lc, nt, tq, tk, post_scale):
    qi = pl.program_id(2)
    rows2 = 2 * tq
    nk = nt // tk
    ncq = lc // tq
    q = q_ref[...].reshape(rows2, LANE)
    off = sc_ref[1]

    def probs(qq, k0, tkk):
        return jnp.exp2(_dot_nt(qq, k_ref[k0:k0 + tkk, :]) - off).astype(BF16)

    def next_tile_probs():
        return probs(qn_ref[...].reshape(rows2, LANE), 0, tk)

    @pl.when(qi < ncq)
    def _():
        acc_ref[...] = _dot(probs(q, 0, lc), v_ref[0:lc, :])

        @pl.when(qi == ncq - 1)
        def _():
            p_ref[0] = next_tile_probs()

    def latent(first_slot):
        tot = None
        for t in range(nk):
            pv = _dot(p_ref[(t + first_slot) % 2], v_ref[t * tk:(t + 1) * tk, :])
            p_ref[(t + 1 + first_slot) % 2] = probs(q, (t + 1) * tk, tk) if t + 1 < nk else next_tile_probs()
            tot = pv if tot is None else tot + pv
        acc_ref[...] = tot

    first_slot = ((qi - ncq) * nk) % 2
    for slot in range(2 if nk % 2 else 1):
        @pl.when((qi >= ncq) & (first_slot == slot))
        def _():
            latent(slot)

    acc = acc_ref[...]
    _diff_finish(sc_ref, g_ref, o_ref, acc[:, 0:LANE], acc[:, LANE:LANE + 1], tq, post_scale)


def _diff(sc, qd, kd, vde, pvec, lc, post_scale):
    n_batch, _, nt, _ = qd.shape
    tq = _tile(math.gcd(lc, nt), 256)
    assert lc % tq == 0
    grid = (n_batch, DIFF_HEADS, nt // tq)
    smem = pl.BlockSpec(memory_space=pltpu.SMEM)
    qspec = pl.BlockSpec((None, 2, tq, LANE), lambda b, h, i: (b, 0, i, h))
    kspec = pl.BlockSpec((None, nt, LANE), lambda b, h, i: (b, 0, h))
    gspec = pl.BlockSpec((8, LANE), lambda b, h, i: (0, 0))
    ospec = pl.BlockSpec((None, tq, LANE), lambda b, h, i: (b, i, h))
    oshape = jax.ShapeDtypeStruct((n_batch, nt, DIFF_HEADS * HEAD_DIM), BF16)

    def offset(sc, qd, kd, vde, pvec):
        tk = _tile(nt, 768)
        kern = functools.partial(_diff_offset_kernel, lc=lc, nt=nt, tq=tq, tk=tk, post_scale=post_scale)
        nq = nt // tq
        qnext = pl.BlockSpec((None, 2, tq, LANE), lambda b, h, i: (b, 0, jnp.minimum(i + 1, nq - 1), h))
        return pl.pallas_call(
            kern, grid=grid,
            in_specs=[smem, qspec, qnext, kspec, pl.BlockSpec((None, nt, 2 * LANE), lambda b, h, i: (b, 0, h)),
                      gspec],
            out_specs=ospec, out_shape=oshape,
            scratch_shapes=[pltpu.VMEM((2 * tq, 2 * LANE), F32), pltpu.VMEM((2, 2 * tq, tk), BF16)],
            compiler_params=_params(("arbitrary",) * 3, 40),
            name="diff_attn_offset",
        )(sc, qd, qd, kd, vde, pvec)

    def online(sc, qd, kd, vde, pvec):
        tk = _tile(math.gcd(lc, nt), 256)
        kern = functools.partial(_diff_online_kernel, lc=lc, nt=nt, tq=tq, tk=tk, post_scale=post_scale)
        return pl.pallas_call(
            kern, grid=grid,
            in_specs=[smem, qspec, kspec, pl.BlockSpec((None, nt, LANE), lambda b, h, i: (b, 0, 2 * h)), gspec],
            out_specs=ospec, out_shape=oshape,
            compiler_params=_params(("arbitrary",) * 3, 32),
            name="diff_attn_online",
        )(sc, qd, kd, vde, pvec)

    return lax.cond(sc[1] <= ATTN_FAST_BOUND, offset, online, sc, qd, kd, vde, pvec)


LRU_HALO = 8


LRU_STEP_BLOCKS = 4


def _lru_gate_kernel(xp_ref, x_ref, xn_ref, cw_ref, cb_ref, gw_ref, gb_ref, lam_ref,
                     af_ref, bf_ref, ab_ref, bb_ref, *, t, seg_starts, seg_ends):
    k = pl.program_id(2)
    is_start = functools.reduce(jnp.logical_or, [k == s for s in seg_starts])
    is_end = functools.reduce(jnp.logical_or, [k == s for s in seg_ends])
    h = LRU_HALO
    outs = ((af_ref, bf_ref), (ab_ref, bb_ref))
    sigmoid = lambda z: 0.5 + 0.5 * jnp.tanh(0.5 * z)
    for c in range(LRU_STEP_BLOCKS):
        cs = slice(c * LANE, (c + 1) * LANE)
        cur = x_ref[:, cs]
        prev = jnp.where(is_start, 0.0, xp_ref[:, cs])
        nxt = jnp.where(is_end, 0.0, xn_ref[:, cs])
        ext = jnp.concatenate([prev, cur, nxt], axis=0)
        xl = (cb_ref[:, cs] + cw_ref[0:1, cs] * ext[h - 2:h - 2 + t, :] + cw_ref[1:2, cs] * ext[h - 1:h - 1 + t, :]
              + cw_ref[2:3, cs] * cur + cw_ref[3:4, cs] * ext[h + 1:h + 1 + t, :])
        xb = xl.astype(BF16)
        for d in range(2):
            r = sigmoid(_dot(xb, gw_ref[d, 0, c].astype(BF16)) + gb_ref[2 * d:2 * d + 1, cs])
            ig = sigmoid(_dot(xb, gw_ref[d, 1, c].astype(BF16)) + gb_ref[2 * d + 1:2 * d + 2, cs])
            a = jnp.exp(LRU_C * r * jax.nn.log_sigmoid(lam_ref[d:d + 1, cs]))
            outs[d][0][:, cs] = a
            outs[d][1][:, cs] = jnp.sqrt(1.0 - a * a) * (ig * xl)


def _lru_gates(u, conv_w, conv_b, gate_w, gate_b, lam, lc):
    n_batch, nt, _ = u.shape
    t = _tile(math.gcd(lc, nt), 256)
    nk = nt // t
    hb = t // LRU_HALO
    seg_starts = (0, lc // t)
    seg_ends = (lc // t - 1, nk - 1)
    w = LRU_STEP_BLOCKS * LANE
    assert OFF_XR % w == 0 and LRU_WIDTH % w == 0
    cb0 = OFF_XR // w
    kern = functools.partial(_lru_gate_kernel, t=t, seg_starts=seg_starts, seg_ends=seg_ends)
    out = pl.BlockSpec((None, t, w), lambda b, c, k: (b, k, c))
    osd = jax.ShapeDtypeStruct((n_batch, nt, LRU_WIDTH), F32)
    return pl.pallas_call(
        kern,
        grid=(n_batch, LRU_WIDTH // w, nk),
        in_specs=[pl.BlockSpec((None, LRU_HALO, w), lambda b, c, k: (b, jnp.maximum(k * hb - 1, 0), cb0 + c)),
                  pl.BlockSpec((None, t, w), lambda b, c, k: (b, k, cb0 + c)),
                  pl.BlockSpec((None, LRU_HALO, w),
                               lambda b, c, k: (b, jnp.minimum((k + 1) * hb, nt // LRU_HALO - 1), cb0 + c)),
                  pl.BlockSpec((4, w), lambda b, c, k: (0, c)),
                  pl.BlockSpec((1, w), lambda b, c, k: (0, c)),
                  pl.BlockSpec((2, 2, LRU_STEP_BLOCKS, LANE, LANE), lambda b, c, k: (0, 0, c, 0, 0)),
                  pl.BlockSpec((4, w), lambda b, c, k: (0, c)),
                  pl.BlockSpec((2, w), lambda b, c, k: (0, c))],
        out_specs=[out, out, out, out],
        out_shape=[osd, osd, osd, osd],
        compiler_params=_params(("arbitrary",) * 3, 24),
        name="lru_gates",
    )(u, u, u, conv_w, conv_b.reshape(1, LRU_WIDTH), gate_w, gate_b.reshape(4, LRU_WIDTH), lam)


def _lru_scan_kernel(af_ref, bf_ref, ab_ref, bb_ref, hf_ref, hb_ref, cf_ref, cbk_ref, *, t):
    @pl.when(pl.program_id(1) == 0)
    def _():
        cf_ref[...] = jnp.zeros_like(cf_ref)
        cbk_ref[...] = jnp.zeros_like(cbk_ref)

    ng = t // 8

    def body(g, carry):
        hf, hb = carry
        r0 = pl.multiple_of(g * 8, 8)
        a = af_ref[pl.ds(r0, 8), :]
        bt = bf_ref[pl.ds(r0, 8), :]
        rows = []
        for r in range(8):
            hf = a[r:r + 1, :] * hf + bt[r:r + 1, :]
            rows.append(hf)
        hf_ref[pl.ds(r0, 8), :] = jnp.concatenate(rows, axis=0)
        r1 = pl.multiple_of((ng - 1 - g) * 8, 8)
        a = ab_ref[pl.ds(r1, 8), :]
        bt = bb_ref[pl.ds(r1, 8), :]
        rows = []
        for r in range(7, -1, -1):
            hb = a[r:r + 1, :] * hb + bt[r:r + 1, :]
            rows.append(hb)
        hb_ref[pl.ds(r1, 8), :] = jnp.concatenate(rows[::-1], axis=0)
        return hf, hb

    hf, hb = lax.fori_loop(0, ng, body, (cf_ref[...], cbk_ref[...]))
    cf_ref[...] = hf
    cbk_ref[...] = hb


def _lru_scan(af, bf, ab, bb, lc):
    n_batch, nt, w = af.shape
    t = lc
    assert nt % t == 0
    nk = nt // t
    kern = functools.partial(_lru_scan_kernel, t=t)
    fwd = pl.BlockSpec((None, t, w), lambda b, k: (b, k, 0))
    bwd = pl.BlockSpec((None, t, w), lambda b, k: (b, jnp.where(k == 0, 0, nk - k), 0))
    osd = jax.ShapeDtypeStruct((n_batch, nt, w), F32)
    return pl.pallas_call(
        kern,
        grid=(n_batch, nk),
        in_specs=[fwd, fwd, bwd, bwd],
        out_specs=[fwd, bwd],
        out_shape=[osd, osd],
        scratch_shapes=[pltpu.VMEM((1, w), F32), pltpu.VMEM((1, w), F32)],
        compiler_params=_params(("arbitrary", "arbitrary"), 24),
        name="lru_scan",
    )(af, bf, ab, bb)


def _outproj_kernel(oa_ref, od_ref, os_ref, hf_ref, hb_ref, g0_ref, g1_ref, gate_ref, w_ref, x_ref,
                    o_ref, lhs_ref, *, n_batch, lc, tm):
    b = pl.program_id(0)
    i = pl.program_id(1)

    @pl.when(pl.program_id(2) == 0)
    def _():
        def chunk(r0):
            rs = pl.ds(r0, ROW_CHUNK)
            lhs_ref[rs, 0:1024] = oa_ref[rs, :]
            lhs_ref[rs, 1024:2048] = od_ref[rs, :]
            lhs_ref[rs, 2048:3072] = os_ref[rs, :]
            for c, g_ref in enumerate((g0_ref, g1_ref)):
                cs = slice(c * 512, (c + 1) * 512)
                y = (hf_ref[rs, cs] + hb_ref[rs, cs]) * jax.nn.gelu(g_ref[rs, :])
                lhs_ref[rs, 3072 + c * 512:3584 + c * 512] = y.astype(BF16)

        _for_row_chunks(tm, chunk)

    row = i * tm + lax.broadcasted_iota(I32, (tm, 1), 0)
    gate = jnp.where(row < lc, gate_ref[pl.ds(n_batch, 1), :], gate_ref[pl.ds(b, 1), :])
    o_ref[...] = x_ref[...] + gate * _dot(lhs_ref[...], w_ref[...])


def _outproj(o_a, o_d, o_s, hf, hb, u, modl, w_bf, xs, lc):
    n_batch, nt, d = xs.shape
    tm = _tile(nt, 768)
    tn = _tile(d, 512)
    ntile = d // tn
    kern = functools.partial(_outproj_kernel, n_batch=n_batch, lc=lc, tm=tm)
    mix = pl.BlockSpec((None, tm, 1024), lambda b, i, j: (b, i, 0))
    gr = lambda c: pl.BlockSpec((None, tm, 512), lambda b, i, j: (b, i, OFF_GR // 512 + c))
    xblk = pl.BlockSpec((None, tm, tn), lambda b, i, j: (b, i, j))
    return pl.pallas_call(
        kern,
        grid=(n_batch, nt // tm, ntile),
        in_specs=[mix, mix, mix, mix, mix, gr(0), gr(1),
                  pl.BlockSpec((MOD_ROWS, tn), lambda b, i, j: (0, 2 * ntile + j)),
                  pl.BlockSpec((D_MIX, tn), lambda b, i, j: (0, j)), xblk],
        out_specs=xblk,
        out_shape=jax.ShapeDtypeStruct(xs.shape, F32),
        scratch_shapes=[pltpu.VMEM((tm, D_MIX), BF16)],
        input_output_aliases={9: 0},
        compiler_params=_params(("arbitrary",) * 3, 56),
        name="out_proj",
    )(o_a, o_d, o_s, hf, hb, u, u, modl, w_bf, xs)


def _split2(x):
    hi = x.astype(BF16)
    return hi, (x - hi.astype(F32)).astype(BF16)


ROUTER_KC = 512


def _router_kernel(x_ref, g_ref, sh_ref, sc_ref, w0_ref, w1_ref, xm_ref, aff_ref, *, n_batch, lc, tm):
    b = pl.program_id(0)
    i = pl.program_id(1)
    d = x_ref.shape[1]

    def chunk(r0):
        sh = _mod_row(sh_ref, i * tm + r0, b, n_batch, lc)
        sc = _mod_row(sc_ref, i * tm + r0, b, n_batch, lc)
        xm_ref[pl.ds(r0, ROW_CHUNK), 0:d] = _norm_modulate(x_ref[pl.ds(r0, ROW_CHUNK), :], g_ref[...], sh, sc)

    _for_row_chunks(tm, chunk)

    logits = jnp.zeros((tm, LANE), F32)
    tk = _tile(d, ROUTER_KC)
    for kc in range(d // tk):
        ks = slice(kc * tk, (kc + 1) * tk)
        x0, x1 = _split2(xm_ref[:, ks])
        logits = logits + (_dot(x0, w0_ref[ks, :]) + (_dot(x0, w1_ref[ks, :]) + _dot(x1, w0_ref[ks, :])))
    lane = lax.broadcasted_iota(I32, (1, LANE), 1)
    logits = jnp.where(lane < N_EXPERTS, logits, NEG_INF)
    m = jnp.max(logits, axis=-1, keepdims=True)
    p = jnp.exp(logits - m)
    aff = p / jnp.sum(p, axis=-1, keepdims=True)
    aff_ref[...] = aff
    xm_ref[:, d:d + LANE] = aff


def _router(xs, g, modl, wr_pad, lc):
    n_batch, nt, d = xs.shape
    tm = _tile(nt, 256)
    assert lc % ROW_CHUNK == 0
    kern = functools.partial(_router_kernel, n_batch=n_batch, lc=lc, tm=tm)
    wspec = pl.BlockSpec((d, LANE), lambda b, i: (0, 0))
    return pl.pallas_call(
        kern,
        grid=(n_batch, nt // tm),
        in_specs=[pl.BlockSpec((None, tm, d), lambda b, i: (b, i, 0)),
                  pl.BlockSpec((1, d), lambda b, i: (0, 0)),
                  pl.BlockSpec((MOD_ROWS, d), lambda b, i: (0, 3)),
                  pl.BlockSpec((MOD_ROWS, d), lambda b, i: (0, 4)),
                  wspec, wspec],
        out_specs=[pl.BlockSpec((None, tm, d + LANE), lambda b, i: (b, i, 0)),
                   pl.BlockSpec((None, tm, LANE), lambda b, i: (b, i, 0))],
        out_shape=[jax.ShapeDtypeStruct((n_batch, nt, d + LANE), F32),
                   jax.ShapeDtypeStruct((n_batch, nt, LANE), F32)],
        compiler_params=_params(("arbitrary", "arbitrary"), 48),
        name="norm2_router",
    )(xs, g, modl, modl, *_split2(wr_pad))


def _select_kernel(aff_ref, l_ref, cnt_ref, off_ref, bits_ref, *, row0, n, cap):
    nc = n // LANE
    bits_ref[...] = lax.bitcast_convert_type(aff_ref[row0:row0 + n, :], I32)

    def bit_step(i, thr):
        cand = thr | (jnp.int32(1) << (30 - i))
        cnt = jnp.sum((bits_ref[...] >= cand).astype(I32), axis=0, keepdims=True)
        return jnp.where(cnt >= cap, cand, thr)

    thr = lax.fori_loop(0, 31, bit_step, jnp.zeros((1, LANE), I32))
    n_gt = jnp.sum((bits_ref[...] > thr).astype(I32), axis=0, keepdims=True)
    need = (cap - n_gt).astype(F32)

    r_io = lax.broadcasted_iota(I32, (LANE, LANE), 0)
    c_io = lax.broadcasted_iota(I32, (LANE, LANE), 1)
    ltri = (c_io <= r_io).astype(BF16)
    t_col = r_io.astype(F32)
    j_row = c_io.astype(F32)

    def chunk(c, carry):
        run_eq, run_sel = carry
        r0 = pl.multiple_of(c * LANE, LANE)
        bc = bits_ref[pl.ds(r0, LANE), :]
        eq = bc == thr
        eq_f = eq.astype(F32)
        incl_eq = _dot(ltri, eq_f.astype(BF16))
        rank = run_eq + incl_eq - eq_f
        sel = (bc > thr) | (eq & (rank < need))
        sel_f = sel.astype(F32)
        incl = _dot(ltri, sel_f.astype(BF16))
        cnt_c = incl[LANE - 1:LANE, :]
        qm = jnp.where(sel, incl - sel_f, -1.0)
        for e in range(N_EXPERTS):
            pos = jnp.broadcast_to(qm[:, e:e + 1], (LANE, LANE))
            local = jnp.sum(jnp.where(pos == j_row, t_col, 0.0), axis=0, keepdims=True)
            l_ref[e, pl.ds(c, 1), :] = local.astype(I32)
        cnt_ref[pl.ds(c, 1), :] = cnt_c.astype(I32)
        off_ref[pl.ds(c, 1), :] = run_sel.astype(I32)
        return run_eq + incl_eq[LANE - 1:LANE, :], run_sel + cnt_c

    zero = jnp.zeros((1, LANE), F32)
    lax.fori_loop(0, nc, chunk, (zero, zero))


def _select(affp, row0, n, cap):
    n_batch, nt, _ = affp.shape
    nc = n // LANE
    kern = functools.partial(_select_kernel, row0=row0, n=n, cap=cap)
    return pl.pallas_call(
        kern,
        grid=(n_batch,),
        in_specs=[pl.BlockSpec((None, nt, LANE), lambda b: (b, 0, 0))],
        out_specs=[pl.BlockSpec((None, N_EXPERTS, nc, LANE), lambda b: (b, 0, 0, 0)),
                   pl.BlockSpec((None, nc, LANE), lambda b: (b, 0, 0)),
                   pl.BlockSpec((None, nc, LANE), lambda b: (b, 0, 0))],
        out_shape=[jax.ShapeDtypeStruct((n_batch, N_EXPERTS, nc, LANE), I32),
                   jax.ShapeDtypeStruct((n_batch, nc, LANE), I32),
                   jax.ShapeDtypeStruct((n_batch, nc, LANE), I32)],
        scratch_shapes=[pltpu.VMEM((n, LANE), I32)],
        compiler_params=_params(("arbitrary",), 40),
        name="ec_select",
    )(affp)


MOE_ROWS = 256


def _moe_kernel(l_hbm, cnt_hbm, off_hbm, xm_hbm, xs_in_hbm, g2_ref, w1_ref, w3_ref, w2_ref, xs_hbm,
                l_s, cnt_s, off_s, idx_s, xg, rows, xb, hdn_s, sem, *, row0, nc, cap, g_row):
    del xs_in_hbm
    b = pl.program_id(0)
    e = pl.program_id(1)
    step = b * pl.num_programs(1) + e
    last_step = pl.num_programs(0) * pl.num_programs(1) - 1
    tables = (pltpu.make_async_copy(l_hbm.at[b, e], l_s, sem.at[3]),
              pltpu.make_async_copy(cnt_hbm.at[b], cnt_s, sem.at[4]),
              pltpu.make_async_copy(off_hbm.at[b], off_s, sem.at[5]))
    for cp in tables:
        cp.start()
    for cp in tables:
        cp.wait()

    def x_gather(tok, slot, n_rows=1):
        return pltpu.make_async_copy(xm_hbm.at[b, pl.ds(tok, n_rows), :], xg.at[pl.ds(slot, n_rows), :], sem.at[0])

    def row_gather(tok, slot, n_rows=1):
        return pltpu.make_async_copy(xs_hbm.at[b, pl.ds(tok, n_rows), :], rows.at[pl.ds(slot, n_rows), :], sem.at[1])

    def row_scatter(tok, slot, n_rows=1):
        return pltpu.make_async_copy(rows.at[pl.ds(slot, n_rows), :], xs_hbm.at[b, pl.ds(tok, n_rows), :], sem.at[2])

    def chunk(c, _):
        base = row0 + c * LANE
        off_c = off_s[c, e]

        def one(j, _):
            idx_s[off_c + j] = base + l_s[c, j]
            return 0

        lax.fori_loop(0, cnt_s[c, e], one, 0)
        return 0

    lax.fori_loop(0, nc, chunk, 0)

    def start_all(copy):
        def one(s, _):
            copy(idx_s[s], s).start()
            return 0

        lax.fori_loop(0, cap, one, 0, unroll=8)

    start_all(x_gather)

    @pl.when(step > 0)
    def _():
        row_scatter(row0, 0, cap).wait()

    start_all(row_gather)
    x_gather(row0, 0, cap).wait()

    d = rows.shape[1]
    tn = _tile(d, 512)
    rc = min(cap, MOE_ROWS)
    lane = lax.broadcasted_iota(I32, (1, LANE), 1)
    g2_row = g_row if g_row is not None else b

    def hidden(r, _):
        rs = pl.ds(pl.multiple_of(r * rc, rc), rc)
        for kc in range(d // tn):
            ks = slice(kc * tn, (kc + 1) * tn)
            xb[:, ks] = xg[rs, ks].astype(BF16)
        h1 = _dot(xb[...], w1_ref[...])
        h3 = _dot(xb[...], w3_ref[...])
        gate = jnp.sum(jnp.where(lane == e, xg[rs, d:d + LANE], 0.0), axis=-1, keepdims=True)
        hdn_s[rs, :] = (h1 * jax.nn.sigmoid(h1) * h3 * gate).astype(BF16)
        return 0

    lax.fori_loop(0, cap // rc, hidden, 0)
    row_gather(row0, 0, cap).wait()

    def down(r, _):
        rs = pl.ds(pl.multiple_of(r * rc, rc), rc)
        for c in range(d // tn):
            sl = slice(c * tn, (c + 1) * tn)
            rows[rs, sl] = rows[rs, sl] + g2_ref[pl.ds(g2_row, 1), sl] * _dot(hdn_s[rs, :], w2_ref[:, sl])
        return 0

    lax.fori_loop(0, cap // rc, down, 0)
    start_all(row_scatter)

    @pl.when(step == last_step)
    def _():
        row_scatter(row0, 0, cap).wait()


def _moe(ltab, cnt, off, xma, xs, modl, w1, w3, w2, row0, cap, g_row):
    n_batch, nt, d = xs.shape
    nc = ltab.shape[2]
    nd = 1
    rc = min(cap, MOE_ROWS)
    assert cap % rc == 0
    kern = functools.partial(_moe_kernel, row0=row0, nc=nc, cap=cap, g_row=g_row)
    any_ = pl.BlockSpec(memory_space=pl.ANY)
    return pl.pallas_call(
        kern,
        grid=(n_batch, N_EXPERTS),
        in_specs=[any_, any_, any_, any_, any_,
                  pl.BlockSpec((MOD_ROWS, d), lambda b, e: (0, 5 * nd)),
                  pl.BlockSpec((None, d, EXPERT_FF), lambda b, e: (e, 0, 0)),
                  pl.BlockSpec((None, d, EXPERT_FF), lambda b, e: (e, 0, 0)),
                  pl.BlockSpec((None, EXPERT_FF, d), lambda b, e: (e, 0, 0))],
        out_specs=any_,
        out_shape=jax.ShapeDtypeStruct(xs.shape, F32),
        scratch_shapes=[pltpu.SMEM((nc, LANE), I32), pltpu.SMEM((nc, LANE), I32), pltpu.SMEM((nc, LANE), I32),
                        pltpu.SMEM((cap,), I32), pltpu.VMEM((cap, d + LANE), F32), pltpu.VMEM((cap, d), F32),
                        pltpu.VMEM((rc, d), BF16), pltpu.VMEM((cap, EXPERT_FF), BF16),
                        pltpu.SemaphoreType.DMA((6,))],
        input_output_aliases={4: 0},
        compiler_params=_params(("arbitrary", "arbitrary"), 56),
        name="moe_ffn",
    )(ltab, cnt, off, xma, xs, modl, w1, w3, w2)


def _rope_tables(n, lc):
    t = jnp.arange(n)
    pos = jnp.stack([t // GRID_W, t % GRID_W], axis=-1).astype(F32)

    def tables(dh):
        quarter = dh // 4
        inv_freq = ROPE_THETA ** (-jnp.arange(quarter, dtype=F32) / quarter)
        ang = pos[:, :, None] * inv_freq
        cos, sin = jnp.cos(ang), jnp.sin(ang)
        c = jnp.concatenate([cos[:, 0], cos[:, 0], cos[:, 1], cos[:, 1]], axis=-1)
        s = jnp.concatenate([-sin[:, 0], sin[:, 0], -sin[:, 1], sin[:, 1]], axis=-1)
        reps = LANE // dh
        c, s = jnp.tile(c, (1, reps)), jnp.tile(s, (1, reps))
        return (jnp.concatenate([jnp.ones((lc, LANE), F32), c], axis=0),
                jnp.concatenate([jnp.zeros((lc, LANE), F32), s], axis=0))

    return tables(HEAD_DIM) + tables(DIFF_QK_DIM)


def kernel(x, c, ctx, c_ctx, w_ada, b_ada, norm_g, w_in, w_out, qk_g_na, na_rpb, qk_g_diff, diff_lambda,
           diff_subln_g, qk_g_swa, swa_sink, lru_conv_w, lru_conv_b, lru_gate_w, lru_gate_b, lru_lam,
           w_router, w1, w3, w2):
    n_batch, n, d = x.shape
    lc = ctx.shape[1]
    depth = w_ada.shape[0]
    assert n_batch < MOD_ROWS and w_in.shape[2] == D_IN

    cvec = jnp.zeros((MOD_ROWS, d), F32).at[:n_batch].set(c).at[n_batch].set(c_ctx)
    mod = _ada(cvec, w_ada, b_ada)
    cs, ss, cd, sd = _rope_tables(n, lc)
    xs = jnp.concatenate([ctx, x], axis=1)

    cap_lat = CAPACITY_FACTOR * n // N_EXPERTS
    cap_ctx = CAPACITY_FACTOR * lc // N_EXPERTS

    for l in range(depth):
        modl = mod[l]
        u = _inproj(xs, norm_g[l, 0].reshape(1, d), modl, w_in[l].astype(BF16), lc)

        pvec = jnp.concatenate([qk_g_na[l], jnp.tile(qk_g_diff[l], (1, 2)), qk_g_swa[l],
                                diff_subln_g[l].reshape(1, LANE), jnp.zeros((1, LANE), F32)], axis=0)
        qa, ka, va, qd, kd, vd, qs, ks, vs = _prep(u, pvec, cs, ss, cd, sd)

        gmax = lambda g: jnp.max(jnp.abs(g[0])) * jnp.max(jnp.abs(g[1]))
        b_na = (BOUND_SLACK * HEAD_DIM * ATTN_Q_SCALE * gmax(qk_g_na[l])
                + LOG2E * jnp.max(jnp.abs(na_rpb[l])))
        b_swa = jnp.maximum(BOUND_SLACK * HEAD_DIM * ATTN_Q_SCALE * gmax(qk_g_swa[l]),
                            LOG2E * jnp.max(jnp.abs(swa_sink[l])))
        b_diff = BOUND_SLACK * DIFF_QK_DIM * DIFF_Q_SCALE * gmax(qk_g_diff[l])

        o_a = _na(b_na, qa, ka, va, _na_bias_tables(na_rpb[l], n), lc)

        lambda_init = 0.8 - 0.6 * math.exp(-0.3 * l)
        dl = diff_lambda[l].astype(F32)
        lam = jnp.exp(jnp.sum(dl[0] * dl[1])) - jnp.exp(jnp.sum(dl[2] * dl[3])) + lambda_init
        o_d = _diff(jnp.stack([lam, b_diff]).astype(F32), qd, kd, vd, pvec, lc, 1.0 - lambda_init)

        o_s = _swa(b_swa, qs, ks, vs, swa_sink[l], lc)

        af, bf, ab, bb = _lru_gates(u, lru_conv_w[l], lru_conv_b[l], lru_gate_w[l], lru_gate_b[l], lru_lam[l], lc)
        hf, hb = _lru_scan(af, bf, ab, bb, lc)

        xs = _outproj(o_a, o_d, o_s, hf, hb, u, modl, w_out[l].astype(BF16), xs, lc)

        wr_pad = jnp.zeros((d, LANE), F32).at[:, :N_EXPERTS].set(w_router[l])
        xma, affp = _router(xs, norm_g[l, 1].reshape(1, d), modl, wr_pad, lc)
        w1b, w3b, w2b = w1[l].astype(BF16), w3[l].astype(BF16), w2[l].astype(BF16)
        lt, cnt, off = _select(affp, lc, n, cap_lat)
        xs = _moe(lt, cnt, off, xma, xs, modl, w1b, w3b, w2b, lc, cap_lat, None)
        if l < depth - 1:
            lt, cnt, off = _select(affp, 0, lc, cap_ctx)
            xs = _moe(lt, cnt, off, xma, xs, modl, w1b, w3b, w2b, 0, cap_ctx, n_batch)
    return xs[:, lc:, :]
```

```python
import functools
import math

import numpy as np
import jax
import jax.numpy as jnp
from jax import lax
from jax.experimental import pallas as pl
from jax.experimental.pallas import tpu as pltpu

GRID_W = 64
HEAD_DIM = 128
NA_HEADS = 8
NA_KH = 8
NA_KW = 16
DIFF_HEADS = 8
DIFF_QK_DIM = 64
SWA_Q_HEADS = 8
SWA_KV_HEADS = 2
SWA_GROUP = SWA_Q_HEADS // SWA_KV_HEADS
SWA_WINDOW = 128
LRU_WIDTH = 1024
LRU_BLOCKS = 8
LRU_C = 8.0
N_EXPERTS = 16
EXPERT_FF = 256
CAPACITY_FACTOR = 2
ROPE_THETA = 10000.0
NEG_INF = -1e30
EPS = 1e-6

OFF_QA, OFF_KA, OFF_VA = 0, 1024, 2048
OFF_QD, OFF_KD, OFF_VD = 3072, 4096, 5120
OFF_QS, OFF_KS, OFF_VS = 6144, 7168, 7424
OFF_XR, OFF_GR = 7680, 8704
D_ATT = 7680
D_IN = 9728
D_MIX = 4096

LOG2E = 1.4426950408889634
DIFF_Q_SCALE = DIFF_QK_DIM ** -0.5 * LOG2E
ATTN_FAST_BOUND = 60.0
ATTN_Q_SCALE = HEAD_DIM ** -0.5 * LOG2E
BOUND_SLACK = 1.02

LANE = 128
MOD_ROWS = 8
V7X_VMEM_BYTES = 64 * 1024 * 1024

F32 = jnp.float32
BF16 = jnp.bfloat16
I32 = jnp.int32


def _tile(n, target, mult=LANE):
    best = None
    for t in range(mult, min(n, target) + 1, mult):
        if n % t == 0:
            best = t
    assert best is not None, (n, target, mult)
    return best


def _params(sem, vmem_mb):
    return pltpu.CompilerParams(dimension_semantics=sem, vmem_limit_bytes=min(vmem_mb, 60) * 1024 * 1024)


def _dot_nt(a, b):
    return lax.dot_general(a, b, (((1,), (1,)), ((), ())), preferred_element_type=F32)


def _dot(a, b):
    return jnp.dot(a, b, preferred_element_type=F32)


def _ada_kernel(c_ref, w_ref, b_ref, o_ref):
    cv = c_ref[...]
    s = cv * jax.nn.sigmoid(cv)
    s_hi = s.astype(BF16)
    s_lo = (s - s_hi.astype(F32)).astype(BF16)
    w = w_ref[...].astype(BF16)
    o_ref[...] = _dot(s_hi, w) + _dot(s_lo, w) + b_ref[...]


def _ada(cvec, w_ada, b_ada):
    depth, d, n6 = w_ada.shape
    tn = _tile(n6, 1024)
    return pl.pallas_call(
        _ada_kernel,
        grid=(depth, n6 // tn),
        in_specs=[pl.BlockSpec((MOD_ROWS, d), lambda l, j: (0, 0)),
                  pl.BlockSpec((None, d, tn), lambda l, j: (l, 0, j)),
                  pl.BlockSpec((None, 1, tn), lambda l, j: (l, 0, j))],
        out_specs=pl.BlockSpec((None, MOD_ROWS, tn), lambda l, j: (l, 0, j)),
        out_shape=jax.ShapeDtypeStruct((depth, MOD_ROWS, n6), F32),
        compiler_params=_params(("arbitrary", "arbitrary"), 56),
        name="ada_mod",
    )(cvec, w_ada, b_ada.reshape(depth, 1, n6))


ROW_CHUNK = 32


def _mod_row(ref, row0, b, n_batch, lc):
    return jnp.where(row0 < lc, ref[pl.ds(n_batch, 1), :], ref[pl.ds(b, 1), :])


def _norm_modulate(x, g, sh, sc):
    ms = jnp.mean(x * x, axis=-1, keepdims=True)
    return x * lax.rsqrt(ms + EPS) * g * (1.0 + sc) + sh


def _for_row_chunks(tm, fn):
    assert tm % ROW_CHUNK == 0

    def body(c, _):
        fn(pl.multiple_of(c * ROW_CHUNK, ROW_CHUNK))
        return 0

    lax.fori_loop(0, tm // ROW_CHUNK, body, 0)


def _inproj_kernel(x_ref, g_ref, sh_ref, sc_ref, w_ref, o_ref, xn_ref, *, n_batch, lc, tm):
    b = pl.program_id(0)
    i = pl.program_id(1)

    @pl.when(pl.program_id(2) == 0)
    def _():
        def chunk(r0):
            sh = _mod_row(sh_ref, i * tm + r0, b, n_batch, lc)
            sc = _mod_row(sc_ref, i * tm + r0, b, n_batch, lc)
            xn = _norm_modulate(x_ref[pl.ds(r0, ROW_CHUNK), :], g_ref[...], sh, sc)
            xn_ref[pl.ds(r0, ROW_CHUNK), :] = xn.astype(BF16)

        _for_row_chunks(tm, chunk)

    o_ref[...] = _dot(xn_ref[...], w_ref[...])


def _inproj(xs, g, modl, w_bf, lc):
    n_batch, nt, d = xs.shape
    d_in = w_bf.shape[1]
    tm = _tile(nt, 768)
    tn = _tile(d_in, 512)
    assert lc % ROW_CHUNK == 0
    kern = functools.partial(_inproj_kernel, n_batch=n_batch, lc=lc, tm=tm)
    return pl.pallas_call(
        kern,
        grid=(n_batch, nt // tm, d_in // tn),
        in_specs=[pl.BlockSpec((None, tm, d), lambda b, i, j: (b, i, 0)),
                  pl.BlockSpec((1, d), lambda b, i, j: (0, 0)),
                  pl.BlockSpec((MOD_ROWS, d), lambda b, i, j: (0, 0)),
                  pl.BlockSpec((MOD_ROWS, d), lambda b, i, j: (0, 1)),
                  pl.BlockSpec((d, tn), lambda b, i, j: (0, j))],
        out_specs=pl.BlockSpec((None, tm, tn), lambda b, i, j: (b, i, j)),
        out_shape=jax.ShapeDtypeStruct((n_batch, nt, d_in), F32),
        scratch_shapes=[pltpu.VMEM((tm, d), BF16)],
        compiler_params=_params(("arbitrary", "arbitrary", "arbitrary"), 56),
        name="in_proj",
    )(xs, g, modl, modl, w_bf)


def _rms_full(x, g):
    ms = jnp.mean(x * x, axis=-1, keepdims=True)
    return x * lax.rsqrt(ms + EPS) * g


def _group_ones(width):
    r = lax.broadcasted_iota(I32, (LANE, LANE), 0) // width
    c = lax.broadcasted_iota(I32, (LANE, LANE), 1) // width
    return jnp.where(r == c, 1.0, 0.0).astype(BF16)


def _rms_groups(x, g, ones, width):
    x2 = x * x
    hi = x2.astype(BF16)
    lo = (x2 - hi.astype(F32)).astype(BF16)
    ms = (_dot(hi, ones) + _dot(lo, ones)) * (1.0 / width)
    return x * lax.rsqrt(ms + EPS) * g


def _rope(x, cos, sin_signed, half):
    lane = lax.broadcasted_iota(I32, (1, LANE), 1)
    up = pltpu.roll(x, LANE - half, axis=1)
    dn = pltpu.roll(x, half, axis=1)
    partner = jnp.where((lane & half) == 0, up, dn)
    return x * cos + partner * sin_signed


def _prep_kernel(u_ref, p_ref, cs_ref, ss_ref, cd_ref, sd_ref,
                 qa_ref, ka_ref, va_ref, qd_ref, kd_ref, vd_ref, qs_ref, ks_ref, vs_ref):
    g_na_q, g_na_k = p_ref[0:1, :], p_ref[1:2, :]
    g_d_q, g_d_k = p_ref[2:3, :], p_ref[3:4, :]
    g_s_q, g_s_k = p_ref[4:5, :], p_ref[5:6, :]
    cs, ss, cd, sd = cs_ref[...], ss_ref[...], cd_ref[...], sd_ref[...]
    lo = lax.broadcasted_iota(I32, (1, LANE), 1) < DIFF_QK_DIM

    def col(off, h):
        return u_ref[:, off + h * LANE: off + (h + 1) * LANE]

    def put(ref, h, val):
        ref[:, h * LANE:(h + 1) * LANE] = val.astype(BF16)

    ones_head = _group_ones(HEAD_DIM)
    ones_half = _group_ones(DIFF_QK_DIM)
    rms_head = lambda x, g: _rms_groups(x, g, ones_head, HEAD_DIM)
    rms_half = lambda x, g: _rms_groups(x, g, ones_half, DIFF_QK_DIM)

    for h in range(NA_HEADS):
        put(qa_ref, h, rms_head(col(OFF_QA, h), g_na_q) * ATTN_Q_SCALE)
        put(ka_ref, h, rms_head(col(OFF_KA, h), g_na_k))
        put(va_ref, h, col(OFF_VA, h))
    ones_col = jnp.where(lax.broadcasted_iota(I32, (u_ref.shape[0], LANE), 1) == 0, 1.0, 0.0).astype(BF16)
    for h in range(DIFF_HEADS):
        q = _rope(rms_half(col(OFF_QD, h), g_d_q), cd, sd, DIFF_QK_DIM // 4) * DIFF_Q_SCALE
        qd_ref[0, :, h * LANE:(h + 1) * LANE] = jnp.where(lo, q, 0.0).astype(BF16)
        qd_ref[1, :, h * LANE:(h + 1) * LANE] = jnp.where(lo, 0.0, q).astype(BF16)
        put(kd_ref, h, _rope(rms_half(col(OFF_KD, h), g_d_k), cd, sd, DIFF_QK_DIM // 4))
        put(vd_ref, 2 * h, col(OFF_VD, h))
        vd_ref[:, (2 * h + 1) * LANE:(2 * h + 2) * LANE] = ones_col
    for h in range(SWA_Q_HEADS):
        put(qs_ref, h, _rope(rms_head(col(OFF_QS, h), g_s_q), cs, ss, HEAD_DIM // 4) * ATTN_Q_SCALE)
    for h in range(SWA_KV_HEADS):
        put(ks_ref, h, _rope(rms_head(col(OFF_KS, h), g_s_k), cs, ss, HEAD_DIM // 4))
        put(vs_ref, h, col(OFF_VS, h))


def _prep(u, pvec, cs, ss, cd, sd):
    n_batch, nt, _ = u.shape
    tr = _tile(nt, 256)
    row = lambda w: pl.BlockSpec((None, tr, w), lambda b, i: (b, i, 0))
    tab = pl.BlockSpec((tr, LANE), lambda b, i: (i, 0))
    sd_ = lambda w: jax.ShapeDtypeStruct((n_batch, nt, w), BF16)
    return pl.pallas_call(
        _prep_kernel,
        grid=(n_batch, nt // tr),
        in_specs=[row(D_ATT), pl.BlockSpec((8, LANE), lambda b, i: (0, 0)), tab, tab, tab, tab],
        out_specs=[row(1024), row(1024), row(1024),
                   pl.BlockSpec((None, 2, tr, 1024), lambda b, i: (b, 0, i, 0)),
                   row(1024), row(2048), row(1024), row(256), row(256)],
        out_shape=[sd_(1024), sd_(1024), sd_(1024),
                   jax.ShapeDtypeStruct((n_batch, 2, nt, 1024), BF16),
                   sd_(1024), sd_(2048), sd_(1024), sd_(256), sd_(256)],
        compiler_params=_params(("arbitrary", "arbitrary"), 48),
        name="qkv_prep",
    )(u, pvec, cs, ss, cd, sd)


NA_QBLK = 2 * GRID_W
NA_KROWS = 10
NA_KBLK = NA_KROWS * GRID_W


def _na_kernel(sc_ref, q_ref, k_ref, v_ref, *rest, lc, nlb, group, online):
    bias_refs, o_ref = rest[:group], rest[group]
    i0 = pl.program_id(2) * group
    kc = k_ref[0:lc, :]
    vc = v_ref[0:lc, :]
    for g in range(group):
        li = i0 + g - lc // NA_QBLK
        kb = jnp.clip(li - 2, 0, nlb - NA_KROWS // 2)
        start = pl.multiple_of(lc + kb * NA_QBLK, NA_QBLK)
        rows = slice(g * NA_QBLK, (g + 1) * NA_QBLK)
        q = q_ref[rows, :]
        vw = v_ref[pl.ds(start, NA_KBLK), :]
        s_n = _dot_nt(q, k_ref[pl.ds(start, NA_KBLK), :]) + bias_refs[g][...]
        s_c = _dot_nt(q, kc)
        if online:
            m = jnp.maximum(jnp.max(s_n, axis=-1, keepdims=True), jnp.max(s_c, axis=-1, keepdims=True))
        else:
            m = sc_ref[0]
        p_n = jnp.exp2(s_n - m)
        p_c = jnp.exp2(s_c - m)
        l = jnp.sum(p_n, axis=-1, keepdims=True) + jnp.sum(p_c, axis=-1, keepdims=True)
        o = _dot(p_n.astype(BF16), vw) + _dot(p_c.astype(BF16), vc)
        o_ref[rows, :] = (o / l).astype(BF16)


def _na_bias_tables(rpb, n):
    rows = n // GRID_W
    nlb = rows // 2
    reps = [0, 1, 2, nlb - 2, nlb - 1]
    h = rpb.shape[0]
    cols = np.arange(GRID_W)
    cstart = np.clip(cols - NA_KW // 2, 0, GRID_W - NA_KW)
    col_ok = (cols[None, :] >= cstart[:, None]) & (cols[None, :] < cstart[:, None] + NA_KW)
    dx = cols[None, :] - cols[:, None] + (NA_KW - 1)
    sel = (dx[None] == np.arange(2 * NA_KW - 1)[:, None, None]) & col_ok[None]
    tx = jnp.einsum('hyx,xqk->hyqk', rpb.astype(F32), jnp.asarray(sel, F32), precision=lax.Precision.HIGHEST)
    dy_i = np.zeros((5, 2, NA_KROWS), np.int32)
    row_ok = np.zeros((5, 2, NA_KROWS), bool)
    for c, li in enumerate(reps):
        kb = int(np.clip(li - 2, 0, nlb - NA_KROWS // 2))
        qr = 2 * li + np.arange(2)
        kr = 2 * kb + np.arange(NA_KROWS)
        rs = np.clip(qr - NA_KH // 2, 0, rows - NA_KH)
        row_ok[c] = (kr[None, :] >= rs[:, None]) & (kr[None, :] < rs[:, None] + NA_KH)
        dy_i[c] = np.clip(kr[None, :] - qr[:, None] + (NA_KH - 1), 0, 2 * NA_KH - 2)
    blocks = jnp.take(tx, jnp.asarray(dy_i.reshape(-1)), axis=1)
    blocks = blocks.reshape(h, 5, 2, NA_KROWS, GRID_W, GRID_W)
    ok = row_ok[:, :, :, None, None] & col_ok[None, None, None]
    tbl = jnp.where(ok[None], blocks * LOG2E, NEG_INF)
    tbl = tbl.transpose(1, 0, 2, 4, 3, 5).reshape(5, h, NA_QBLK, NA_KBLK)
    dead = jnp.full((1,) + tbl.shape[1:], NEG_INF, F32)
    return jnp.concatenate([tbl, dead], axis=0)


def _group(nblocks, target):
    return max(g for g in range(1, target + 1) if nblocks % g == 0)


def _softmax_dispatch(bound, call):
    sc = jnp.reshape(bound, (1,)).astype(F32)
    return lax.cond(bound <= ATTN_FAST_BOUND, lambda s: call(False)(s), lambda s: call(True)(s), sc)


def _na(bound, qa, ka, va, bias, lc):
    n_batch, nt, _ = qa.shape
    n = nt - lc
    nlb = n // NA_QBLK
    assert nlb >= 5 and lc % NA_QBLK == 0
    ncb = lc // NA_QBLK
    group = _group(nt // NA_QBLK, 11)

    def case(i):
        li = i - ncb
        return jnp.where(li < 0, 5, jnp.where(li < 2, li, jnp.where(li >= nlb - 2, li - (nlb - 5), 2)))

    kv = pl.BlockSpec((None, nt, LANE), lambda b, h, i: (b, 0, h))
    qo = pl.BlockSpec((None, group * NA_QBLK, LANE), lambda b, h, i: (b, i, h))
    bias_specs = [pl.BlockSpec((None, None, NA_QBLK, NA_KBLK),
                               functools.partial(lambda b, h, i, g: (case(i * group + g), h, 0, 0), g=g))
                  for g in range(group)]

    def call(online):
        kern = functools.partial(_na_kernel, lc=lc, nlb=nlb, group=group, online=online)
        return lambda sc: pl.pallas_call(
            kern,
            grid=(n_batch, NA_HEADS, nt // (group * NA_QBLK)),
            in_specs=[pl.BlockSpec(memory_space=pltpu.SMEM), qo, kv, kv] + bias_specs,
            out_specs=qo,
            out_shape=jax.ShapeDtypeStruct((n_batch, nt, NA_HEADS * HEAD_DIM), BF16),
            compiler_params=_params(("arbitrary",) * 3, 40),
            name="na_attn_online" if online else "na_attn_offset",
        )(sc, qa, ka, va, *([bias] * group))

    return _softmax_dispatch(bound, call)


SWA_BLK = 128
SWA_WIN = 3 * SWA_BLK


def _swa_kernel(sc_ref, q_ref, k_ref, v_ref, sink_ref, *rest, lc, n, group, online):
    mask_refs, o_ref = rest[:group], rest[group]
    i0 = pl.program_id(2) * group
    kc = k_ref[0:lc, :]
    vc = v_ref[0:lc, :]
    for gq in range(group):
        li = i0 + gq - lc // SWA_BLK
        ws = jnp.clip((li - 1) * SWA_BLK, 0, n - SWA_WIN)
        start = pl.multiple_of(lc + ws, SWA_BLK)
        kw = k_ref[pl.ds(start, SWA_WIN), :]
        vw = v_ref[pl.ds(start, SWA_WIN), :]
        rows = slice(gq * SWA_BLK, (gq + 1) * SWA_BLK)
        mask = mask_refs[gq][...]
        for g in range(SWA_GROUP):
            q = q_ref[rows, g * LANE:(g + 1) * LANE]
            s_w = _dot_nt(q, kw)
            s_c = _dot_nt(q, kc)
            sink = sink_ref[g * SWA_BLK:(g + 1) * SWA_BLK, 0:1]
            if online:
                s_w = jnp.where(mask > 0.0, s_w, NEG_INF)
                m = jnp.maximum(jnp.maximum(jnp.max(s_w, axis=-1, keepdims=True),
                                            jnp.max(s_c, axis=-1, keepdims=True)), sink)
                p_w = jnp.exp2(s_w - m)
            else:
                m = sc_ref[0]
                p_w = jnp.exp2(s_w - m) * mask
            p_c = jnp.exp2(s_c - m)
            l = jnp.sum(p_w, axis=-1, keepdims=True) + jnp.sum(p_c, axis=-1, keepdims=True) + jnp.exp2(sink - m)
            o = (_dot(p_w.astype(BF16), vw) + _dot(p_c.astype(BF16), vc)) / l
            o_ref[rows, g * LANE:(g + 1) * LANE] = o.astype(BF16)


def _swa_masks():
    r = np.arange(SWA_BLK)[:, None]
    j = np.arange(SWA_WIN)[None, :]
    rel = [j - r, j - r - SWA_BLK, j - r - 2 * SWA_BLK]
    tabs = [(np.abs(d) <= SWA_WINDOW) for d in rel] + [np.zeros((SWA_BLK, SWA_WIN), bool)]
    return jnp.asarray(np.stack(tabs), F32)


def _swa(bound, qs, ks, vs, sink, lc):
    n_batch, nt, _ = qs.shape
    n = nt - lc
    nb = n // SWA_BLK
    assert nb >= 3 and lc % SWA_BLK == 0
    ncb = lc // SWA_BLK
    rows = SWA_GROUP * SWA_BLK
    group = _group(nt // SWA_BLK, 6)
    sink_rows = jnp.broadcast_to(
        jnp.repeat((sink.astype(F32) * LOG2E).reshape(SWA_KV_HEADS, SWA_GROUP), SWA_BLK, axis=1)[:, :, None],
        (SWA_KV_HEADS, rows, LANE))
    masks = _swa_masks()

    def case(i):
        li = i - ncb
        return jnp.where(li < 0, 3, jnp.where(li == 0, 0, jnp.where(li == nb - 1, 2, 1)))

    kv = pl.BlockSpec((None, nt, LANE), lambda b, k, i: (b, 0, k))
    qo = pl.BlockSpec((None, group * SWA_BLK, SWA_GROUP * LANE), lambda b, k, i: (b, i, k))
    mask_specs = [pl.BlockSpec((None, SWA_BLK, SWA_WIN),
                               functools.partial(lambda b, k, i, g: (case(i * group + g), 0, 0), g=g))
                  for g in range(group)]

    def call(online):
        kern = functools.partial(_swa_kernel, lc=lc, n=n, group=group, online=online)
        return lambda sc: pl.pallas_call(
            kern,
            grid=(n_batch, SWA_KV_HEADS, nt // (group * SWA_BLK)),
            in_specs=[pl.BlockSpec(memory_space=pltpu.SMEM), qo, kv, kv,
                      pl.BlockSpec((None, rows, LANE), lambda b, k, i: (k, 0, 0))] + mask_specs,
            out_specs=qo,
            out_shape=jax.ShapeDtypeStruct((n_batch, nt, SWA_Q_HEADS * HEAD_DIM), BF16),
            compiler_params=_params(("arbitrary",) * 3, 32),
            name="swa_attn_online" if online else "swa_attn_offset",
        )(sc, qs, ks, vs, sink_rows, *([masks] * group))

    return _softmax_dispatch(bound, call)


def _diff_finish(sc_ref, g_ref, o_ref, num, den, tq, post_scale):
    o2 = num / den
    o = o2[0:tq, :] - sc_ref[0] * o2[tq:2 * tq, :]
    o_ref[...] = (_rms_full(o, g_ref[6:7, :]) * post_scale).astype(BF16)


def _diff_online_kernel(sc_ref, q_ref, k_ref, v_ref, g_ref, o_ref, *, lc, nt, tq, tk, post_scale):
    qi = pl.program_id(2)
    q = q_ref[...].reshape(2 * tq, LANE)
    nk = jnp.where(qi < lc // tq, lc // tk, nt // tk)

    def body(kt, carry):
        m, l, acc = carry
        k0 = pl.multiple_of(kt * tk, tk)
        k = k_ref[pl.ds(k0, tk), :]
        v = v_ref[pl.ds(k0, tk), :]
        s = _dot_nt(q, k)
        m_new = jnp.maximum(m, jnp.max(s, axis=-1, keepdims=True))
        alpha = jnp.exp2(m - m_new)
        p = jnp.exp2(s - m_new)
        l = alpha * l + jnp.sum(p, axis=-1, keepdims=True)
        acc = alpha * acc + _dot(p.astype(BF16), v)
        return m_new, l, acc

    init = (jnp.full((2 * tq, 1), NEG_INF, F32), jnp.zeros((2 * tq, 1), F32), jnp.zeros((2 * tq, LANE), F32))
    _, l, acc = lax.fori_loop(0, nk, body, init)
    _diff_finish(sc_ref, g_ref, o_ref, acc, l, tq, post_scale)


def _diff_offset_kernel(sc_ref, q_ref, qn_ref, k_ref, v_ref, g_ref, o_ref, acc_ref, p_ref,
                        *, lc, nt, tq, tk, post_scale):
    qi = pl.program_id(2)
    rows2 = 2 * tq
    nk = nt // tk
    ncq = lc // tq
    q = q_ref[...].reshape(rows2, LANE)
    off = sc_ref[1]

    def probs(qq, k0, tkk):
        return jnp.exp2(_dot_nt(qq, k_ref[k0:k0 + tkk, :]) - off).astype(BF16)

    def next_tile_probs():
        return probs(qn_ref[...].reshape(rows2, LANE), 0, tk)

    @pl.when(qi < ncq)
    def _():
        acc_ref[...] = _dot(probs(q, 0, lc), v_ref[0:lc, :])

        @pl.when(qi == ncq - 1)
        def _():
            p_ref[0] = next_tile_probs()

    def latent(first_slot):
        tot = None
        for t in range(nk):
            pv = _dot(p_ref[(t + first_slot) % 2], v_ref[t * tk:(t + 1) * tk, :])
            p_ref[(t + 1 + first_slot) % 2] = probs(q, (t + 1) * tk, tk) if t + 1 < nk else next_tile_probs()
            tot = pv if tot is None else tot + pv
        acc_ref[...] = tot

    first_slot = ((qi - ncq) * nk) % 2
    for slot in range(2 if nk % 2 else 1):
        @pl.when((qi >= ncq) & (first_slot == slot))
        def _():
            latent(slot)

    acc = acc_ref[...]
    _diff_finish(sc_ref, g_ref, o_ref, acc[:, 0:LANE], acc[:, LANE:LANE + 1], tq, post_scale)


def _diff(sc, qd, kd, vde, pvec, lc, post_scale):
    n_batch, _, nt, _ = qd.shape
    tq = _tile(math.gcd(lc, nt), 256)
    assert lc % tq == 0
    grid = (n_batch, DIFF_HEADS, nt // tq)
    smem = pl.BlockSpec(memory_space=pltpu.SMEM)
    qspec = pl.BlockSpec((None, 2, tq, LANE), lambda b, h, i: (b, 0, i, h))
    kspec = pl.BlockSpec((None, nt, LANE), lambda b, h, i: (b, 0, h))
    gspec = pl.BlockSpec((8, LANE), lambda b, h, i: (0, 0))
    ospec = pl.BlockSpec((None, tq, LANE), lambda b, h, i: (b, i, h))
    oshape = jax.ShapeDtypeStruct((n_batch, nt, DIFF_HEADS * HEAD_DIM), BF16)

    def offset(sc, qd, kd, vde, pvec):
        tk = _tile(nt, 768)
        kern = functools.partial(_diff_offset_kernel, lc=lc, nt=nt, tq=tq, tk=tk, post_scale=post_scale)
        nq = nt // tq
        qnext = pl.BlockSpec((None, 2, tq, LANE), lambda b, h, i: (b, 0, jnp.minimum(i + 1, nq - 1), h))
        return pl.pallas_call(
            kern, grid=grid,
            in_specs=[smem, qspec, qnext, kspec, pl.BlockSpec((None, nt, 2 * LANE), lambda b, h, i: (b, 0, h)),
                      gspec],
            out_specs=ospec, out_shape=oshape,
            scratch_shapes=[pltpu.VMEM((2 * tq, 2 * LANE), F32), pltpu.VMEM((2, 2 * tq, tk), BF16)],
            compiler_params=_params(("arbitrary",) * 3, 40),
            name="diff_attn_offset",
        )(sc, qd, qd, kd, vde, pvec)

    def online(sc, qd, kd, vde, pvec):
        tk = _tile(math.gcd(lc, nt), 256)
        kern = functools.partial(_diff_online_kernel, lc=lc, nt=nt, tq=tq, tk=tk, post_scale=post_scale)
        return pl.pallas_call(
            kern, grid=grid,
            in_specs=[smem, qspec, kspec, pl.BlockSpec((None, nt, LANE), lambda b, h, i: (b, 0, 2 * h)), gspec],
            out_specs=ospec, out_shape=oshape,
            compiler_params=_params(("arbitrary",) * 3, 32),
            name="diff_attn_online",
        )(sc, qd, kd, vde, pvec)

    return lax.cond(sc[1] <= ATTN_FAST_BOUND, offset, online, sc, qd, kd, vde, pvec)


LRU_HALO = 8


LRU_STEP_BLOCKS = 4


def _lru_gate_kernel(xp_ref, x_ref, xn_ref, cw_ref, cb_ref, gw_ref, gb_ref, lam_ref,
                     af_ref, bf_ref, ab_ref, bb_ref, *, t, seg_starts, seg_ends):
    k = pl.program_id(2)
    is_start = functools.reduce(jnp.logical_or, [k == s for s in seg_starts])
    is_end = functools.reduce(jnp.logical_or, [k == s for s in seg_ends])
    h = LRU_HALO
    outs = ((af_ref, bf_ref), (ab_ref, bb_ref))
    sigmoid = lambda z: 0.5 + 0.5 * jnp.tanh(0.5 * z)
    for c in range(LRU_STEP_BLOCKS):
        cs = slice(c * LANE, (c + 1) * LANE)
        cur = x_ref[:, cs]
        prev = jnp.where(is_start, 0.0, xp_ref[:, cs])
        nxt = jnp.where(is_end, 0.0, xn_ref[:, cs])
        ext = jnp.concatenate([prev, cur, nxt], axis=0)
        xl = (cb_ref[:, cs] + cw_ref[0:1, cs] * ext[h - 2:h - 2 + t, :] + cw_ref[1:2, cs] * ext[h - 1:h - 1 + t, :]
              + cw_ref[2:3, cs] * cur + cw_ref[3:4, cs] * ext[h + 1:h + 1 + t, :])
        xb = xl.astype(BF16)
        for d in range(2):
            r = sigmoid(_dot(xb, gw_ref[d, 0, c].astype(BF16)) + gb_ref[2 * d:2 * d + 1, cs])
            ig = sigmoid(_dot(xb, gw_ref[d, 1, c].astype(BF16)) + gb_ref[2 * d + 1:2 * d + 2, cs])
            a = jnp.exp(LRU_C * r * jax.nn.log_sigmoid(lam_ref[d:d + 1, cs]))
            outs[d][0][:, cs] = a
            outs[d][1][:, cs] = jnp.sqrt(1.0 - a * a) * (ig * xl)


def _lru_gates(u, conv_w, conv_b, gate_w, gate_b, lam, lc):
    n_batch, nt, _ = u.shape
    t = _tile(math.gcd(lc, nt), 256)
    nk = nt // t
    hb = t // LRU_HALO
    seg_starts = (0, lc // t)
    seg_ends = (lc // t - 1, nk - 1)
    w = LRU_STEP_BLOCKS * LANE
    assert OFF_XR % w == 0 and LRU_WIDTH % w == 0
    cb0 = OFF_XR // w
    kern = functools.partial(_lru_gate_kernel, t=t, seg_starts=seg_starts, seg_ends=seg_ends)
    out = pl.BlockSpec((None, t, w), lambda b, c, k: (b, k, c))
    osd = jax.ShapeDtypeStruct((n_batch, nt, LRU_WIDTH), F32)
    return pl.pallas_call(
        kern,
        grid=(n_batch, LRU_WIDTH // w, nk),
        in_specs=[pl.BlockSpec((None, LRU_HALO, w), lambda b, c, k: (b, jnp.maximum(k * hb - 1, 0), cb0 + c)),
                  pl.BlockSpec((None, t, w), lambda b, c, k: (b, k, cb0 + c)),
                  pl.BlockSpec((None, LRU_HALO, w),
                               lambda b, c, k: (b, jnp.minimum((k + 1) * hb, nt // LRU_HALO - 1), cb0 + c)),
                  pl.BlockSpec((4, w), lambda b, c, k: (0, c)),
                  pl.BlockSpec((1, w), lambda b, c, k: (0, c)),
                  pl.BlockSpec((2, 2, LRU_STEP_BLOCKS, LANE, LANE), lambda b, c, k: (0, 0, c, 0, 0)),
                  pl.BlockSpec((4, w), lambda b, c, k: (0, c)),
                  pl.BlockSpec((2, w), lambda b, c, k: (0, c))],
        out_specs=[out, out, out, out],
        out_shape=[osd, osd, osd, osd],
        compiler_params=_params(("arbitrary",) * 3, 24),
        name="lru_gates",
    )(u, u, u, conv_w, conv_b.reshape(1, LRU_WIDTH), gate_w, gate_b.reshape(4, LRU_WIDTH), lam)


def _lru_scan_kernel(af_ref, bf_ref, ab_ref, bb_ref, hf_ref, hb_ref, cf_ref, cbk_ref, *, t):
    @pl.when(pl.program_id(1) == 0)
    def _():
        cf_ref[...] = jnp.zeros_like(cf_ref)
        cbk_ref[...] = jnp.zeros_like(cbk_ref)

    ng = t // 8

    def body(g, carry):
        hf, hb = carry
        r0 = pl.multiple_of(g * 8, 8)
        a = af_ref[pl.ds(r0, 8), :]
        bt = bf_ref[pl.ds(r0, 8), :]
        rows = []
        for r in range(8):
            hf = a[r:r + 1, :] * hf + bt[r:r + 1, :]
            rows.append(hf)
        hf_ref[pl.ds(r0, 8), :] = jnp.concatenate(rows, axis=0)
        r1 = pl.multiple_of((ng - 1 - g) * 8, 8)
        a = ab_ref[pl.ds(r1, 8), :]
        bt = bb_ref[pl.ds(r1, 8), :]
        rows = []
        for r in range(7, -1, -1):
            hb = a[r:r + 1, :] * hb + bt[r:r + 1, :]
            rows.append(hb)
        hb_ref[pl.ds(r1, 8), :] = jnp.concatenate(rows[::-1], axis=0)
        return hf, hb

    hf, hb = lax.fori_loop(0, ng, body, (cf_ref[...], cbk_ref[...]))
    cf_ref[...] = hf
    cbk_ref[...] = hb


def _lru_scan(af, bf, ab, bb, lc):
    n_batch, nt, w = af.shape
    t = lc
    assert nt % t == 0
    nk = nt // t
    kern = functools.partial(_lru_scan_kernel, t=t)
    fwd = pl.BlockSpec((None, t, w), lambda b, k: (b, k, 0))
    bwd = pl.BlockSpec((None, t, w), lambda b, k: (b, jnp.where(k == 0, 0, nk - k), 0))
    osd = jax.ShapeDtypeStruct((n_batch, nt, w), F32)
    return pl.pallas_call(
        kern,
        grid=(n_batch, nk),
        in_specs=[fwd, fwd, bwd, bwd],
        out_specs=[fwd, bwd],
        out_shape=[osd, osd],
        scratch_shapes=[pltpu.VMEM((1, w), F32), pltpu.VMEM((1, w), F32)],
        compiler_params=_params(("arbitrary", "arbitrary"), 24),
        name="lru_scan",
    )(af, bf, ab, bb)


def _outproj_kernel(oa_ref, od_ref, os_ref, hf_ref, hb_ref, g0_ref, g1_ref, gate_ref, w_ref, x_ref,
                    o_ref, lhs_ref, *, n_batch, lc, tm):
    b = pl.program_id(0)
    i = pl.program_id(1)

    @pl.when(pl.program_id(2) == 0)
    def _():
        def chunk(r0):
            rs = pl.ds(r0, ROW_CHUNK)
            lhs_ref[rs, 0:1024] = oa_ref[rs, :]
            lhs_ref[rs, 1024:2048] = od_ref[rs, :]
            lhs_ref[rs, 2048:3072] = os_ref[rs, :]
            for c, g_ref in enumerate((g0_ref, g1_ref)):
                cs = slice(c * 512, (c + 1) * 512)
                y = (hf_ref[rs, cs] + hb_ref[rs, cs]) * jax.nn.gelu(g_ref[rs, :])
                lhs_ref[rs, 3072 + c * 512:3584 + c * 512] = y.astype(BF16)

        _for_row_chunks(tm, chunk)

    row = i * tm + lax.broadcasted_iota(I32, (tm, 1), 0)
    gate = jnp.where(row < lc, gate_ref[pl.ds(n_batch, 1), :], gate_ref[pl.ds(b, 1), :])
    o_ref[...] = x_ref[...] + gate * _dot(lhs_ref[...], w_ref[...])


def _outproj(o_a, o_d, o_s, hf, hb, u, modl, w_bf, xs, lc):
    n_batch, nt, d = xs.shape
    tm = _tile(nt, 768)
    tn = _tile(d, 512)
    ntile = d // tn
    kern = functools.partial(_outproj_kernel, n_batch=n_batch, lc=lc, tm=tm)
    mix = pl.BlockSpec((None, tm, 1024), lambda b, i, j: (b, i, 0))
    gr = lambda c: pl.BlockSpec((None, tm, 512), lambda b, i, j: (b, i, OFF_GR // 512 + c))
    xblk = pl.BlockSpec((None, tm, tn), lambda b, i, j: (b, i, j))
    return pl.pallas_call(
        kern,
        grid=(n_batch, nt // tm, ntile),
        in_specs=[mix, mix, mix, mix, mix, gr(0), gr(1),
                  pl.BlockSpec((MOD_ROWS, tn), lambda b, i, j: (0, 2 * ntile + j)),
                  pl.BlockSpec((D_MIX, tn), lambda b, i, j: (0, j)), xblk],
        out_specs=xblk,
        out_shape=jax.ShapeDtypeStruct(xs.shape, F32),
        scratch_shapes=[pltpu.VMEM((tm, D_MIX), BF16)],
        input_output_aliases={9: 0},
        compiler_params=_params(("arbitrary",) * 3, 56),
        name="out_proj",
    )(o_a, o_d, o_s, hf, hb, u, u, modl, w_bf, xs)


def _split2(x):
    hi = x.astype(BF16)
    return hi, (x - hi.astype(F32)).astype(BF16)


ROUTER_KC = 512


HI16 = -65536


def _pack_bf16_pair(lo, hi):
    bits = lambda v: lax.bitcast_convert_type(v.astype(BF16).astype(F32), I32)
    return (bits(hi) & HI16) | lax.shift_right_logical(bits(lo), 16)


def _unpack_bf16_pair(w):
    lo = lax.bitcast_convert_type(lax.shift_left(w, 16), F32)
    hi = lax.bitcast_convert_type(w & HI16, F32)
    return lo.astype(BF16), hi.astype(BF16)


def _router_kernel(x_ref, g_ref, sh_ref, sc_ref, w0_ref, w1_ref, xm_ref, aff_ref, xf_ref, *, n_batch, lc, tm):
    b = pl.program_id(0)
    i = pl.program_id(1)
    d = x_ref.shape[1]
    half = d // 2

    def chunk(r0):
        rs = pl.ds(r0, ROW_CHUNK)
        sh = _mod_row(sh_ref, i * tm + r0, b, n_batch, lc)
        sc = _mod_row(sc_ref, i * tm + r0, b, n_batch, lc)
        xm = _norm_modulate(x_ref[rs, :], g_ref[...], sh, sc)
        xf_ref[rs, :] = xm
        xm_ref[rs, 0:half] = _pack_bf16_pair(xm[:, 0:half], xm[:, half:d])

    _for_row_chunks(tm, chunk)

    logits = jnp.zeros((tm, LANE), F32)
    tk = _tile(d, ROUTER_KC)
    for kc in range(d // tk):
        ks = slice(kc * tk, (kc + 1) * tk)
        x0, x1 = _split2(xf_ref[:, ks])
        logits = logits + (_dot(x0, w0_ref[ks, :]) + (_dot(x0, w1_ref[ks, :]) + _dot(x1, w0_ref[ks, :])))
    lane = lax.broadcasted_iota(I32, (1, LANE), 1)
    logits = jnp.where(lane < N_EXPERTS, logits, NEG_INF)
    m = jnp.max(logits, axis=-1, keepdims=True)
    p = jnp.exp(logits - m)
    aff = p / jnp.sum(p, axis=-1, keepdims=True)
    aff_ref[...] = aff
    xm_ref[:, half:half + LANE] = lax.bitcast_convert_type(aff, I32)


def _router(xs, g, modl, wr_pad, lc):
    n_batch, nt, d = xs.shape
    tm = _tile(nt, 256)
    assert lc % ROW_CHUNK == 0
    kern = functools.partial(_router_kernel, n_batch=n_batch, lc=lc, tm=tm)
    wspec = pl.BlockSpec((d, LANE), lambda b, i: (0, 0))
    return pl.pallas_call(
        kern,
        grid=(n_batch, nt // tm),
        in_specs=[pl.BlockSpec((None, tm, d), lambda b, i: (b, i, 0)),
                  pl.BlockSpec((1, d), lambda b, i: (0, 0)),
                  pl.BlockSpec((MOD_ROWS, d), lambda b, i: (0, 3)),
                  pl.BlockSpec((MOD_ROWS, d), lambda b, i: (0, 4)),
                  wspec, wspec],
        out_specs=[pl.BlockSpec((None, tm, d // 2 + LANE), lambda b, i: (b, i, 0)),
                   pl.BlockSpec((None, tm, LANE), lambda b, i: (b, i, 0))],
        out_shape=[jax.ShapeDtypeStruct((n_batch, nt, d // 2 + LANE), I32),
                   jax.ShapeDtypeStruct((n_batch, nt, LANE), F32)],
        scratch_shapes=[pltpu.VMEM((tm, d), F32)],
        compiler_params=_params(("arbitrary", "arbitrary"), 48),
        name="norm2_router",
    )(xs, g, modl, modl, *_split2(wr_pad))


def _select_kernel(aff_ref, l_ref, cnt_ref, off_ref, bits_ref, *, row0, n, cap):
    nc = n // LANE
    bits_ref[...] = lax.bitcast_convert_type(aff_ref[row0:row0 + n, :], I32)

    def bit_step(i, thr):
        cand = thr | (jnp.int32(1) << (30 - i))
        cnt = jnp.sum((bits_ref[...] >= cand).astype(I32), axis=0, keepdims=True)
        return jnp.where(cnt >= cap, cand, thr)

    thr = lax.fori_loop(0, 31, bit_step, jnp.zeros((1, LANE), I32))
    n_gt = jnp.sum((bits_ref[...] > thr).astype(I32), axis=0, keepdims=True)
    need = (cap - n_gt).astype(F32)

    r_io = lax.broadcasted_iota(I32, (LANE, LANE), 0)
    c_io = lax.broadcasted_iota(I32, (LANE, LANE), 1)
    ltri = (c_io <= r_io).astype(BF16)
    t_col = r_io.astype(F32)
    j_row = c_io.astype(F32)

    def chunk(c, carry):
        run_eq, run_sel = carry
        r0 = pl.multiple_of(c * LANE, LANE)
        bc = bits_ref[pl.ds(r0, LANE), :]
        eq = bc == thr
        eq_f = eq.astype(F32)
        incl_eq = _dot(ltri, eq_f.astype(BF16))
        rank = run_eq + incl_eq - eq_f
        sel = (bc > thr) | (eq & (rank < need))
        sel_f = sel.astype(F32)
        incl = _dot(ltri, sel_f.astype(BF16))
        cnt_c = incl[LANE - 1:LANE, :]
        qm = jnp.where(sel, incl - sel_f, -1.0)
        for e in range(N_EXPERTS):
            pos = jnp.broadcast_to(qm[:, e:e + 1], (LANE, LANE))
            local = jnp.sum(jnp.where(pos == j_row, t_col, 0.0), axis=0, keepdims=True)
            l_ref[e, pl.ds(c, 1), :] = local.astype(I32)
        cnt_ref[pl.ds(c, 1), :] = cnt_c.astype(I32)
        off_ref[pl.ds(c, 1), :] = run_sel.astype(I32)
        return run_eq + incl_eq[LANE - 1:LANE, :], run_sel + cnt_c

    zero = jnp.zeros((1, LANE), F32)
    lax.fori_loop(0, nc, chunk, (zero, zero))


def _select(affp, row0, n, cap):
    n_batch, nt, _ = affp.shape
    nc = n // LANE
    kern = functools.partial(_select_kernel, row0=row0, n=n, cap=cap)
    return pl.pallas_call(
        kern,
        grid=(n_batch,),
        in_specs=[pl.BlockSpec((None, nt, LANE), lambda b: (b, 0, 0))],
        out_specs=[pl.BlockSpec((None, N_EXPERTS, nc, LANE), lambda b: (b, 0, 0, 0)),
                   pl.BlockSpec((None, nc, LANE), lambda b: (b, 0, 0)),
                   pl.BlockSpec((None, nc, LANE), lambda b: (b, 0, 0))],
        out_shape=[jax.ShapeDtypeStruct((n_batch, N_EXPERTS, nc, LANE), I32),
                   jax.ShapeDtypeStruct((n_batch, nc, LANE), I32),
                   jax.ShapeDtypeStruct((n_batch, nc, LANE), I32)],
        scratch_shapes=[pltpu.VMEM((n, LANE), I32)],
        compiler_params=_params(("arbitrary",), 40),
        name="ec_select",
    )(affp)


MOE_ROWS = 256


def _moe_kernel(l_hbm, cnt_hbm, off_hbm, xm_hbm, xs_in_hbm, g2_ref, w1_ref, w3_ref, w2_ref, xs_hbm,
                l_s, cnt_s, off_s, idx_s, xg, rows, xb, hdn_s, sem, *, row0, nc, cap, g_row):
    del xs_in_hbm
    b = pl.program_id(0)
    e = pl.program_id(1)
    step = b * pl.num_programs(1) + e
    last_step = pl.num_programs(0) * pl.num_programs(1) - 1
    tables = (pltpu.make_async_copy(l_hbm.at[b, e], l_s, sem.at[3]),
              pltpu.make_async_copy(cnt_hbm.at[b], cnt_s, sem.at[4]),
              pltpu.make_async_copy(off_hbm.at[b], off_s, sem.at[5]))
    for cp in tables:
        cp.start()
    for cp in tables:
        cp.wait()

    def x_gather(tok, slot, n_rows=1):
        return pltpu.make_async_copy(xm_hbm.at[b, pl.ds(tok, n_rows), :], xg.at[pl.ds(slot, n_rows), :], sem.at[0])

    def row_gather(tok, slot, n_rows=1):
        return pltpu.make_async_copy(xs_hbm.at[b, pl.ds(tok, n_rows), :], rows.at[pl.ds(slot, n_rows), :], sem.at[1])

    def row_scatter(tok, slot, n_rows=1):
        return pltpu.make_async_copy(rows.at[pl.ds(slot, n_rows), :], xs_hbm.at[b, pl.ds(tok, n_rows), :], sem.at[2])

    def chunk(c, _):
        base = row0 + c * LANE
        off_c = off_s[c, e]

        def one(j, _):
            idx_s[off_c + j] = base + l_s[c, j]
            return 0

        lax.fori_loop(0, cnt_s[c, e], one, 0)
        return 0

    lax.fori_loop(0, nc, chunk, 0)

    def start_all(copy):
        def one(s, _):
            copy(idx_s[s], s).start()
            return 0

        lax.fori_loop(0, cap, one, 0, unroll=8)

    start_all(x_gather)

    @pl.when(step > 0)
    def _():
        row_scatter(row0, 0, cap).wait()

    start_all(row_gather)
    x_gather(row0, 0, cap).wait()

    d = rows.shape[1]
    tn = _tile(d, 512)
    rc = min(cap, MOE_ROWS)
    lane = lax.broadcasted_iota(I32, (1, LANE), 1)
    g2_row = g_row if g_row is not None else b

    half = d // 2
    th = _tile(half, 512)

    def hidden(r, _):
        rs = pl.ds(pl.multiple_of(r * rc, rc), rc)
        for kc in range(half // th):
            ks = slice(kc * th, (kc + 1) * th)
            lo, hi = _unpack_bf16_pair(xg[rs, ks])
            xb[:, ks] = lo
            xb[:, half + kc * th:half + (kc + 1) * th] = hi
        h1 = _dot(xb[...], w1_ref[...])
        h3 = _dot(xb[...], w3_ref[...])
        aff = lax.bitcast_convert_type(xg[rs, half:half + LANE], F32)
        gate = jnp.sum(jnp.where(lane == e, aff, 0.0), axis=-1, keepdims=True)
        hdn_s[rs, :] = (h1 * jax.nn.sigmoid(h1) * h3 * gate).astype(BF16)
        return 0

    lax.fori_loop(0, cap // rc, hidden, 0)
    row_gather(row0, 0, cap).wait()

    def down(r, _):
        rs = pl.ds(pl.multiple_of(r * rc, rc), rc)
        for c in range(d // tn):
            sl = slice(c * tn, (c + 1) * tn)
            rows[rs, sl] = rows[rs, sl] + g2_ref[pl.ds(g2_row, 1), sl] * _dot(hdn_s[rs, :], w2_ref[:, sl])
        return 0

    lax.fori_loop(0, cap // rc, down, 0)
    start_all(row_scatter)

    @pl.when(step == last_step)
    def _():
        row_scatter(row0, 0, cap).wait()


def _moe(ltab, cnt, off, xma, xs, modl, w1, w3, w2, row0, cap, g_row):
    n_batch, nt, d = xs.shape
    nc = ltab.shape[2]
    nd = 1
    rc = min(cap, MOE_ROWS)
    assert cap % rc == 0
    kern = functools.partial(_moe_kernel, row0=row0, nc=nc, cap=cap, g_row=g_row)
    any_ = pl.BlockSpec(memory_space=pl.ANY)
    return pl.pallas_call(
        kern,
        grid=(n_batch, N_EXPERTS),
        in_specs=[any_, any_, any_, any_, any_,
                  pl.BlockSpec((MOD_ROWS, d), lambda b, e: (0, 5 * nd)),
                  pl.BlockSpec((None, d, EXPERT_FF), lambda b, e: (e, 0, 0)),
                  pl.BlockSpec((None, d, EXPERT_FF), lambda b, e: (e, 0, 0)),
                  pl.BlockSpec((None, EXPERT_FF, d), lambda b, e: (e, 0, 0))],
        out_specs=any_,
        out_shape=jax.ShapeDtypeStruct(xs.shape, F32),
        scratch_shapes=[pltpu.SMEM((nc, LANE), I32), pltpu.SMEM((nc, LANE), I32), pltpu.SMEM((nc, LANE), I32),
                        pltpu.SMEM((cap,), I32), pltpu.VMEM((cap, d // 2 + LANE), I32), pltpu.VMEM((cap, d), F32),
                        pltpu.VMEM((rc, d), BF16), pltpu.VMEM((cap, EXPERT_FF), BF16),
                        pltpu.SemaphoreType.DMA((6,))],
        input_output_aliases={4: 0},
        compiler_params=_params(("arbitrary", "arbitrary"), 56),
        name="moe_ffn",
    )(ltab, cnt, off, xma, xs, modl, w1, w3, w2)


def _rope_tables(n, lc):
    t = jnp.arange(n)
    pos = jnp.stack([t // GRID_W, t % GRID_W], axis=-1).astype(F32)

    def tables(dh):
        quarter = dh // 4
        inv_freq = ROPE_THETA ** (-jnp.arange(quarter, dtype=F32) / quarter)
        ang = pos[:, :, None] * inv_freq
        cos, sin = jnp.cos(ang), jnp.sin(ang)
        c = jnp.concatenate([cos[:, 0], cos[:, 0], cos[:, 1], cos[:, 1]], axis=-1)
        s = jnp.concatenate([-sin[:, 0], sin[:, 0], -sin[:, 1], sin[:, 1]], axis=-1)
        reps = LANE // dh
        c, s = jnp.tile(c, (1, reps)), jnp.tile(s, (1, reps))
        return (jnp.concatenate([jnp.ones((lc, LANE), F32), c], axis=0),
                jnp.concatenate([jnp.zeros((lc, LANE), F32), s], axis=0))

    return tables(HEAD_DIM) + tables(DIFF_QK_DIM)


def kernel(x, c, ctx, c_ctx, w_ada, b_ada, norm_g, w_in, w_out, qk_g_na, na_rpb, qk_g_diff, diff_lambda,
           diff_subln_g, qk_g_swa, swa_sink, lru_conv_w, lru_conv_b, lru_gate_w, lru_gate_b, lru_lam,
           w_router, w1, w3, w2):
    n_batch, n, d = x.shape
    lc = ctx.shape[1]
    depth = w_ada.shape[0]
    assert n_batch < MOD_ROWS and w_in.shape[2] == D_IN

    cvec = jnp.zeros((MOD_ROWS, d), F32).at[:n_batch].set(c).at[n_batch].set(c_ctx)
    mod = _ada(cvec, w_ada, b_ada)
    cs, ss, cd, sd = _rope_tables(n, lc)
    xs = jnp.concatenate([ctx, x], axis=1)

    cap_lat = CAPACITY_FACTOR * n // N_EXPERTS
    cap_ctx = CAPACITY_FACTOR * lc // N_EXPERTS

    for l in range(depth):
        modl = mod[l]
        u = _inproj(xs, norm_g[l, 0].reshape(1, d), modl, w_in[l].astype(BF16), lc)

        pvec = jnp.concatenate([qk_g_na[l], jnp.tile(qk_g_diff[l], (1, 2)), qk_g_swa[l],
                                diff_subln_g[l].reshape(1, LANE), jnp.zeros((1, LANE), F32)], axis=0)
        qa, ka, va, qd, kd, vd, qs, ks, vs = _prep(u, pvec, cs, ss, cd, sd)

        gmax = lambda g: jnp.max(jnp.abs(g[0])) * jnp.max(jnp.abs(g[1]))
        b_na = (BOUND_SLACK * HEAD_DIM * ATTN_Q_SCALE * gmax(qk_g_na[l])
                + LOG2E * jnp.max(jnp.abs(na_rpb[l])))
        b_swa = jnp.maximum(BOUND_SLACK * HEAD_DIM * ATTN_Q_SCALE * gmax(qk_g_swa[l]),
                            LOG2E * jnp.max(jnp.abs(swa_sink[l])))
        b_diff = BOUND_SLACK * DIFF_QK_DIM * DIFF_Q_SCALE * gmax(qk_g_diff[l])

        o_a = _na(b_na, qa, ka, va, _na_bias_tables(na_rpb[l], n), lc)

        lambda_init = 0.8 - 0.6 * math.exp(-0.3 * l)
        dl = diff_lambda[l].astype(F32)
        lam = jnp.exp(jnp.sum(dl[0] * dl[1])) - jnp.exp(jnp.sum(dl[2] * dl[3])) + lambda_init
        o_d = _diff(jnp.stack([lam, b_diff]).astype(F32), qd, kd, vd, pvec, lc, 1.0 - lambda_init)

        o_s = _swa(b_swa, qs, ks, vs, swa_sink[l], lc)

        af, bf, ab, bb = _lru_gates(u, lru_conv_w[l], lru_conv_b[l], lru_gate_w[l], lru_gate_b[l], lru_lam[l], lc)
        hf, hb = _lru_scan(af, bf, ab, bb, lc)

        xs = _outproj(o_a, o_d, o_s, hf, hb, u, modl, w_out[l].astype(BF16), xs, lc)

        wr_pad = jnp.zeros((d, LANE), F32).at[:, :N_EXPERTS].set(w_router[l])
        xma, affp = _router(xs, norm_g[l, 1].reshape(1, d), modl, wr_pad, lc)
        w1b, w3b, w2b = w1[l].astype(BF16), w3[l].astype(BF16), w2[l].astype(BF16)
        lt, cnt, off = _select(affp, lc, n, cap_lat)
        xs = _moe(lt, cnt, off, xma, xs, modl, w1b, w3b, w2b, lc, cap_lat, None)
        if l < depth - 1:
            lt, cnt, off = _select(affp, 0, lc, cap_ctx)
            xs = _moe(lt, cnt, off, xma, xs, modl, w1b, w3b, w2b, 0, cap_ctx, n_batch)
    return xs[:, lc:, :]
```

```python
import functools
import math

import numpy as np
import jax
import jax.numpy as jnp
from jax import lax
from jax.experimental import pallas as pl
from jax.experimental.pallas import tpu as pltpu

GRID_W = 64
HEAD_DIM = 128
NA_HEADS = 8
NA_KH = 8
NA_KW = 16
DIFF_HEADS = 8
DIFF_QK_DIM = 64
SWA_Q_HEADS = 8
SWA_KV_HEADS = 2
SWA_GROUP = SWA_Q_HEADS // SWA_KV_HEADS
SWA_WINDOW = 128
LRU_WIDTH = 1024
LRU_BLOCKS = 8
LRU_C = 8.0
N_EXPERTS = 16
EXPERT_FF = 256
CAPACITY_FACTOR = 2
ROPE_THETA = 10000.0
NEG_INF = -1e30
EPS = 1e-6

OFF_QA, OFF_KA, OFF_VA = 0, 1024, 2048
OFF_QD, OFF_KD, OFF_VD = 3072, 4096, 5120
OFF_QS, OFF_KS, OFF_VS = 6144, 7168, 7424
OFF_XR, OFF_GR = 7680, 8704
D_ATT = 7680
D_IN = 9728
D_MIX = 4096

LOG2E = 1.4426950408889634
DIFF_Q_SCALE = DIFF_QK_DIM ** -0.5 * LOG2E
ATTN_FAST_BOUND = 60.0
ATTN_Q_SCALE = HEAD_DIM ** -0.5 * LOG2E
BOUND_SLACK = 1.02

LANE = 128
MOD_ROWS = 8
V7X_VMEM_BYTES = 64 * 1024 * 1024

F32 = jnp.float32
BF16 = jnp.bfloat16
I32 = jnp.int32


def _tile(n, target, mult=LANE):
    best = None
    for t in range(mult, min(n, target) + 1, mult):
        if n % t == 0:
            best = t
    assert best is not None, (n, target, mult)
    return best


def _params(sem, vmem_mb):
    return pltpu.CompilerParams(dimension_semantics=sem, vmem_limit_bytes=min(vmem_mb, 60) * 1024 * 1024)


def _dot_nt(a, b):
    return lax.dot_general(a, b, (((1,), (1,)), ((), ())), preferred_element_type=F32)


def _dot(a, b):
    return jnp.dot(a, b, preferred_element_type=F32)


def _ada_kernel(c_ref, w_ref, b_ref, o_ref):
    cv = c_ref[...]
    s = cv * jax.nn.sigmoid(cv)
    s_hi = s.astype(BF16)
    s_lo = (s - s_hi.astype(F32)).astype(BF16)
    w = w_ref[...].astype(BF16)
    o_ref[...] = _dot(s_hi, w) + _dot(s_lo, w) + b_ref[...]


def _ada(cvec, w_ada, b_ada):
    depth, d, n6 = w_ada.shape
    tn = _tile(n6, 1024)
    return pl.pallas_call(
        _ada_kernel,
        grid=(depth, n6 // tn),
        in_specs=[pl.BlockSpec((MOD_ROWS, d), lambda l, j: (0, 0)),
                  pl.BlockSpec((None, d, tn), lambda l, j: (l, 0, j)),
                  pl.BlockSpec((None, 1, tn), lambda l, j: (l, 0, j))],
        out_specs=pl.BlockSpec((None, MOD_ROWS, tn), lambda l, j: (l, 0, j)),
        out_shape=jax.ShapeDtypeStruct((depth, MOD_ROWS, n6), F32),
        compiler_params=_params(("arbitrary", "arbitrary"), 56),
        name="ada_mod",
    )(cvec, w_ada, b_ada.reshape(depth, 1, n6))


ROW_CHUNK = 32


def _mod_row(ref, row0, b, n_batch, lc):
    return jnp.where(row0 < lc, ref[pl.ds(n_batch, 1), :], ref[pl.ds(b, 1), :])


def _norm_modulate(x, g, sh, sc):
    ms = jnp.mean(x * x, axis=-1, keepdims=True)
    return x * lax.rsqrt(ms + EPS) * g * (1.0 + sc) + sh


def _for_row_chunks(tm, fn):
    assert tm % ROW_CHUNK == 0

    def body(c, _):
        fn(pl.multiple_of(c * ROW_CHUNK, ROW_CHUNK))
        return 0

    lax.fori_loop(0, tm // ROW_CHUNK, body, 0)


def _inproj_kernel(x_ref, g_ref, sh_ref, sc_ref, w_ref, o_ref, xn_ref, *, n_batch, lc, tm):
    b = pl.program_id(0)
    i = pl.program_id(1)

    @pl.when(pl.program_id(2) == 0)
    def _():
        def chunk(r0):
            sh = _mod_row(sh_ref, i * tm + r0, b, n_batch, lc)
            sc = _mod_row(sc_ref, i * tm + r0, b, n_batch, lc)
            xn = _norm_modulate(x_ref[pl.ds(r0, ROW_CHUNK), :], g_ref[...], sh, sc)
            xn_ref[pl.ds(r0, ROW_CHUNK), :] = xn.astype(BF16)

        _for_row_chunks(tm, chunk)

    o_ref[...] = _dot(xn_ref[...], w_ref[...])


def _inproj(xs, g, modl, w_bf, layer, lc):
    n_batch, nt, d = xs.shape
    d_in = w_bf.shape[2]
    tm = _tile(nt, 768)
    tn = _tile(d_in, 512)
    assert lc % ROW_CHUNK == 0
    kern = functools.partial(_inproj_kernel, n_batch=n_batch, lc=lc, tm=tm)
    return pl.pallas_call(
        kern,
        grid=(n_batch, nt // tm, d_in // tn),
        in_specs=[pl.BlockSpec((None, tm, d), lambda b, i, j: (b, i, 0)),
                  pl.BlockSpec((1, d), lambda b, i, j: (0, 0)),
                  pl.BlockSpec((MOD_ROWS, d), lambda b, i, j: (0, 0)),
                  pl.BlockSpec((MOD_ROWS, d), lambda b, i, j: (0, 1)),
                  pl.BlockSpec((None, d, tn), lambda b, i, j: (layer, 0, j))],
        out_specs=pl.BlockSpec((None, tm, tn), lambda b, i, j: (b, i, j)),
        out_shape=jax.ShapeDtypeStruct((n_batch, nt, d_in), F32),
        scratch_shapes=[pltpu.VMEM((tm, d), BF16)],
        compiler_params=_params(("arbitrary", "arbitrary", "arbitrary"), 56),
        name="in_proj",
    )(xs, g, modl, modl, w_bf)


def _rms_full(x, g):
    ms = jnp.mean(x * x, axis=-1, keepdims=True)
    return x * lax.rsqrt(ms + EPS) * g


def _group_ones(width):
    r = lax.broadcasted_iota(I32, (LANE, LANE), 0) // width
    c = lax.broadcasted_iota(I32, (LANE, LANE), 1) // width
    return jnp.where(r == c, 1.0, 0.0).astype(BF16)


def _rms_groups(x, g, ones, width):
    x2 = x * x
    hi = x2.astype(BF16)
    lo = (x2 - hi.astype(F32)).astype(BF16)
    ms = (_dot(hi, ones) + _dot(lo, ones)) * (1.0 / width)
    return x * lax.rsqrt(ms + EPS) * g


def _rope(x, cos, sin_signed, half):
    lane = lax.broadcasted_iota(I32, (1, LANE), 1)
    up = pltpu.roll(x, LANE - half, axis=1)
    dn = pltpu.roll(x, half, axis=1)
    partner = jnp.where((lane & half) == 0, up, dn)
    return x * cos + partner * sin_signed


def _prep_kernel(u_ref, p_ref, cs_ref, ss_ref, cd_ref, sd_ref,
                 qa_ref, ka_ref, va_ref, qd_ref, kd_ref, vd_ref, qs_ref, ks_ref, vs_ref):
    g_na_q, g_na_k = p_ref[0:1, :], p_ref[1:2, :]
    g_d_q, g_d_k = p_ref[2:3, :], p_ref[3:4, :]
    g_s_q, g_s_k = p_ref[4:5, :], p_ref[5:6, :]
    cs, ss, cd, sd = cs_ref[...], ss_ref[...], cd_ref[...], sd_ref[...]
    lo = lax.broadcasted_iota(I32, (1, LANE), 1) < DIFF_QK_DIM

    def col(off, h):
        return u_ref[:, off + h * LANE: off + (h + 1) * LANE]

    def put(ref, h, val):
        ref[:, h * LANE:(h + 1) * LANE] = val.astype(BF16)

    ones_head = _group_ones(HEAD_DIM)
    ones_half = _group_ones(DIFF_QK_DIM)
    rms_head = lambda x, g: _rms_groups(x, g, ones_head, HEAD_DIM)
    rms_half = lambda x, g: _rms_groups(x, g, ones_half, DIFF_QK_DIM)

    for h in range(NA_HEADS):
        put(qa_ref, h, rms_head(col(OFF_QA, h), g_na_q) * ATTN_Q_SCALE)
        put(ka_ref, h, rms_head(col(OFF_KA, h), g_na_k))
        put(va_ref, h, col(OFF_VA, h))
    ones_col = jnp.where(lax.broadcasted_iota(I32, (u_ref.shape[0], LANE), 1) == 0, 1.0, 0.0).astype(BF16)
    for h in range(DIFF_HEADS):
        q = _rope(rms_half(col(OFF_QD, h), g_d_q), cd, sd, DIFF_QK_DIM // 4) * DIFF_Q_SCALE
        qd_ref[0, :, h * LANE:(h + 1) * LANE] = jnp.where(lo, q, 0.0).astype(BF16)
        qd_ref[1, :, h * LANE:(h + 1) * LANE] = jnp.where(lo, 0.0, q).astype(BF16)
        put(kd_ref, h, _rope(rms_half(col(OFF_KD, h), g_d_k), cd, sd, DIFF_QK_DIM // 4))
        put(vd_ref, 2 * h, col(OFF_VD, h))
        vd_ref[:, (2 * h + 1) * LANE:(2 * h + 2) * LANE] = ones_col
    for h in range(SWA_Q_HEADS):
        put(qs_ref, h, _rope(rms_head(col(OFF_QS, h), g_s_q), cs, ss, HEAD_DIM // 4) * ATTN_Q_SCALE)
    for h in range(SWA_KV_HEADS):
        put(ks_ref, h, _rope(rms_head(col(OFF_KS, h), g_s_k), cs, ss, HEAD_DIM // 4))
        put(vs_ref, h, col(OFF_VS, h))


def _prep(u, pvec, cs, ss, cd, sd):
    n_batch, nt, _ = u.shape
    tr = _tile(nt, 256)
    row = lambda w: pl.BlockSpec((None, tr, w), lambda b, i: (b, i, 0))
    tab = pl.BlockSpec((tr, LANE), lambda b, i: (i, 0))
    sd_ = lambda w: jax.ShapeDtypeStruct((n_batch, nt, w), BF16)
    return pl.pallas_call(
        _prep_kernel,
        grid=(n_batch, nt // tr),
        in_specs=[row(D_ATT), pl.BlockSpec((8, LANE), lambda b, i: (0, 0)), tab, tab, tab, tab],
        out_specs=[row(1024), row(1024), row(1024),
                   pl.BlockSpec((None, 2, tr, 1024), lambda b, i: (b, 0, i, 0)),
                   row(1024), row(2048), row(1024), row(256), row(256)],
        out_shape=[sd_(1024), sd_(1024), sd_(1024),
                   jax.ShapeDtypeStruct((n_batch, 2, nt, 1024), BF16),
                   sd_(1024), sd_(2048), sd_(1024), sd_(256), sd_(256)],
        compiler_params=_params(("arbitrary", "arbitrary"), 48),
        name="qkv_prep",
    )(u, pvec, cs, ss, cd, sd)


NA_QBLK = 2 * GRID_W
NA_KROWS = 10
NA_KBLK = NA_KROWS * GRID_W


def _skewed(n_chains, scores, numerators, output):
    s_prev = p_prev = None
    for c in range(n_chains + 2):
        s_new = scores(c) if c < n_chains else None
        p_new = numerators(c - 1, s_prev) if s_prev is not None else None
        if p_prev is not None:
            output(c - 2, p_prev)
        s_prev, p_prev = s_new, p_new


def _na_kernel(sc_ref, q_ref, k_ref, v_ref, *rest, lc, nlb, group, online):
    bias_refs, o_ref = rest[:group], rest[group]
    i0 = pl.program_id(2) * group
    kc = k_ref[0:lc, :]
    vc = v_ref[0:lc, :]

    def window_start(g):
        li = i0 + g - lc // NA_QBLK
        kb = jnp.clip(li - 2, 0, nlb - NA_KROWS // 2)
        return pl.multiple_of(lc + kb * NA_QBLK, NA_QBLK)

    def scores(g):
        q = q_ref[g * NA_QBLK:(g + 1) * NA_QBLK, :]
        return _dot_nt(q, k_ref[pl.ds(window_start(g), NA_KBLK), :]) + bias_refs[g][...], _dot_nt(q, kc)

    def numerators(g, s):
        s_n, s_c = s
        if online:
            m = jnp.maximum(jnp.max(s_n, axis=-1, keepdims=True), jnp.max(s_c, axis=-1, keepdims=True))
        else:
            m = sc_ref[0]
        p_n = jnp.exp2(s_n - m)
        p_c = jnp.exp2(s_c - m)
        l = jnp.sum(p_n, axis=-1, keepdims=True) + jnp.sum(p_c, axis=-1, keepdims=True)
        return p_n.astype(BF16), p_c.astype(BF16), l

    def output(g, p):
        o = _dot(p[0], v_ref[pl.ds(window_start(g), NA_KBLK), :]) + _dot(p[1], vc)
        o_ref[g * NA_QBLK:(g + 1) * NA_QBLK, :] = (o / p[2]).astype(BF16)

    _skewed(group, scores, numerators, output)


def _na_bias_tables(rpb, n):
    rows = n // GRID_W
    nlb = rows // 2
    reps = [0, 1, 2, nlb - 2, nlb - 1]
    h = rpb.shape[0]
    cols = np.arange(GRID_W)
    cstart = np.clip(cols - NA_KW // 2, 0, GRID_W - NA_KW)
    col_ok = (cols[None, :] >= cstart[:, None]) & (cols[None, :] < cstart[:, None] + NA_KW)
    dx = cols[None, :] - cols[:, None] + (NA_KW - 1)
    sel = (dx[None] == np.arange(2 * NA_KW - 1)[:, None, None]) & col_ok[None]
    tx = jnp.einsum('hyx,xqk->hyqk', rpb.astype(F32), jnp.asarray(sel, F32), precision=lax.Precision.HIGHEST)
    dy_i = np.zeros((5, 2, NA_KROWS), np.int32)
    row_ok = np.zeros((5, 2, NA_KROWS), bool)
    for c, li in enumerate(reps):
        kb = int(np.clip(li - 2, 0, nlb - NA_KROWS // 2))
        qr = 2 * li + np.arange(2)
        kr = 2 * kb + np.arange(NA_KROWS)
        rs = np.clip(qr - NA_KH // 2, 0, rows - NA_KH)
        row_ok[c] = (kr[None, :] >= rs[:, None]) & (kr[None, :] < rs[:, None] + NA_KH)
        dy_i[c] = np.clip(kr[None, :] - qr[:, None] + (NA_KH - 1), 0, 2 * NA_KH - 2)
    blocks = jnp.take(tx, jnp.asarray(dy_i.reshape(-1)), axis=1)
    blocks = blocks.reshape(h, 5, 2, NA_KROWS, GRID_W, GRID_W)
    ok = row_ok[:, :, :, None, None] & col_ok[None, None, None]
    tbl = jnp.where(ok[None], blocks * LOG2E, NEG_INF)
    tbl = tbl.transpose(1, 0, 2, 4, 3, 5).reshape(5, h, NA_QBLK, NA_KBLK)
    dead = jnp.full((1,) + tbl.shape[1:], NEG_INF, F32)
    return jnp.concatenate([tbl, dead], axis=0)


def _group(nblocks, target):
    return max(g for g in range(1, target + 1) if nblocks % g == 0)


def _softmax_dispatch(bound, call):
    sc = jnp.reshape(bound, (1,)).astype(F32)
    return lax.cond(bound <= ATTN_FAST_BOUND, lambda s: call(False)(s), lambda s: call(True)(s), sc)


def _na(bound, qa, ka, va, bias, lc):
    n_batch, nt, _ = qa.shape
    n = nt - lc
    nlb = n // NA_QBLK
    assert nlb >= 5 and lc % NA_QBLK == 0
    ncb = lc // NA_QBLK
    group = _group(nt // NA_QBLK, 11)

    def case(i):
        li = i - ncb
        return jnp.where(li < 0, 5, jnp.where(li < 2, li, jnp.where(li >= nlb - 2, li - (nlb - 5), 2)))

    kv = pl.BlockSpec((None, nt, LANE), lambda b, h, i: (b, 0, h))
    qo = pl.BlockSpec((None, group * NA_QBLK, LANE), lambda b, h, i: (b, i, h))
    bias_specs = [pl.BlockSpec((None, None, NA_QBLK, NA_KBLK),
                               functools.partial(lambda b, h, i, g: (case(i * group + g), h, 0, 0), g=g))
                  for g in range(group)]

    def call(online):
        kern = functools.partial(_na_kernel, lc=lc, nlb=nlb, group=group, online=online)
        return lambda sc: pl.pallas_call(
            kern,
            grid=(n_batch, NA_HEADS, nt // (group * NA_QBLK)),
            in_specs=[pl.BlockSpec(memory_space=pltpu.SMEM), qo, kv, kv] + bias_specs,
            out_specs=qo,
            out_shape=jax.ShapeDtypeStruct((n_batch, nt, NA_HEADS * HEAD_DIM), BF16),
            compiler_params=_params(("arbitrary",) * 3, 40),
            name="na_attn_online" if online else "na_attn_offset",
        )(sc, qa, ka, va, *([bias] * group))

    return _softmax_dispatch(bound, call)


SWA_BLK = 128
SWA_WIN = 3 * SWA_BLK


def _swa_kernel(sc_ref, q_ref, k_ref, v_ref, sink_ref, *rest, lc, n, group, online):
    mask_refs, o_ref = rest[:group], rest[group]
    i0 = pl.program_id(2) * group
    kc = k_ref[0:lc, :]
    vc = v_ref[0:lc, :]

    def window_start(c):
        li = i0 + c // SWA_GROUP - lc // SWA_BLK
        ws = jnp.clip((li - 1) * SWA_BLK, 0, n - SWA_WIN)
        return pl.multiple_of(lc + ws, SWA_BLK)

    def block(c):
        gq, g = divmod(c, SWA_GROUP)
        return slice(gq * SWA_BLK, (gq + 1) * SWA_BLK), slice(g * LANE, (g + 1) * LANE)

    def scores(c):
        rows, cols = block(c)
        q = q_ref[rows, cols]
        return _dot_nt(q, k_ref[pl.ds(window_start(c), SWA_WIN), :]), _dot_nt(q, kc)

    def numerators(c, s):
        s_w, s_c = s
        g = c % SWA_GROUP
        mask = mask_refs[c // SWA_GROUP][...]
        sink = sink_ref[g * SWA_BLK:(g + 1) * SWA_BLK, 0:1]
        if online:
            s_w = jnp.where(mask > 0.0, s_w, NEG_INF)
            m = jnp.maximum(jnp.maximum(jnp.max(s_w, axis=-1, keepdims=True),
                                        jnp.max(s_c, axis=-1, keepdims=True)), sink)
            p_w = jnp.exp2(s_w - m)
        else:
            m = sc_ref[0]
            p_w = jnp.exp2(s_w - m) * mask
        p_c = jnp.exp2(s_c - m)
        l = jnp.sum(p_w, axis=-1, keepdims=True) + jnp.sum(p_c, axis=-1, keepdims=True) + jnp.exp2(sink - m)
        return p_w.astype(BF16), p_c.astype(BF16), l

    def output(c, p):
        rows, cols = block(c)
        o = (_dot(p[0], v_ref[pl.ds(window_start(c), SWA_WIN), :]) + _dot(p[1], vc)) / p[2]
        o_ref[rows, cols] = o.astype(BF16)

    _skewed(group * SWA_GROUP, scores, numerators, output)


def _swa_masks():
    r = np.arange(SWA_BLK)[:, None]
    j = np.arange(SWA_WIN)[None, :]
    rel = [j - r, j - r - SWA_BLK, j - r - 2 * SWA_BLK]
    tabs = [(np.abs(d) <= SWA_WINDOW) for d in rel] + [np.zeros((SWA_BLK, SWA_WIN), bool)]
    return jnp.asarray(np.stack(tabs), F32)


def _swa(bound, qs, ks, vs, sink, lc):
    n_batch, nt, _ = qs.shape
    n = nt - lc
    nb = n // SWA_BLK
    assert nb >= 3 and lc % SWA_BLK == 0
    ncb = lc // SWA_BLK
    rows = SWA_GROUP * SWA_BLK
    group = _group(nt // SWA_BLK, 6)
    sink_rows = jnp.broadcast_to(
        jnp.repeat((sink.astype(F32) * LOG2E).reshape(SWA_KV_HEADS, SWA_GROUP), SWA_BLK, axis=1)[:, :, None],
        (SWA_KV_HEADS, rows, LANE))
    masks = _swa_masks()

    def case(i):
        li = i - ncb
        return jnp.where(li < 0, 3, jnp.where(li == 0, 0, jnp.where(li == nb - 1, 2, 1)))

    kv = pl.BlockSpec((None, nt, LANE), lambda b, k, i: (b, 0, k))
    qo = pl.BlockSpec((None, group * SWA_BLK, SWA_GROUP * LANE), lambda b, k, i: (b, i, k))
    mask_specs = [pl.BlockSpec((None, SWA_BLK, SWA_WIN),
                               functools.partial(lambda b, k, i, g: (case(i * group + g), 0, 0), g=g))
                  for g in range(group)]

    def call(online):
        kern = functools.partial(_swa_kernel, lc=lc, n=n, group=group, online=online)
        return lambda sc: pl.pallas_call(
            kern,
            grid=(n_batch, SWA_KV_HEADS, nt // (group * SWA_BLK)),
            in_specs=[pl.BlockSpec(memory_space=pltpu.SMEM), qo, kv, kv,
                      pl.BlockSpec((None, rows, LANE), lambda b, k, i: (k, 0, 0))] + mask_specs,
            out_specs=qo,
            out_shape=jax.ShapeDtypeStruct((n_batch, nt, SWA_Q_HEADS * HEAD_DIM), BF16),
            compiler_params=_params(("arbitrary",) * 3, 32),
            name="swa_attn_online" if online else "swa_attn_offset",
        )(sc, qs, ks, vs, sink_rows, *([masks] * group))

    return _softmax_dispatch(bound, call)


def _diff_finish(sc_ref, g_ref, o_ref, num, den, tq, post_scale):
    o2 = num / den
    o = o2[0:tq, :] - sc_ref[0] * o2[tq:2 * tq, :]
    o_ref[...] = (_rms_full(o, g_ref[6:7, :]) * post_scale).astype(BF16)


def _diff_online_kernel(sc_ref, q_ref, k_ref, v_ref, g_ref, o_ref, *, lc, nt, tq, tk, post_scale):
    qi = pl.program_id(2)
    q = q_ref[...].reshape(2 * tq, LANE)
    nk = jnp.where(qi < lc // tq, lc // tk, nt // tk)

    def body(kt, carry):
        m, l, acc = carry
        k0 = pl.multiple_of(kt * tk, tk)
        k = k_ref[pl.ds(k0, tk), :]
        v = v_ref[pl.ds(k0, tk), :]
        s = _dot_nt(q, k)
        m_new = jnp.maximum(m, jnp.max(s, axis=-1, keepdims=True))
        alpha = jnp.exp2(m - m_new)
        p = jnp.exp2(s - m_new)
        l = alpha * l + jnp.sum(p, axis=-1, keepdims=True)
        acc = alpha * acc + _dot(p.astype(BF16), v)
        return m_new, l, acc

    init = (jnp.full((2 * tq, 1), NEG_INF, F32), jnp.zeros((2 * tq, 1), F32), jnp.zeros((2 * tq, LANE), F32))
    _, l, acc = lax.fori_loop(0, nk, body, init)
    _diff_finish(sc_ref, g_ref, o_ref, acc, l, tq, post_scale)


def _diff_offset_kernel(sc_ref, q_ref, qn_ref, k_ref, v_ref, g_ref, o_ref, acc_ref, p_ref,
                        *, lc, nt, tq, tk, post_scale):
    qi = pl.program_id(2)
    rows2 = 2 * tq
    nk = nt // tk
    ncq = lc // tq
    q = q_ref[...].reshape(rows2, LANE)
    off = sc_ref[1]

    def probs(qq, k0, tkk):
        return jnp.exp2(_dot_nt(qq, k_ref[k0:k0 + tkk, :]) - off).astype(BF16)

    def next_tile_probs():
        return probs(qn_ref[...].reshape(rows2, LANE), 0, tk)

    @pl.when(qi < ncq)
    def _():
        acc_ref[...] = _dot(probs(q, 0, lc), v_ref[0:lc, :])

        @pl.when(qi == ncq - 1)
        def _():
            p_ref[0] = next_tile_probs()

    def latent(first_slot):
        tot = None
        for t in range(nk):
            pv = _dot(p_ref[(t + first_slot) % 2], v_ref[t * tk:(t + 1) * tk, :])
            p_ref[(t + 1 + first_slot) % 2] = probs(q, (t + 1) * tk, tk) if t + 1 < nk else next_tile_probs()
            tot = pv if tot is None else tot + pv
        acc_ref[...] = tot

    first_slot = ((qi - ncq) * nk) % 2
    for slot in range(2 if nk % 2 else 1):
        @pl.when((qi >= ncq) & (first_slot == slot))
        def _():
            latent(slot)

    acc = acc_ref[...]
    _diff_finish(sc_ref, g_ref, o_ref, acc[:, 0:LANE], acc[:, LANE:LANE + 1], tq, post_scale)


def _diff(sc, qd, kd, vde, pvec, lc, post_scale):
    n_batch, _, nt, _ = qd.shape
    tq = _tile(math.gcd(lc, nt), 256)
    assert lc % tq == 0
    grid = (n_batch, DIFF_HEADS, nt // tq)
    smem = pl.BlockSpec(memory_space=pltpu.SMEM)
    qspec = pl.BlockSpec((None, 2, tq, LANE), lambda b, h, i: (b, 0, i, h))
    kspec = pl.BlockSpec((None, nt, LANE), lambda b, h, i: (b, 0, h))
    gspec = pl.BlockSpec((8, LANE), lambda b, h, i: (0, 0))
    ospec = pl.BlockSpec((None, tq, LANE), lambda b, h, i: (b, i, h))
    oshape = jax.ShapeDtypeStruct((n_batch, nt, DIFF_HEADS * HEAD_DIM), BF16)

    def offset(sc, qd, kd, vde, pvec):
        tk = _tile(nt, 768)
        kern = functools.partial(_diff_offset_kernel, lc=lc, nt=nt, tq=tq, tk=tk, post_scale=post_scale)
        nq = nt // tq
        qnext = pl.BlockSpec((None, 2, tq, LANE), lambda b, h, i: (b, 0, jnp.minimum(i + 1, nq - 1), h))
        return pl.pallas_call(
            kern, grid=grid,
            in_specs=[smem, qspec, qnext, kspec, pl.BlockSpec((None, nt, 2 * LANE), lambda b, h, i: (b, 0, h)),
                      gspec],
            out_specs=ospec, out_shape=oshape,
            scratch_shapes=[pltpu.VMEM((2 * tq, 2 * LANE), F32), pltpu.VMEM((2, 2 * tq, tk), BF16)],
            compiler_params=_params(("arbitrary",) * 3, 40),
            name="diff_attn_offset",
        )(sc, qd, qd, kd, vde, pvec)

    def online(sc, qd, kd, vde, pvec):
        tk = _tile(math.gcd(lc, nt), 256)
        kern = functools.partial(_diff_online_kernel, lc=lc, nt=nt, tq=tq, tk=tk, post_scale=post_scale)
        return pl.pallas_call(
            kern, grid=grid,
            in_specs=[smem, qspec, kspec, pl.BlockSpec((None, nt, LANE), lambda b, h, i: (b, 0, 2 * h)), gspec],
            out_specs=ospec, out_shape=oshape,
            compiler_params=_params(("arbitrary",) * 3, 32),
            name="diff_attn_online",
        )(sc, qd, kd, vde, pvec)

    return lax.cond(sc[1] <= ATTN_FAST_BOUND, offset, online, sc, qd, kd, vde, pvec)


LRU_HALO = 8


LRU_STEP_BLOCKS = 4


def _lru_gate_kernel(xp_ref, x_ref, xn_ref, cw_ref, cb_ref, gw_ref, gb_ref, lam_ref,
                     af_ref, bf_ref, ab_ref, bb_ref, *, t, seg_starts, seg_ends):
    k = pl.program_id(2)
    is_start = functools.reduce(jnp.logical_or, [k == s for s in seg_starts])
    is_end = functools.reduce(jnp.logical_or, [k == s for s in seg_ends])
    h = LRU_HALO
    outs = ((af_ref, bf_ref), (ab_ref, bb_ref))
    sigmoid = lambda z: 0.5 + 0.5 * jnp.tanh(0.5 * z)
    for c in range(LRU_STEP_BLOCKS):
        cs = slice(c * LANE, (c + 1) * LANE)
        cur = x_ref[:, cs]
        prev = jnp.where(is_start, 0.0, xp_ref[:, cs])
        nxt = jnp.where(is_end, 0.0, xn_ref[:, cs])
        ext = jnp.concatenate([prev, cur, nxt], axis=0)
        xl = (cb_ref[:, cs] + cw_ref[0:1, cs] * ext[h - 2:h - 2 + t, :] + cw_ref[1:2, cs] * ext[h - 1:h - 1 + t, :]
              + cw_ref[2:3, cs] * cur + cw_ref[3:4, cs] * ext[h + 1:h + 1 + t, :])
        xb = xl.astype(BF16)
        for d in range(2):
            r = sigmoid(_dot(xb, gw_ref[d, 0, c].astype(BF16)) + gb_ref[2 * d:2 * d + 1, cs])
            ig = sigmoid(_dot(xb, gw_ref[d, 1, c].astype(BF16)) + gb_ref[2 * d + 1:2 * d + 2, cs])
            a = jnp.exp(LRU_C * r * jax.nn.log_sigmoid(lam_ref[d:d + 1, cs]))
            outs[d][0][:, cs] = a
            outs[d][1][:, cs] = jnp.sqrt(1.0 - a * a) * (ig * xl)


def _lru_gates(u, conv_w, conv_b, gate_w, gate_b, lam, lc):
    n_batch, nt, _ = u.shape
    t = _tile(math.gcd(lc, nt), 256)
    nk = nt // t
    hb = t // LRU_HALO
    seg_starts = (0, lc // t)
    seg_ends = (lc // t - 1, nk - 1)
    w = LRU_STEP_BLOCKS * LANE
    assert OFF_XR % w == 0 and LRU_WIDTH % w == 0
    cb0 = OFF_XR // w
    kern = functools.partial(_lru_gate_kernel, t=t, seg_starts=seg_starts, seg_ends=seg_ends)
    out = pl.BlockSpec((None, t, w), lambda b, c, k: (b, k, c))
    osd = jax.ShapeDtypeStruct((n_batch, nt, LRU_WIDTH), F32)
    return pl.pallas_call(
        kern,
        grid=(n_batch, LRU_WIDTH // w, nk),
        in_specs=[pl.BlockSpec((None, LRU_HALO, w), lambda b, c, k: (b, jnp.maximum(k * hb - 1, 0), cb0 + c)),
                  pl.BlockSpec((None, t, w), lambda b, c, k: (b, k, cb0 + c)),
                  pl.BlockSpec((None, LRU_HALO, w),
                               lambda b, c, k: (b, jnp.minimum((k + 1) * hb, nt // LRU_HALO - 1), cb0 + c)),
                  pl.BlockSpec((4, w), lambda b, c, k: (0, c)),
                  pl.BlockSpec((1, w), lambda b, c, k: (0, c)),
                  pl.BlockSpec((2, 2, LRU_STEP_BLOCKS, LANE, LANE), lambda b, c, k: (0, 0, c, 0, 0)),
                  pl.BlockSpec((4, w), lambda b, c, k: (0, c)),
                  pl.BlockSpec((2, w), lambda b, c, k: (0, c))],
        out_specs=[out, out, out, out],
        out_shape=[osd, osd, osd, osd],
        compiler_params=_params(("arbitrary",) * 3, 24),
        name="lru_gates",
    )(u, u, u, conv_w, conv_b.reshape(1, LRU_WIDTH), gate_w, gate_b.reshape(4, LRU_WIDTH), lam)


def _lru_scan_kernel(af_ref, bf_ref, ab_ref, bb_ref, hf_ref, hb_ref, cf_ref, cbk_ref, *, t):
    @pl.when(pl.program_id(1) == 0)
    def _():
        cf_ref[...] = jnp.zeros_like(cf_ref)
        cbk_ref[...] = jnp.zeros_like(cbk_ref)

    ng = t // 8

    def body(g, carry):
        hf, hb = carry
        r0 = pl.multiple_of(g * 8, 8)
        a = af_ref[pl.ds(r0, 8), :]
        bt = bf_ref[pl.ds(r0, 8), :]
        rows = []
        for r in range(8):
            hf = a[r:r + 1, :] * hf + bt[r:r + 1, :]
            rows.append(hf)
        hf_ref[pl.ds(r0, 8), :] = jnp.concatenate(rows, axis=0)
        r1 = pl.multiple_of((ng - 1 - g) * 8, 8)
        a = ab_ref[pl.ds(r1, 8), :]
        bt = bb_ref[pl.ds(r1, 8), :]
        rows = []
        for r in range(7, -1, -1):
            hb = a[r:r + 1, :] * hb + bt[r:r + 1, :]
            rows.append(hb)
        hb_ref[pl.ds(r1, 8), :] = jnp.concatenate(rows[::-1], axis=0)
        return hf, hb

    hf, hb = lax.fori_loop(0, ng, body, (cf_ref[...], cbk_ref[...]))
    cf_ref[...] = hf
    cbk_ref[...] = hb


def _lru_scan(af, bf, ab, bb, lc):
    n_batch, nt, w = af.shape
    t = lc
    assert nt % t == 0
    nk = nt // t
    kern = functools.partial(_lru_scan_kernel, t=t)
    fwd = pl.BlockSpec((None, t, w), lambda b, k: (b, k, 0))
    bwd = pl.BlockSpec((None, t, w), lambda b, k: (b, jnp.where(k == 0, 0, nk - k), 0))
    osd = jax.ShapeDtypeStruct((n_batch, nt, w), F32)
    return pl.pallas_call(
        kern,
        grid=(n_batch, nk),
        in_specs=[fwd, fwd, bwd, bwd],
        out_specs=[fwd, bwd],
        out_shape=[osd, osd],
        scratch_shapes=[pltpu.VMEM((1, w), F32), pltpu.VMEM((1, w), F32)],
        compiler_params=_params(("arbitrary", "arbitrary"), 24),
        name="lru_scan",
    )(af, bf, ab, bb)


def _outproj_kernel(oa_ref, od_ref, os_ref, hf_ref, hb_ref, g0_ref, g1_ref, gate_ref, w_ref, x_ref,
                    o_ref, lhs_ref, *, n_batch, lc, tm):
    b = pl.program_id(0)
    i = pl.program_id(1)

    @pl.when(pl.program_id(2) == 0)
    def _():
        def chunk(r0):
            rs = pl.ds(r0, ROW_CHUNK)
            lhs_ref[rs, 0:1024] = oa_ref[rs, :]
            lhs_ref[rs, 1024:2048] = od_ref[rs, :]
            lhs_ref[rs, 2048:3072] = os_ref[rs, :]
            for c, g_ref in enumerate((g0_ref, g1_ref)):
                cs = slice(c * 512, (c + 1) * 512)
                y = (hf_ref[rs, cs] + hb_ref[rs, cs]) * jax.nn.gelu(g_ref[rs, :])
                lhs_ref[rs, 3072 + c * 512:3584 + c * 512] = y.astype(BF16)

        _for_row_chunks(tm, chunk)

    row = i * tm + lax.broadcasted_iota(I32, (tm, 1), 0)
    gate = jnp.where(row < lc, gate_ref[pl.ds(n_batch, 1), :], gate_ref[pl.ds(b, 1), :])
    o_ref[...] = x_ref[...] + gate * _dot(lhs_ref[...], w_ref[...])


def _outproj(o_a, o_d, o_s, hf, hb, u, modl, w_bf, layer, xs, lc):
    n_batch, nt, d = xs.shape
    tm = _tile(nt, 768)
    tn = _tile(d, 512)
    ntile = d // tn
    kern = functools.partial(_outproj_kernel, n_batch=n_batch, lc=lc, tm=tm)
    mix = pl.BlockSpec((None, tm, 1024), lambda b, i, j: (b, i, 0))
    gr = lambda c: pl.BlockSpec((None, tm, 512), lambda b, i, j: (b, i, OFF_GR // 512 + c))
    xblk = pl.BlockSpec((None, tm, tn), lambda b, i, j: (b, i, j))
    return pl.pallas_call(
        kern,
        grid=(n_batch, nt // tm, ntile),
        in_specs=[mix, mix, mix, mix, mix, gr(0), gr(1),
                  pl.BlockSpec((MOD_ROWS, tn), lambda b, i, j: (0, 2 * ntile + j)),
                  pl.BlockSpec((None, D_MIX, tn), lambda b, i, j: (layer, 0, j)), xblk],
        out_specs=xblk,
        out_shape=jax.ShapeDtypeStruct(xs.shape, F32),
        scratch_shapes=[pltpu.VMEM((tm, D_MIX), BF16)],
        input_output_aliases={9: 0},
        compiler_params=_params(("arbitrary",) * 3, 56),
        name="out_proj",
    )(o_a, o_d, o_s, hf, hb, u, u, modl, w_bf, xs)


def _split2(x):
    hi = x.astype(BF16)
    return hi, (x - hi.astype(F32)).astype(BF16)


ROUTER_KC = 512


HI16 = -65536


def _pack_bf16_pair(lo, hi):
    bits = lambda v: lax.bitcast_convert_type(v.astype(BF16).astype(F32), I32)
    return (bits(hi) & HI16) | lax.shift_right_logical(bits(lo), 16)


def _unpack_bf16_pair(w):
    lo = lax.bitcast_convert_type(lax.shift_left(w, 16), F32)
    hi = lax.bitcast_convert_type(w & HI16, F32)
    return lo.astype(BF16), hi.astype(BF16)


def _router_kernel(x_ref, g_ref, sh_ref, sc_ref, w0_ref, w1_ref, xm_ref, aff_ref, xf_ref, *, n_batch, lc, tm):
    b = pl.program_id(0)
    i = pl.program_id(1)
    d = x_ref.shape[1]
    half = d // 2

    def chunk(r0):
        rs = pl.ds(r0, ROW_CHUNK)
        sh = _mod_row(sh_ref, i * tm + r0, b, n_batch, lc)
        sc = _mod_row(sc_ref, i * tm + r0, b, n_batch, lc)
        xm = _norm_modulate(x_ref[rs, :], g_ref[...], sh, sc)
        xf_ref[rs, :] = xm
        xm_ref[rs, 0:half] = _pack_bf16_pair(xm[:, 0:half], xm[:, half:d])

    _for_row_chunks(tm, chunk)

    logits = jnp.zeros((tm, LANE), F32)
    tk = _tile(d, ROUTER_KC)
    for kc in range(d // tk):
        ks = slice(kc * tk, (kc + 1) * tk)
        x0, x1 = _split2(xf_ref[:, ks])
        logits = logits + (_dot(x0, w0_ref[ks, :]) + (_dot(x0, w1_ref[ks, :]) + _dot(x1, w0_ref[ks, :])))
    lane = lax.broadcasted_iota(I32, (1, LANE), 1)
    logits = jnp.where(lane < N_EXPERTS, logits, NEG_INF)
    m = jnp.max(logits, axis=-1, keepdims=True)
    p = jnp.exp(logits - m)
    aff = p / jnp.sum(p, axis=-1, keepdims=True)
    aff_ref[...] = aff
    xm_ref[:, half:half + LANE] = lax.bitcast_convert_type(aff, I32)


def _router(xs, g, modl, wr_pad, lc):
    n_batch, nt, d = xs.shape
    tm = _tile(nt, 256)
    assert lc % ROW_CHUNK == 0
    kern = functools.partial(_router_kernel, n_batch=n_batch, lc=lc, tm=tm)
    wspec = pl.BlockSpec((d, LANE), lambda b, i: (0, 0))
    return pl.pallas_call(
        kern,
        grid=(n_batch, nt // tm),
        in_specs=[pl.BlockSpec((None, tm, d), lambda b, i: (b, i, 0)),
                  pl.BlockSpec((1, d), lambda b, i: (0, 0)),
                  pl.BlockSpec((MOD_ROWS, d), lambda b, i: (0, 3)),
                  pl.BlockSpec((MOD_ROWS, d), lambda b, i: (0, 4)),
                  wspec, wspec],
        out_specs=[pl.BlockSpec((None, tm, d // 2 + LANE), lambda b, i: (b, i, 0)),
                   pl.BlockSpec((None, tm, LANE), lambda b, i: (b, i, 0))],
        out_shape=[jax.ShapeDtypeStruct((n_batch, nt, d // 2 + LANE), I32),
                   jax.ShapeDtypeStruct((n_batch, nt, LANE), F32)],
        scratch_shapes=[pltpu.VMEM((tm, d), F32)],
        compiler_params=_params(("arbitrary", "arbitrary"), 48),
        name="norm2_router",
    )(xs, g, modl, modl, *_split2(wr_pad))


def _select_kernel(aff_ref, l_ref, cnt_ref, off_ref, bits_ref, *, row0, n, cap):
    nc = n // LANE
    bits_ref[...] = lax.bitcast_convert_type(aff_ref[row0:row0 + n, :], I32)

    def bit_step(i, thr):
        cand = thr | (jnp.int32(1) << (30 - i))
        cnt = jnp.sum((bits_ref[...] >= cand).astype(I32), axis=0, keepdims=True)
        return jnp.where(cnt >= cap, cand, thr)

    thr = lax.fori_loop(0, 31, bit_step, jnp.zeros((1, LANE), I32))
    n_gt = jnp.sum((bits_ref[...] > thr).astype(I32), axis=0, keepdims=True)
    need = (cap - n_gt).astype(F32)

    r_io = lax.broadcasted_iota(I32, (LANE, LANE), 0)
    c_io = lax.broadcasted_iota(I32, (LANE, LANE), 1)
    ltri = (c_io <= r_io).astype(BF16)
    t_col = r_io.astype(F32)
    j_row = c_io.astype(F32)

    def chunk(c, carry):
        run_eq, run_sel = carry
        r0 = pl.multiple_of(c * LANE, LANE)
        bc = bits_ref[pl.ds(r0, LANE), :]
        eq = bc == thr
        eq_f = eq.astype(F32)
        incl_eq = _dot(ltri, eq_f.astype(BF16))
        rank = run_eq + incl_eq - eq_f
        sel = (bc > thr) | (eq & (rank < need))
        sel_f = sel.astype(F32)
        incl = _dot(ltri, sel_f.astype(BF16))
        cnt_c = incl[LANE - 1:LANE, :]
        qm = jnp.where(sel, incl - sel_f, -1.0)
        for e in range(N_EXPERTS):
            pos = jnp.broadcast_to(qm[:, e:e + 1], (LANE, LANE))
            local = jnp.sum(jnp.where(pos == j_row, t_col, 0.0), axis=0, keepdims=True)
            l_ref[e, pl.ds(c, 1), :] = local.astype(I32)
        cnt_ref[pl.ds(c, 1), :] = cnt_c.astype(I32)
        off_ref[pl.ds(c, 1), :] = run_sel.astype(I32)
        return run_eq + incl_eq[LANE - 1:LANE, :], run_sel + cnt_c

    zero = jnp.zeros((1, LANE), F32)
    lax.fori_loop(0, nc, chunk, (zero, zero))


def _select(affp, row0, n, cap):
    n_batch, nt, _ = affp.shape
    nc = n // LANE
    kern = functools.partial(_select_kernel, row0=row0, n=n, cap=cap)
    return pl.pallas_call(
        kern,
        grid=(n_batch,),
        in_specs=[pl.BlockSpec((None, nt, LANE), lambda b: (b, 0, 0))],
        out_specs=[pl.BlockSpec((None, N_EXPERTS, nc, LANE), lambda b: (b, 0, 0, 0)),
                   pl.BlockSpec((None, nc, LANE), lambda b: (b, 0, 0)),
                   pl.BlockSpec((None, nc, LANE), lambda b: (b, 0, 0))],
        out_shape=[jax.ShapeDtypeStruct((n_batch, N_EXPERTS, nc, LANE), I32),
                   jax.ShapeDtypeStruct((n_batch, nc, LANE), I32),
                   jax.ShapeDtypeStruct((n_batch, nc, LANE), I32)],
        scratch_shapes=[pltpu.VMEM((n, LANE), I32)],
        compiler_params=_params(("arbitrary",), 40),
        name="ec_select",
    )(affp)


MOE_ROWS = 256


def _moe_kernel(l_hbm, cnt_hbm, off_hbm, xm_hbm, xs_in_hbm, g2_ref, w1_ref, w3_ref, w2_ref, xs_hbm,
                l_s, cnt_s, off_s, idx_s, xg, rows, xb, hdn_s, sem, *, row0, nc, cap, g_row):
    del xs_in_hbm
    b = pl.program_id(0)
    e = pl.program_id(1)
    step = b * pl.num_programs(1) + e
    last_step = pl.num_programs(0) * pl.num_programs(1) - 1
    tables = (pltpu.make_async_copy(l_hbm.at[b, e], l_s, sem.at[3]),
              pltpu.make_async_copy(cnt_hbm.at[b], cnt_s, sem.at[4]),
              pltpu.make_async_copy(off_hbm.at[b], off_s, sem.at[5]))
    for cp in tables:
        cp.start()
    for cp in tables:
        cp.wait()

    def x_gather(tok, slot, n_rows=1):
        return pltpu.make_async_copy(xm_hbm.at[b, pl.ds(tok, n_rows), :], xg.at[pl.ds(slot, n_rows), :], sem.at[0])

    def row_gather(tok, slot, n_rows=1):
        return pltpu.make_async_copy(xs_hbm.at[b, pl.ds(tok, n_rows), :], rows.at[pl.ds(slot, n_rows), :], sem.at[1])

    def row_scatter(tok, slot, n_rows=1):
        return pltpu.make_async_copy(rows.at[pl.ds(slot, n_rows), :], xs_hbm.at[b, pl.ds(tok, n_rows), :], sem.at[2])

    def chunk(c, _):
        base = row0 + c * LANE
        off_c = off_s[c, e]

        def one(j, _):
            idx_s[off_c + j] = base + l_s[c, j]
            return 0

        lax.fori_loop(0, cnt_s[c, e], one, 0)
        return 0

    lax.fori_loop(0, nc, chunk, 0)

    def start_all(copy):
        def one(s, _):
            copy(idx_s[s], s).start()
            return 0

        lax.fori_loop(0, cap, one, 0, unroll=8)

    start_all(x_gather)

    @pl.when(step > 0)
    def _():
        row_scatter(row0, 0, cap).wait()

    start_all(row_gather)
    x_gather(row0, 0, cap).wait()

    d = rows.shape[1]
    tn = _tile(d, 512)
    rc = min(cap, MOE_ROWS)
    lane = lax.broadcasted_iota(I32, (1, LANE), 1)
    g2_row = g_row if g_row is not None else b

    half = d // 2
    th = _tile(half, 512)

    def hidden(r, _):
        rs = pl.ds(pl.multiple_of(r * rc, rc), rc)
        for kc in range(half // th):
            ks = slice(kc * th, (kc + 1) * th)
            lo, hi = _unpack_bf16_pair(xg[rs, ks])
            xb[:, ks] = lo
            xb[:, half + kc * th:half + (kc + 1) * th] = hi
        h1 = _dot(xb[...], w1_ref[...])
        h3 = _dot(xb[...], w3_ref[...])
        aff = lax.bitcast_convert_type(xg[rs, half:half + LANE], F32)
        gate = jnp.sum(jnp.where(lane == e, aff, 0.0), axis=-1, keepdims=True)
        hdn_s[rs, :] = (h1 * jax.nn.sigmoid(h1) * h3 * gate).astype(BF16)
        return 0

    lax.fori_loop(0, cap // rc, hidden, 0)
    row_gather(row0, 0, cap).wait()

    def down(r, _):
        rs = pl.ds(pl.multiple_of(r * rc, rc), rc)
        for c in range(d // tn):
            sl = slice(c * tn, (c + 1) * tn)
            rows[rs, sl] = rows[rs, sl] + g2_ref[pl.ds(g2_row, 1), sl] * _dot(hdn_s[rs, :], w2_ref[:, sl])
        return 0

    lax.fori_loop(0, cap // rc, down, 0)
    start_all(row_scatter)

    @pl.when(step == last_step)
    def _():
        row_scatter(row0, 0, cap).wait()


def _moe(ltab, cnt, off, xma, xs, modl, w1, w3, w2, layer, row0, cap, g_row):
    n_batch, nt, d = xs.shape
    nc = ltab.shape[2]
    nd = 1
    rc = min(cap, MOE_ROWS)
    assert cap % rc == 0
    kern = functools.partial(_moe_kernel, row0=row0, nc=nc, cap=cap, g_row=g_row)
    any_ = pl.BlockSpec(memory_space=pl.ANY)
    return pl.pallas_call(
        kern,
        grid=(n_batch, N_EXPERTS),
        in_specs=[any_, any_, any_, any_, any_,
                  pl.BlockSpec((MOD_ROWS, d), lambda b, e: (0, 5 * nd)),
                  pl.BlockSpec((None, None, d, EXPERT_FF), lambda b, e: (layer, e, 0, 0)),
                  pl.BlockSpec((None, None, d, EXPERT_FF), lambda b, e: (layer, e, 0, 0)),
                  pl.BlockSpec((None, None, EXPERT_FF, d), lambda b, e: (layer, e, 0, 0))],
        out_specs=any_,
        out_shape=jax.ShapeDtypeStruct(xs.shape, F32),
        scratch_shapes=[pltpu.SMEM((nc, LANE), I32), pltpu.SMEM((nc, LANE), I32), pltpu.SMEM((nc, LANE), I32),
                        pltpu.SMEM((cap,), I32), pltpu.VMEM((cap, d // 2 + LANE), I32), pltpu.VMEM((cap, d), F32),
                        pltpu.VMEM((rc, d), BF16), pltpu.VMEM((cap, EXPERT_FF), BF16),
                        pltpu.SemaphoreType.DMA((6,))],
        input_output_aliases={4: 0},
        compiler_params=_params(("arbitrary", "arbitrary"), 56),
        name="moe_ffn",
    )(ltab, cnt, off, xma, xs, modl, w1, w3, w2)


def _rope_tables(n, lc):
    t = jnp.arange(n)
    pos = jnp.stack([t // GRID_W, t % GRID_W], axis=-1).astype(F32)

    def tables(dh):
        quarter = dh // 4
        inv_freq = ROPE_THETA ** (-jnp.arange(quarter, dtype=F32) / quarter)
        ang = pos[:, :, None] * inv_freq
        cos, sin = jnp.cos(ang), jnp.sin(ang)
        c = jnp.concatenate([cos[:, 0], cos[:, 0], cos[:, 1], cos[:, 1]], axis=-1)
        s = jnp.concatenate([-sin[:, 0], sin[:, 0], -sin[:, 1], sin[:, 1]], axis=-1)
        reps = LANE // dh
        c, s = jnp.tile(c, (1, reps)), jnp.tile(s, (1, reps))
        return (jnp.concatenate([jnp.ones((lc, LANE), F32), c], axis=0),
                jnp.concatenate([jnp.zeros((lc, LANE), F32), s], axis=0))

    return tables(HEAD_DIM) + tables(DIFF_QK_DIM)


def kernel(x, c, ctx, c_ctx, w_ada, b_ada, norm_g, w_in, w_out, qk_g_na, na_rpb, qk_g_diff, diff_lambda,
           diff_subln_g, qk_g_swa, swa_sink, lru_conv_w, lru_conv_b, lru_gate_w, lru_gate_b, lru_lam,
           w_router, w1, w3, w2):
    n_batch, n, d = x.shape
    lc = ctx.shape[1]
    depth = w_ada.shape[0]
    assert n_batch < MOD_ROWS and w_in.shape[2] == D_IN

    cvec = jnp.zeros((MOD_ROWS, d), F32).at[:n_batch].set(c).at[n_batch].set(c_ctx)
    mod = _ada(cvec, w_ada, b_ada)
    cs, ss, cd, sd = _rope_tables(n, lc)
    xs = jnp.concatenate([ctx, x], axis=1)

    cap_lat = CAPACITY_FACTOR * n // N_EXPERTS
    cap_ctx = CAPACITY_FACTOR * lc // N_EXPERTS
    w_in_b, w_out_b = w_in.astype(BF16), w_out.astype(BF16)
    w1b, w3b, w2b = w1.astype(BF16), w3.astype(BF16), w2.astype(BF16)

    for l in range(depth):
        modl = mod[l]
        u = _inproj(xs, norm_g[l, 0].reshape(1, d), modl, w_in_b, l, lc)

        pvec = jnp.concatenate([qk_g_na[l], jnp.tile(qk_g_diff[l], (1, 2)), qk_g_swa[l],
                                diff_subln_g[l].reshape(1, LANE), jnp.zeros((1, LANE), F32)], axis=0)
        qa, ka, va, qd, kd, vd, qs, ks, vs = _prep(u, pvec, cs, ss, cd, sd)

        gmax = lambda g: jnp.max(jnp.abs(g[0])) * jnp.max(jnp.abs(g[1]))
        b_na = (BOUND_SLACK * HEAD_DIM * ATTN_Q_SCALE * gmax(qk_g_na[l])
                + LOG2E * jnp.max(jnp.abs(na_rpb[l])))
        b_swa = jnp.maximum(BOUND_SLACK * HEAD_DIM * ATTN_Q_SCALE * gmax(qk_g_swa[l]),
                            LOG2E * jnp.max(jnp.abs(swa_sink[l])))
        b_diff = BOUND_SLACK * DIFF_QK_DIM * DIFF_Q_SCALE * gmax(qk_g_diff[l])

        o_a = _na(b_na, qa, ka, va, _na_bias_tables(na_rpb[l], n), lc)

        lambda_init = 0.8 - 0.6 * math.exp(-0.3 * l)
        dl = diff_lambda[l].astype(F32)
        lam = jnp.exp(jnp.sum(dl[0] * dl[1])) - jnp.exp(jnp.sum(dl[2] * dl[3])) + lambda_init
        o_d = _diff(jnp.stack([lam, b_diff]).astype(F32), qd, kd, vd, pvec, lc, 1.0 - lambda_init)

        o_s = _swa(b_swa, qs, ks, vs, swa_sink[l], lc)

        af, bf, ab, bb = _lru_gates(u, lru_conv_w[l], lru_conv_b[l], lru_gate_w[l], lru_gate_b[l], lru_lam[l], lc)
        hf, hb = _lru_scan(af, bf, ab, bb, lc)

        xs = _outproj(o_a, o_d, o_s, hf, hb, u, modl, w_out_b, l, xs, lc)

        wr_pad = jnp.zeros((d, LANE), F32).at[:, :N_EXPERTS].set(w_router[l])
        xma, affp = _router(xs, norm_g[l, 1].reshape(1, d), modl, wr_pad, lc)
        lt, cnt, off = _select(affp, lc, n, cap_lat)
        xs = _moe(lt, cnt, off, xma, xs, modl, w1b, w3b, w2b, l, lc, cap_lat, None)
        if l < depth - 1:
            lt, cnt, off = _select(affp, 0, lc, cap_ctx)
            xs = _moe(lt, cnt, off, xma, xs, modl, w1b, w3b, w2b, l, 0, cap_ctx, n_batch)
    return xs[:, lc:, :]
```

```python
import functools
import math

import numpy as np
import jax
import jax.numpy as jnp
from jax import lax
from jax.experimental import pallas as pl
from jax.experimental.pallas import tpu as pltpu

GRID_W = 64
HEAD_DIM = 128
NA_HEADS = 8
NA_KH = 8
NA_KW = 16
DIFF_HEADS = 8
DIFF_QK_DIM = 64
SWA_Q_HEADS = 8
SWA_KV_HEADS = 2
SWA_GROUP = SWA_Q_HEADS // SWA_KV_HEADS
SWA_WINDOW = 128
LRU_WIDTH = 1024
LRU_BLOCKS = 8
LRU_C = 8.0
N_EXPERTS = 16
EXPERT_FF = 256
CAPACITY_FACTOR = 2
ROPE_THETA = 10000.0
NEG_INF = -1e30
EPS = 1e-6

OFF_QA, OFF_KA, OFF_VA = 0, 1024, 2048
OFF_QD, OFF_KD, OFF_VD = 3072, 4096, 5120
OFF_QS, OFF_KS, OFF_VS = 6144, 7168, 7424
OFF_XR, OFF_GR = 7680, 8704
D_ATT = 7680
D_IN = 9728
D_MIX = 4096

LOG2E = 1.4426950408889634
DIFF_Q_SCALE = DIFF_QK_DIM ** -0.5 * LOG2E
ATTN_FAST_BOUND = 60.0
ATTN_Q_SCALE = HEAD_DIM ** -0.5 * LOG2E
BOUND_SLACK = 1.02

LANE = 128
MOD_ROWS = 8
V7X_VMEM_BYTES = 64 * 1024 * 1024

F32 = jnp.float32
BF16 = jnp.bfloat16
I32 = jnp.int32


def _tile(n, target, mult=LANE):
    best = None
    for t in range(mult, min(n, target) + 1, mult):
        if n % t == 0:
            best = t
    assert best is not None, (n, target, mult)
    return best


def _params(sem, vmem_mb):
    return pltpu.CompilerParams(dimension_semantics=sem, vmem_limit_bytes=min(vmem_mb, 60) * 1024 * 1024)


def _dot_nt(a, b):
    return lax.dot_general(a, b, (((1,), (1,)), ((), ())), preferred_element_type=F32)


def _dot(a, b):
    return jnp.dot(a, b, preferred_element_type=F32)


def _ada_kernel(c_ref, w_ref, b_ref, o_ref):
    cv = c_ref[...]
    s = cv * jax.nn.sigmoid(cv)
    s_hi = s.astype(BF16)
    s_lo = (s - s_hi.astype(F32)).astype(BF16)
    w = w_ref[...].astype(BF16)
    o_ref[...] = _dot(s_hi, w) + _dot(s_lo, w) + b_ref[...]


def _ada(cvec, w_ada, b_ada):
    depth, d, n6 = w_ada.shape
    tn = _tile(n6, 1024)
    return pl.pallas_call(
        _ada_kernel,
        grid=(depth, n6 // tn),
        in_specs=[pl.BlockSpec((MOD_ROWS, d), lambda l, j: (0, 0)),
                  pl.BlockSpec((None, d, tn), lambda l, j: (l, 0, j)),
                  pl.BlockSpec((None, 1, tn), lambda l, j: (l, 0, j))],
        out_specs=pl.BlockSpec((None, MOD_ROWS, tn), lambda l, j: (l, 0, j)),
        out_shape=jax.ShapeDtypeStruct((depth, MOD_ROWS, n6), F32),
        compiler_params=_params(("arbitrary", "arbitrary"), 56),
        name="ada_mod",
    )(cvec, w_ada, b_ada.reshape(depth, 1, n6))


ROW_CHUNK = 32


def _mod_row(ref, row0, b, n_batch, lc):
    return jnp.where(row0 < lc, ref[pl.ds(n_batch, 1), :], ref[pl.ds(b, 1), :])


def _norm_modulate(x, g, sh, sc):
    ms = jnp.mean(x * x, axis=-1, keepdims=True)
    return x * lax.rsqrt(ms + EPS) * g * (1.0 + sc) + sh


def _for_row_chunks(tm, fn):
    assert tm % ROW_CHUNK == 0

    def body(c, _):
        fn(pl.multiple_of(c * ROW_CHUNK, ROW_CHUNK))
        return 0

    lax.fori_loop(0, tm // ROW_CHUNK, body, 0)


def _inproj_kernel(x_ref, g_ref, sh_ref, sc_ref, w_ref, o_ref, xn_ref, *, n_batch, lc, tm):
    b = pl.program_id(0)
    i = pl.program_id(1)

    @pl.when(pl.program_id(2) == 0)
    def _():
        def chunk(r0):
            sh = _mod_row(sh_ref, i * tm + r0, b, n_batch, lc)
            sc = _mod_row(sc_ref, i * tm + r0, b, n_batch, lc)
            xn = _norm_modulate(x_ref[pl.ds(r0, ROW_CHUNK), :], g_ref[...], sh, sc)
            xn_ref[pl.ds(r0, ROW_CHUNK), :] = xn.astype(BF16)

        _for_row_chunks(tm, chunk)

    o_ref[...] = _dot(xn_ref[...], w_ref[...])


def _inproj(xs, g, modl, w_bf, layer, lc):
    n_batch, nt, d = xs.shape
    d_in = w_bf.shape[2]
    tm = _tile(nt, 768)
    tn = _tile(d_in, 512)
    assert lc % ROW_CHUNK == 0
    kern = functools.partial(_inproj_kernel, n_batch=n_batch, lc=lc, tm=tm)
    return pl.pallas_call(
        kern,
        grid=(n_batch, nt // tm, d_in // tn),
        in_specs=[pl.BlockSpec((None, tm, d), lambda b, i, j: (b, i, 0)),
                  pl.BlockSpec((1, d), lambda b, i, j: (0, 0)),
                  pl.BlockSpec((MOD_ROWS, d), lambda b, i, j: (0, 0)),
                  pl.BlockSpec((MOD_ROWS, d), lambda b, i, j: (0, 1)),
                  pl.BlockSpec((None, d, tn), lambda b, i, j: (layer, 0, j))],
        out_specs=pl.BlockSpec((None, tm, tn), lambda b, i, j: (b, i, j)),
        out_shape=jax.ShapeDtypeStruct((n_batch, nt, d_in), F32),
        scratch_shapes=[pltpu.VMEM((tm, d), BF16)],
        compiler_params=_params(("arbitrary", "arbitrary", "arbitrary"), 56),
        name="in_proj",
    )(xs, g, modl, modl, w_bf)


def _rms_full(x, g):
    ms = jnp.mean(x * x, axis=-1, keepdims=True)
    return x * lax.rsqrt(ms + EPS) * g


def _group_ones(width):
    r = lax.broadcasted_iota(I32, (LANE, LANE), 0) // width
    c = lax.broadcasted_iota(I32, (LANE, LANE), 1) // width
    return jnp.where(r == c, 1.0, 0.0).astype(BF16)


def _rms_groups(x, g, ones, width):
    x2 = x * x
    hi = x2.astype(BF16)
    lo = (x2 - hi.astype(F32)).astype(BF16)
    ms = (_dot(hi, ones) + _dot(lo, ones)) * (1.0 / width)
    return x * lax.rsqrt(ms + EPS) * g


def _rope(x, cos, sin_signed, half):
    lane = lax.broadcasted_iota(I32, (1, LANE), 1)
    up = pltpu.roll(x, LANE - half, axis=1)
    dn = pltpu.roll(x, half, axis=1)
    partner = jnp.where((lane & half) == 0, up, dn)
    return x * cos + partner * sin_signed


def _prep_kernel(u_ref, p_ref, cs_ref, ss_ref, cd_ref, sd_ref,
                 qa_ref, ka_ref, va_ref, qd_ref, kd_ref, vd_ref, qs_ref, ks_ref, vs_ref):
    g_na_q, g_na_k = p_ref[0:1, :], p_ref[1:2, :]
    g_d_q, g_d_k = p_ref[2:3, :], p_ref[3:4, :]
    g_s_q, g_s_k = p_ref[4:5, :], p_ref[5:6, :]
    cs, ss, cd, sd = cs_ref[...], ss_ref[...], cd_ref[...], sd_ref[...]
    lo = lax.broadcasted_iota(I32, (1, LANE), 1) < DIFF_QK_DIM

    def col(off, h):
        return u_ref[:, off + h * LANE: off + (h + 1) * LANE]

    def put(ref, h, val):
        ref[:, h * LANE:(h + 1) * LANE] = val.astype(BF16)

    ones_head = _group_ones(HEAD_DIM)
    ones_half = _group_ones(DIFF_QK_DIM)
    rms_head = lambda x, g: _rms_groups(x, g, ones_head, HEAD_DIM)
    rms_half = lambda x, g: _rms_groups(x, g, ones_half, DIFF_QK_DIM)

    for h in range(NA_HEADS):
        put(qa_ref, h, rms_head(col(OFF_QA, h), g_na_q) * ATTN_Q_SCALE)
        put(ka_ref, h, rms_head(col(OFF_KA, h), g_na_k))
        put(va_ref, h, col(OFF_VA, h))
    ones_col = jnp.where(lax.broadcasted_iota(I32, (u_ref.shape[0], LANE), 1) == 0, 1.0, 0.0).astype(BF16)
    for h in range(DIFF_HEADS):
        q = _rope(rms_half(col(OFF_QD, h), g_d_q), cd, sd, DIFF_QK_DIM // 4) * DIFF_Q_SCALE
        qd_ref[0, :, h * LANE:(h + 1) * LANE] = jnp.where(lo, q, 0.0).astype(BF16)
        qd_ref[1, :, h * LANE:(h + 1) * LANE] = jnp.where(lo, 0.0, q).astype(BF16)
        put(kd_ref, h, _rope(rms_half(col(OFF_KD, h), g_d_k), cd, sd, DIFF_QK_DIM // 4))
        put(vd_ref, 2 * h, col(OFF_VD, h))
        vd_ref[:, (2 * h + 1) * LANE:(2 * h + 2) * LANE] = ones_col
    for h in range(SWA_Q_HEADS):
        put(qs_ref, h, _rope(rms_head(col(OFF_QS, h), g_s_q), cs, ss, HEAD_DIM // 4) * ATTN_Q_SCALE)
    for h in range(SWA_KV_HEADS):
        put(ks_ref, h, _rope(rms_head(col(OFF_KS, h), g_s_k), cs, ss, HEAD_DIM // 4))
        put(vs_ref, h, col(OFF_VS, h))


def _prep(u, pvec, cs, ss, cd, sd):
    n_batch, nt, _ = u.shape
    tr = _tile(nt, 256)
    row = lambda w: pl.BlockSpec((None, tr, w), lambda b, i: (b, i, 0))
    tab = pl.BlockSpec((tr, LANE), lambda b, i: (i, 0))
    sd_ = lambda w: jax.ShapeDtypeStruct((n_batch, nt, w), BF16)
    return pl.pallas_call(
        _prep_kernel,
        grid=(n_batch, nt // tr),
        in_specs=[row(D_ATT), pl.BlockSpec((8, LANE), lambda b, i: (0, 0)), tab, tab, tab, tab],
        out_specs=[row(1024), row(1024), row(1024),
                   pl.BlockSpec((None, 2, tr, 1024), lambda b, i: (b, 0, i, 0)),
                   row(1024), row(2048), row(1024), row(256), row(256)],
        out_shape=[sd_(1024), sd_(1024), sd_(1024),
                   jax.ShapeDtypeStruct((n_batch, 2, nt, 1024), BF16),
                   sd_(1024), sd_(2048), sd_(1024), sd_(256), sd_(256)],
        compiler_params=_params(("arbitrary", "arbitrary"), 48),
        name="qkv_prep",
    )(u, pvec, cs, ss, cd, sd)


NA_QBLK = 2 * GRID_W
NA_KROWS = 10
NA_KBLK = NA_KROWS * GRID_W


def _skewed(n_chains, scores, numerators, output):
    s_prev = p_prev = None
    for c in range(n_chains + 2):
        s_new = scores(c) if c < n_chains else None
        p_new = numerators(c - 1, s_prev) if s_prev is not None else None
        if p_prev is not None:
            output(c - 2, p_prev)
        s_prev, p_prev = s_new, p_new


def _na_kernel(sc_ref, q_ref, k_ref, v_ref, *rest, lc, nlb, group, online):
    bias_refs, o_ref = rest[:group], rest[group]
    i0 = pl.program_id(2) * group
    kc = k_ref[0:lc, :]
    vc = v_ref[0:lc, :]

    def window_start(g):
        li = i0 + g - lc // NA_QBLK
        kb = jnp.clip(li - 2, 0, nlb - NA_KROWS // 2)
        return pl.multiple_of(lc + kb * NA_QBLK, NA_QBLK)

    def scores(g):
        q = q_ref[g * NA_QBLK:(g + 1) * NA_QBLK, :]
        return _dot_nt(q, k_ref[pl.ds(window_start(g), NA_KBLK), :]) + bias_refs[g][...], _dot_nt(q, kc)

    def numerators(g, s):
        s_n, s_c = s
        if online:
            m = jnp.maximum(jnp.max(s_n, axis=-1, keepdims=True), jnp.max(s_c, axis=-1, keepdims=True))
        else:
            m = sc_ref[0]
        p_n = jnp.exp2(s_n - m)
        p_c = jnp.exp2(s_c - m)
        l = jnp.sum(p_n, axis=-1, keepdims=True) + jnp.sum(p_c, axis=-1, keepdims=True)
        return p_n.astype(BF16), p_c.astype(BF16), l

    def output(g, p):
        o = _dot(p[0], v_ref[pl.ds(window_start(g), NA_KBLK), :]) + _dot(p[1], vc)
        o_ref[g * NA_QBLK:(g + 1) * NA_QBLK, :] = (o / p[2]).astype(BF16)

    _skewed(group, scores, numerators, output)


def _na_bias_tables(rpb, n):
    rows = n // GRID_W
    nlb = rows // 2
    reps = [0, 1, 2, nlb - 2, nlb - 1]
    h = rpb.shape[0]
    cols = np.arange(GRID_W)
    cstart = np.clip(cols - NA_KW // 2, 0, GRID_W - NA_KW)
    col_ok = (cols[None, :] >= cstart[:, None]) & (cols[None, :] < cstart[:, None] + NA_KW)
    dx = cols[None, :] - cols[:, None] + (NA_KW - 1)
    sel = (dx[None] == np.arange(2 * NA_KW - 1)[:, None, None]) & col_ok[None]
    tx = jnp.einsum('hyx,xqk->hyqk', rpb.astype(F32), jnp.asarray(sel, F32), precision=lax.Precision.HIGHEST)
    dy_i = np.zeros((5, 2, NA_KROWS), np.int32)
    row_ok = np.zeros((5, 2, NA_KROWS), bool)
    for c, li in enumerate(reps):
        kb = int(np.clip(li - 2, 0, nlb - NA_KROWS // 2))
        qr = 2 * li + np.arange(2)
        kr = 2 * kb + np.arange(NA_KROWS)
        rs = np.clip(qr - NA_KH // 2, 0, rows - NA_KH)
        row_ok[c] = (kr[None, :] >= rs[:, None]) & (kr[None, :] < rs[:, None] + NA_KH)
        dy_i[c] = np.clip(kr[None, :] - qr[:, None] + (NA_KH - 1), 0, 2 * NA_KH - 2)
    blocks = jnp.take(tx, jnp.asarray(dy_i.reshape(-1)), axis=1)
    blocks = blocks.reshape(h, 5, 2, NA_KROWS, GRID_W, GRID_W)
    ok = row_ok[:, :, :, None, None] & col_ok[None, None, None]
    tbl = jnp.where(ok[None], blocks * LOG2E, NEG_INF)
    tbl = tbl.transpose(1, 0, 2, 4, 3, 5).reshape(5, h, NA_QBLK, NA_KBLK)
    dead = jnp.full((1,) + tbl.shape[1:], NEG_INF, F32)
    return jnp.concatenate([tbl, dead], axis=0)


def _group(nblocks, target):
    return max(g for g in range(1, target + 1) if nblocks % g == 0)


def _softmax_dispatch(bound, call):
    sc = jnp.reshape(bound, (1,)).astype(F32)
    return lax.cond(bound <= ATTN_FAST_BOUND, lambda s: call(False)(s), lambda s: call(True)(s), sc)


def _na(bound, qa, ka, va, bias, lc):
    n_batch, nt, _ = qa.shape
    n = nt - lc
    nlb = n // NA_QBLK
    assert nlb >= 5 and lc % NA_QBLK == 0
    ncb = lc // NA_QBLK
    group = _group(nt // NA_QBLK, 11)

    def case(i):
        li = i - ncb
        return jnp.where(li < 0, 5, jnp.where(li < 2, li, jnp.where(li >= nlb - 2, li - (nlb - 5), 2)))

    kv = pl.BlockSpec((None, nt, LANE), lambda b, h, i: (b, 0, h))
    qo = pl.BlockSpec((None, group * NA_QBLK, LANE), lambda b, h, i: (b, i, h))
    bias_specs = [pl.BlockSpec((None, None, NA_QBLK, NA_KBLK),
                               functools.partial(lambda b, h, i, g: (case(i * group + g), h, 0, 0), g=g))
                  for g in range(group)]

    def call(online):
        kern = functools.partial(_na_kernel, lc=lc, nlb=nlb, group=group, online=online)
        return lambda sc: pl.pallas_call(
            kern,
            grid=(n_batch, NA_HEADS, nt // (group * NA_QBLK)),
            in_specs=[pl.BlockSpec(memory_space=pltpu.SMEM), qo, kv, kv] + bias_specs,
            out_specs=qo,
            out_shape=jax.ShapeDtypeStruct((n_batch, nt, NA_HEADS * HEAD_DIM), BF16),
            compiler_params=_params(("arbitrary",) * 3, 40),
            name="na_attn_online" if online else "na_attn_offset",
        )(sc, qa, ka, va, *([bias] * group))

    return _softmax_dispatch(bound, call)


SWA_BLK = 128
SWA_WIN = 3 * SWA_BLK


def _swa_kernel(sc_ref, q_ref, k_ref, v_ref, sink_ref, *rest, lc, n, group, online):
    mask_refs, o_ref = rest[:group], rest[group]
    i0 = pl.program_id(2) * group
    kc = k_ref[0:lc, :]
    vc = v_ref[0:lc, :]

    def window_start(c):
        li = i0 + c // SWA_GROUP - lc // SWA_BLK
        ws = jnp.clip((li - 1) * SWA_BLK, 0, n - SWA_WIN)
        return pl.multiple_of(lc + ws, SWA_BLK)

    def block(c):
        gq, g = divmod(c, SWA_GROUP)
        return slice(gq * SWA_BLK, (gq + 1) * SWA_BLK), slice(g * LANE, (g + 1) * LANE)

    def scores(c):
        rows, cols = block(c)
        q = q_ref[rows, cols]
        return _dot_nt(q, k_ref[pl.ds(window_start(c), SWA_WIN), :]), _dot_nt(q, kc)

    def numerators(c, s):
        s_w, s_c = s
        g = c % SWA_GROUP
        mask = mask_refs[c // SWA_GROUP][...]
        sink = sink_ref[g * SWA_BLK:(g + 1) * SWA_BLK, 0:1]
        if online:
            s_w = jnp.where(mask > 0.0, s_w, NEG_INF)
            m = jnp.maximum(jnp.maximum(jnp.max(s_w, axis=-1, keepdims=True),
                                        jnp.max(s_c, axis=-1, keepdims=True)), sink)
            p_w = jnp.exp2(s_w - m)
        else:
            m = sc_ref[0]
            p_w = jnp.exp2(s_w - m) * mask
        p_c = jnp.exp2(s_c - m)
        l = jnp.sum(p_w, axis=-1, keepdims=True) + jnp.sum(p_c, axis=-1, keepdims=True) + jnp.exp2(sink - m)
        return p_w.astype(BF16), p_c.astype(BF16), l

    def output(c, p):
        rows, cols = block(c)
        o = (_dot(p[0], v_ref[pl.ds(window_start(c), SWA_WIN), :]) + _dot(p[1], vc)) / p[2]
        o_ref[rows, cols] = o.astype(BF16)

    _skewed(group * SWA_GROUP, scores, numerators, output)


def _swa_masks():
    r = np.arange(SWA_BLK)[:, None]
    j = np.arange(SWA_WIN)[None, :]
    rel = [j - r, j - r - SWA_BLK, j - r - 2 * SWA_BLK]
    tabs = [(np.abs(d) <= SWA_WINDOW) for d in rel] + [np.zeros((SWA_BLK, SWA_WIN), bool)]
    return jnp.asarray(np.stack(tabs), F32)


def _swa(bound, qs, ks, vs, sink, lc):
    n_batch, nt, _ = qs.shape
    n = nt - lc
    nb = n // SWA_BLK
    assert nb >= 3 and lc % SWA_BLK == 0
    ncb = lc // SWA_BLK
    rows = SWA_GROUP * SWA_BLK
    group = _group(nt // SWA_BLK, 6)
    sink_rows = jnp.broadcast_to(
        jnp.repeat((sink.astype(F32) * LOG2E).reshape(SWA_KV_HEADS, SWA_GROUP), SWA_BLK, axis=1)[:, :, None],
        (SWA_KV_HEADS, rows, LANE))
    masks = _swa_masks()

    def case(i):
        li = i - ncb
        return jnp.where(li < 0, 3, jnp.where(li == 0, 0, jnp.where(li == nb - 1, 2, 1)))

    kv = pl.BlockSpec((None, nt, LANE), lambda b, k, i: (b, 0, k))
    qo = pl.BlockSpec((None, group * SWA_BLK, SWA_GROUP * LANE), lambda b, k, i: (b, i, k))
    mask_specs = [pl.BlockSpec((None, SWA_BLK, SWA_WIN),
                               functools.partial(lambda b, k, i, g: (case(i * group + g), 0, 0), g=g))
                  for g in range(group)]

    def call(online):
        kern = functools.partial(_swa_kernel, lc=lc, n=n, group=group, online=online)
        return lambda sc: pl.pallas_call(
            kern,
            grid=(n_batch, SWA_KV_HEADS, nt // (group * SWA_BLK)),
            in_specs=[pl.BlockSpec(memory_space=pltpu.SMEM), qo, kv, kv,
                      pl.BlockSpec((None, rows, LANE), lambda b, k, i: (k, 0, 0))] + mask_specs,
            out_specs=qo,
            out_shape=jax.ShapeDtypeStruct((n_batch, nt, SWA_Q_HEADS * HEAD_DIM), BF16),
            compiler_params=_params(("arbitrary",) * 3, 32),
            name="swa_attn_online" if online else "swa_attn_offset",
        )(sc, qs, ks, vs, sink_rows, *([masks] * group))

    return _softmax_dispatch(bound, call)


def _diff_finish(sc_ref, g_ref, o_ref, num, den, tq, post_scale):
    o2 = num / den
    o = o2[0:tq, :] - sc_ref[0] * o2[tq:2 * tq, :]
    o_ref[...] = (_rms_full(o, g_ref[6:7, :]) * post_scale).astype(BF16)


def _diff_online_kernel(sc_ref, q_ref, k_ref, v_ref, g_ref, o_ref, *, lc, nt, tq, tk, post_scale):
    qi = pl.program_id(2)
    q = q_ref[...].reshape(2 * tq, LANE)
    nk = jnp.where(qi < lc // tq, lc // tk, nt // tk)

    def body(kt, carry):
        m, l, acc = carry
        k0 = pl.multiple_of(kt * tk, tk)
        k = k_ref[pl.ds(k0, tk), :]
        v = v_ref[pl.ds(k0, tk), :]
        s = _dot_nt(q, k)
        m_new = jnp.maximum(m, jnp.max(s, axis=-1, keepdims=True))
        alpha = jnp.exp2(m - m_new)
        p = jnp.exp2(s - m_new)
        l = alpha * l + jnp.sum(p, axis=-1, keepdims=True)
        acc = alpha * acc + _dot(p.astype(BF16), v)
        return m_new, l, acc

    init = (jnp.full((2 * tq, 1), NEG_INF, F32), jnp.zeros((2 * tq, 1), F32), jnp.zeros((2 * tq, LANE), F32))
    _, l, acc = lax.fori_loop(0, nk, body, init)
    _diff_finish(sc_ref, g_ref, o_ref, acc, l, tq, post_scale)


def _diff_offset_kernel(sc_ref, q_ref, qn_ref, k_ref, v_ref, g_ref, o_ref, acc_ref, p_ref,
                        *, lc, nt, tq, tk, post_scale):
    qi = pl.program_id(2)
    rows2 = 2 * tq
    nk = nt // tk
    ncq = lc // tq
    q = q_ref[...].reshape(rows2, LANE)
    off = sc_ref[1]

    def probs(qq, k0, tkk):
        return jnp.exp2(_dot_nt(qq, k_ref[k0:k0 + tkk, :]) - off).astype(BF16)

    def next_tile_probs():
        return probs(qn_ref[...].reshape(rows2, LANE), 0, tk)

    @pl.when(qi < ncq)
    def _():
        acc_ref[...] = _dot(probs(q, 0, lc), v_ref[0:lc, :])

        @pl.when(qi == ncq - 1)
        def _():
            p_ref[0] = next_tile_probs()

    def latent(first_slot):
        tot = None
        for t in range(nk):
            pv = _dot(p_ref[(t + first_slot) % 2], v_ref[t * tk:(t + 1) * tk, :])
            p_ref[(t + 1 + first_slot) % 2] = probs(q, (t + 1) * tk, tk) if t + 1 < nk else next_tile_probs()
            tot = pv if tot is None else tot + pv
        acc_ref[...] = tot

    first_slot = ((qi - ncq) * nk) % 2
    for slot in range(2 if nk % 2 else 1):
        @pl.when((qi >= ncq) & (first_slot == slot))
        def _():
            latent(slot)

    acc = acc_ref[...]
    _diff_finish(sc_ref, g_ref, o_ref, acc[:, 0:LANE], acc[:, LANE:LANE + 1], tq, post_scale)


def _diff(sc, qd, kd, vde, pvec, lc, post_scale):
    n_batch, _, nt, _ = qd.shape
    tq = _tile(math.gcd(lc, nt), 256)
    assert lc % tq == 0
    grid = (n_batch, DIFF_HEADS, nt // tq)
    smem = pl.BlockSpec(memory_space=pltpu.SMEM)
    qspec = pl.BlockSpec((None, 2, tq, LANE), lambda b, h, i: (b, 0, i, h))
    kspec = pl.BlockSpec((None, nt, LANE), lambda b, h, i: (b, 0, h))
    gspec = pl.BlockSpec((8, LANE), lambda b, h, i: (0, 0))
    ospec = pl.BlockSpec((None, tq, LANE), lambda b, h, i: (b, i, h))
    oshape = jax.ShapeDtypeStruct((n_batch, nt, DIFF_HEADS * HEAD_DIM), BF16)

    def offset(sc, qd, kd, vde, pvec):
        tk = _tile(nt, 768)
        kern = functools.partial(_diff_offset_kernel, lc=lc, nt=nt, tq=tq, tk=tk, post_scale=post_scale)
        nq = nt // tq
        qnext = pl.BlockSpec((None, 2, tq, LANE), lambda b, h, i: (b, 0, jnp.minimum(i + 1, nq - 1), h))
        return pl.pallas_call(
            kern, grid=grid,
            in_specs=[smem, qspec, qnext, kspec, pl.BlockSpec((None, nt, 2 * LANE), lambda b, h, i: (b, 0, h)),
                      gspec],
            out_specs=ospec, out_shape=oshape,
            scratch_shapes=[pltpu.VMEM((2 * tq, 2 * LANE), F32), pltpu.VMEM((2, 2 * tq, tk), BF16)],
            compiler_params=_params(("arbitrary",) * 3, 40),
            name="diff_attn_offset",
        )(sc, qd, qd, kd, vde, pvec)

    def online(sc, qd, kd, vde, pvec):
        tk = _tile(math.gcd(lc, nt), 256)
        kern = functools.partial(_diff_online_kernel, lc=lc, nt=nt, tq=tq, tk=tk, post_scale=post_scale)
        return pl.pallas_call(
            kern, grid=grid,
            in_specs=[smem, qspec, kspec, pl.BlockSpec((None, nt, LANE), lambda b, h, i: (b, 0, 2 * h)), gspec],
            out_specs=ospec, out_shape=oshape,
            compiler_params=_params(("arbitrary",) * 3, 32),
            name="diff_attn_online",
        )(sc, qd, kd, vde, pvec)

    return lax.cond(sc[1] <= ATTN_FAST_BOUND, offset, online, sc, qd, kd, vde, pvec)


LRU_HALO = 8


LRU_STEP_BLOCKS = 4


def _lru_gate_kernel(xp_ref, x_ref, xn_ref, cw_ref, cb_ref, gw_ref, gb_ref, lam_ref,
                     af_ref, bf_ref, ab_ref, bb_ref, *, t, seg_starts, seg_ends):
    k = pl.program_id(2)
    is_start = functools.reduce(jnp.logical_or, [k == s for s in seg_starts])
    is_end = functools.reduce(jnp.logical_or, [k == s for s in seg_ends])
    h = LRU_HALO
    outs = ((af_ref, bf_ref), (ab_ref, bb_ref))
    sigmoid = lambda z: 0.5 + 0.5 * jnp.tanh(0.5 * z)
    for c in range(LRU_STEP_BLOCKS):
        cs = slice(c * LANE, (c + 1) * LANE)
        cur = x_ref[:, cs]
        prev = jnp.where(is_start, 0.0, xp_ref[:, cs])
        nxt = jnp.where(is_end, 0.0, xn_ref[:, cs])
        ext = jnp.concatenate([prev, cur, nxt], axis=0)
        xl = (cb_ref[:, cs] + cw_ref[0:1, cs] * ext[h - 2:h - 2 + t, :] + cw_ref[1:2, cs] * ext[h - 1:h - 1 + t, :]
              + cw_ref[2:3, cs] * cur + cw_ref[3:4, cs] * ext[h + 1:h + 1 + t, :])
        xb = xl.astype(BF16)
        for d in range(2):
            r = sigmoid(_dot(xb, gw_ref[d, 0, c].astype(BF16)) + gb_ref[2 * d:2 * d + 1, cs])
            ig = sigmoid(_dot(xb, gw_ref[d, 1, c].astype(BF16)) + gb_ref[2 * d + 1:2 * d + 2, cs])
            a = jnp.exp(LRU_C * r * jax.nn.log_sigmoid(lam_ref[d:d + 1, cs]))
            outs[d][0][:, cs] = a
            outs[d][1][:, cs] = jnp.sqrt(1.0 - a * a) * (ig * xl)


def _lru_gates(u, conv_w, conv_b, gate_w, gate_b, lam, lc):
    n_batch, nt, _ = u.shape
    t = _tile(math.gcd(lc, nt), 256)
    nk = nt // t
    hb = t // LRU_HALO
    seg_starts = (0, lc // t)
    seg_ends = (lc // t - 1, nk - 1)
    w = LRU_STEP_BLOCKS * LANE
    assert OFF_XR % w == 0 and LRU_WIDTH % w == 0
    cb0 = OFF_XR // w
    kern = functools.partial(_lru_gate_kernel, t=t, seg_starts=seg_starts, seg_ends=seg_ends)
    out = pl.BlockSpec((None, t, w), lambda b, c, k: (b, k, c))
    osd = jax.ShapeDtypeStruct((n_batch, nt, LRU_WIDTH), F32)
    return pl.pallas_call(
        kern,
        grid=(n_batch, LRU_WIDTH // w, nk),
        in_specs=[pl.BlockSpec((None, LRU_HALO, w), lambda b, c, k: (b, jnp.maximum(k * hb - 1, 0), cb0 + c)),
                  pl.BlockSpec((None, t, w), lambda b, c, k: (b, k, cb0 + c)),
                  pl.BlockSpec((None, LRU_HALO, w),
                               lambda b, c, k: (b, jnp.minimum((k + 1) * hb, nt // LRU_HALO - 1), cb0 + c)),
                  pl.BlockSpec((4, w), lambda b, c, k: (0, c)),
                  pl.BlockSpec((1, w), lambda b, c, k: (0, c)),
                  pl.BlockSpec((2, 2, LRU_STEP_BLOCKS, LANE, LANE), lambda b, c, k: (0, 0, c, 0, 0)),
                  pl.BlockSpec((4, w), lambda b, c, k: (0, c)),
                  pl.BlockSpec((2, w), lambda b, c, k: (0, c))],
        out_specs=[out, out, out, out],
        out_shape=[osd, osd, osd, osd],
        compiler_params=_params(("arbitrary",) * 3, 24),
        name="lru_gates",
    )(u, u, u, conv_w, conv_b.reshape(1, LRU_WIDTH), gate_w, gate_b.reshape(4, LRU_WIDTH), lam)


def _lru_scan_kernel(af_ref, bf_ref, ab_ref, bb_ref, hf_ref, hb_ref, cf_ref, cbk_ref, *, t):
    @pl.when(pl.program_id(1) == 0)
    def _():
        cf_ref[...] = jnp.zeros_like(cf_ref)
        cbk_ref[...] = jnp.zeros_like(cbk_ref)

    ng = t // 8

    def body(g, carry):
        hf, hb = carry
        r0 = pl.multiple_of(g * 8, 8)
        a = af_ref[pl.ds(r0, 8), :]
        bt = bf_ref[pl.ds(r0, 8), :]
        rows = []
        for r in range(8):
            hf = a[r:r + 1, :] * hf + bt[r:r + 1, :]
            rows.append(hf)
        hf_ref[pl.ds(r0, 8), :] = jnp.concatenate(rows, axis=0)
        r1 = pl.multiple_of((ng - 1 - g) * 8, 8)
        a = ab_ref[pl.ds(r1, 8), :]
        bt = bb_ref[pl.ds(r1, 8), :]
        rows = []
        for r in range(7, -1, -1):
            hb = a[r:r + 1, :] * hb + bt[r:r + 1, :]
            rows.append(hb)
        hb_ref[pl.ds(r1, 8), :] = jnp.concatenate(rows[::-1], axis=0)
        return hf, hb

    hf, hb = lax.fori_loop(0, ng, body, (cf_ref[...], cbk_ref[...]))
    cf_ref[...] = hf
    cbk_ref[...] = hb


def _lru_scan(af, bf, ab, bb, lc):
    n_batch, nt, w = af.shape
    t = lc
    assert nt % t == 0
    nk = nt // t
    kern = functools.partial(_lru_scan_kernel, t=t)
    fwd = pl.BlockSpec((None, t, w), lambda b, k: (b, k, 0))
    bwd = pl.BlockSpec((None, t, w), lambda b, k: (b, jnp.where(k == 0, 0, nk - k), 0))
    osd = jax.ShapeDtypeStruct((n_batch, nt, w), F32)
    return pl.pallas_call(
        kern,
        grid=(n_batch, nk),
        in_specs=[fwd, fwd, bwd, bwd],
        out_specs=[fwd, bwd],
        out_shape=[osd, osd],
        scratch_shapes=[pltpu.VMEM((1, w), F32), pltpu.VMEM((1, w), F32)],
        compiler_params=_params(("arbitrary", "arbitrary"), 24),
        name="lru_scan",
    )(af, bf, ab, bb)


def _outproj_kernel(oa_ref, od_ref, os_ref, hf_ref, hb_ref, g0_ref, g1_ref, gate_ref, w_ref, x_ref,
                    o_ref, lhs_ref, *, n_batch, lc, tm):
    b = pl.program_id(0)
    i = pl.program_id(1)

    @pl.when(pl.program_id(2) == 0)
    def _():
        def chunk(r0):
            rs = pl.ds(r0, ROW_CHUNK)
            lhs_ref[rs, 0:1024] = oa_ref[rs, :]
            lhs_ref[rs, 1024:2048] = od_ref[rs, :]
            lhs_ref[rs, 2048:3072] = os_ref[rs, :]
            for c, g_ref in enumerate((g0_ref, g1_ref)):
                cs = slice(c * 512, (c + 1) * 512)
                y = (hf_ref[rs, cs] + hb_ref[rs, cs]) * jax.nn.gelu(g_ref[rs, :])
                lhs_ref[rs, 3072 + c * 512:3584 + c * 512] = y.astype(BF16)

        _for_row_chunks(tm, chunk)

    row = i * tm + lax.broadcasted_iota(I32, (tm, 1), 0)
    gate = jnp.where(row < lc, gate_ref[pl.ds(n_batch, 1), :], gate_ref[pl.ds(b, 1), :])
    o_ref[...] = x_ref[...] + gate * _dot(lhs_ref[...], w_ref[...])


def _outproj(o_a, o_d, o_s, hf, hb, u, modl, w_bf, layer, xs, lc):
    n_batch, nt, d = xs.shape
    tm = _tile(nt, 768)
    tn = _tile(d, 512)
    ntile = d // tn
    kern = functools.partial(_outproj_kernel, n_batch=n_batch, lc=lc, tm=tm)
    mix = pl.BlockSpec((None, tm, 1024), lambda b, i, j: (b, i, 0))
    gr = lambda c: pl.BlockSpec((None, tm, 512), lambda b, i, j: (b, i, OFF_GR // 512 + c))
    xblk = pl.BlockSpec((None, tm, tn), lambda b, i, j: (b, i, j))
    return pl.pallas_call(
        kern,
        grid=(n_batch, nt // tm, ntile),
        in_specs=[mix, mix, mix, mix, mix, gr(0), gr(1),
                  pl.BlockSpec((MOD_ROWS, tn), lambda b, i, j: (0, 2 * ntile + j)),
                  pl.BlockSpec((None, D_MIX, tn), lambda b, i, j: (layer, 0, j)), xblk],
        out_specs=xblk,
        out_shape=jax.ShapeDtypeStruct(xs.shape, F32),
        scratch_shapes=[pltpu.VMEM((tm, D_MIX), BF16)],
        input_output_aliases={9: 0},
        compiler_params=_params(("arbitrary",) * 3, 56),
        name="out_proj",
    )(o_a, o_d, o_s, hf, hb, u, u, modl, w_bf, xs)


def _split2(x):
    hi = x.astype(BF16)
    return hi, (x - hi.astype(F32)).astype(BF16)


ROUTER_KC = 512


HI16 = -65536


def _pack_bf16_pair(lo, hi):
    bits = lambda v: lax.bitcast_convert_type(v.astype(BF16).astype(F32), I32)
    return (bits(hi) & HI16) | lax.shift_right_logical(bits(lo), 16)


def _unpack_bf16_pair(w):
    lo = lax.bitcast_convert_type(lax.shift_left(w, 16), F32)
    hi = lax.bitcast_convert_type(w & HI16, F32)
    return lo.astype(BF16), hi.astype(BF16)


def _router_kernel(x_ref, g_ref, sh_ref, sc_ref, w0_ref, w1_ref, xm_ref, aff_ref, xf_ref, *, n_batch, lc, tm):
    b = pl.program_id(0)
    i = pl.program_id(1)
    d = x_ref.shape[1]
    half = d // 2

    def chunk(r0):
        rs = pl.ds(r0, ROW_CHUNK)
        sh = _mod_row(sh_ref, i * tm + r0, b, n_batch, lc)
        sc = _mod_row(sc_ref, i * tm + r0, b, n_batch, lc)
        xm = _norm_modulate(x_ref[rs, :], g_ref[...], sh, sc)
        xf_ref[rs, :] = xm
        xm_ref[rs, 0:half] = _pack_bf16_pair(xm[:, 0:half], xm[:, half:d])

    _for_row_chunks(tm, chunk)

    logits = jnp.zeros((tm, LANE), F32)
    tk = _tile(d, ROUTER_KC)
    for kc in range(d // tk):
        ks = slice(kc * tk, (kc + 1) * tk)
        x0, x1 = _split2(xf_ref[:, ks])
        logits = logits + (_dot(x0, w0_ref[ks, :]) + (_dot(x0, w1_ref[ks, :]) + _dot(x1, w0_ref[ks, :])))
    lane = lax.broadcasted_iota(I32, (1, LANE), 1)
    logits = jnp.where(lane < N_EXPERTS, logits, NEG_INF)
    m = jnp.max(logits, axis=-1, keepdims=True)
    p = jnp.exp(logits - m)
    aff = p / jnp.sum(p, axis=-1, keepdims=True)
    aff_ref[...] = aff
    xm_ref[:, half:half + LANE] = lax.bitcast_convert_type(aff, I32)


def _router(xs, g, modl, wr_pad, lc):
    n_batch, nt, d = xs.shape
    tm = _tile(nt, 256)
    assert lc % ROW_CHUNK == 0
    kern = functools.partial(_router_kernel, n_batch=n_batch, lc=lc, tm=tm)
    wspec = pl.BlockSpec((d, LANE), lambda b, i: (0, 0))
    return pl.pallas_call(
        kern,
        grid=(n_batch, nt // tm),
        in_specs=[pl.BlockSpec((None, tm, d), lambda b, i: (b, i, 0)),
                  pl.BlockSpec((1, d), lambda b, i: (0, 0)),
                  pl.BlockSpec((MOD_ROWS, d), lambda b, i: (0, 3)),
                  pl.BlockSpec((MOD_ROWS, d), lambda b, i: (0, 4)),
                  wspec, wspec],
        out_specs=[pl.BlockSpec((None, tm, d // 2 + LANE), lambda b, i: (b, i, 0)),
                   pl.BlockSpec((None, tm, LANE), lambda b, i: (b, i, 0))],
        out_shape=[jax.ShapeDtypeStruct((n_batch, nt, d // 2 + LANE), I32),
                   jax.ShapeDtypeStruct((n_batch, nt, LANE), F32)],
        scratch_shapes=[pltpu.VMEM((tm, d), F32)],
        compiler_params=_params(("arbitrary", "arbitrary"), 48),
        name="norm2_router",
    )(xs, g, modl, modl, *_split2(wr_pad))


def _select_kernel(aff_ref, l_ref, cnt_ref, off_ref, bits_ref, *, row0, n, cap):
    nc = n // LANE
    bits_ref[...] = lax.bitcast_convert_type(aff_ref[row0:row0 + n, :], I32)

    def bit_step(i, thr):
        cand = thr | (jnp.int32(1) << (30 - i))
        cnt = jnp.sum((bits_ref[...] >= cand).astype(I32), axis=0, keepdims=True)
        return jnp.where(cnt >= cap, cand, thr)

    thr = lax.fori_loop(0, 31, bit_step, jnp.zeros((1, LANE), I32))
    n_gt = jnp.sum((bits_ref[...] > thr).astype(I32), axis=0, keepdims=True)
    need = (cap - n_gt).astype(F32)

    r_io = lax.broadcasted_iota(I32, (LANE, LANE), 0)
    c_io = lax.broadcasted_iota(I32, (LANE, LANE), 1)
    ltri = (c_io <= r_io).astype(BF16)
    t_col = r_io.astype(F32)
    j_row = c_io.astype(F32)

    def chunk(c, carry):
        run_eq, run_sel = carry
        r0 = pl.multiple_of(c * LANE, LANE)
        bc = bits_ref[pl.ds(r0, LANE), :]
        eq = bc == thr
        eq_f = eq.astype(F32)
        incl_eq = _dot(ltri, eq_f.astype(BF16))
        rank = run_eq + incl_eq - eq_f
        sel = (bc > thr) | (eq & (rank < need))
        sel_f = sel.astype(F32)
        incl = _dot(ltri, sel_f.astype(BF16))
        cnt_c = incl[LANE - 1:LANE, :]
        qm = jnp.where(sel, incl - sel_f, -1.0)
        for e in range(N_EXPERTS):
            pos = jnp.broadcast_to(qm[:, e:e + 1], (LANE, LANE))
            local = jnp.sum(jnp.where(pos == j_row, t_col, 0.0), axis=0, keepdims=True)
            l_ref[e, pl.ds(c, 1), :] = local.astype(I32)
        cnt_ref[pl.ds(c, 1), :] = cnt_c.astype(I32)
        off_ref[pl.ds(c, 1), :] = run_sel.astype(I32)
        return run_eq + incl_eq[LANE - 1:LANE, :], run_sel + cnt_c

    zero = jnp.zeros((1, LANE), F32)
    lax.fori_loop(0, nc, chunk, (zero, zero))


def _select(affp, row0, n, cap):
    n_batch, nt, _ = affp.shape
    nc = n // LANE
    kern = functools.partial(_select_kernel, row0=row0, n=n, cap=cap)
    return pl.pallas_call(
        kern,
        grid=(n_batch,),
        in_specs=[pl.BlockSpec((None, nt, LANE), lambda b: (b, 0, 0))],
        out_specs=[pl.BlockSpec((None, N_EXPERTS, nc, LANE), lambda b: (b, 0, 0, 0)),
                   pl.BlockSpec((None, nc, LANE), lambda b: (b, 0, 0)),
                   pl.BlockSpec((None, nc, LANE), lambda b: (b, 0, 0))],
        out_shape=[jax.ShapeDtypeStruct((n_batch, N_EXPERTS, nc, LANE), I32),
                   jax.ShapeDtypeStruct((n_batch, nc, LANE), I32),
                   jax.ShapeDtypeStruct((n_batch, nc, LANE), I32)],
        scratch_shapes=[pltpu.VMEM((n, LANE), I32)],
        compiler_params=_params(("arbitrary",), 40),
        name="ec_select",
    )(affp)


MOE_ROWS = 256


def _moe_kernel(*refs, sets, d):
    ns = len(sets)
    tab_hbm = refs[0:3 * ns]
    xm_hbm, _, g2_ref, w1_ref, w3_ref, w2_ref, xs_hbm = refs[3 * ns:3 * ns + 7]
    tab_s = refs[3 * ns + 7:6 * ns + 7]
    idx_s, xg, rows, xb, hdn_s, sem = refs[6 * ns + 7:]
    total = sum(cap for _, _, cap, _ in sets)
    b = pl.program_id(0)
    e = pl.program_id(1)
    step = b * pl.num_programs(1) + e
    last_step = pl.num_programs(0) * pl.num_programs(1) - 1

    tables = []
    for k in range(ns):
        l_hbm, cnt_hbm, off_hbm = tab_hbm[3 * k:3 * k + 3]
        l_s, cnt_s, off_s = tab_s[3 * k:3 * k + 3]
        tables += [pltpu.make_async_copy(l_hbm.at[b, e], l_s, sem.at[3 + 3 * k]),
                   pltpu.make_async_copy(cnt_hbm.at[b], cnt_s, sem.at[4 + 3 * k]),
                   pltpu.make_async_copy(off_hbm.at[b], off_s, sem.at[5 + 3 * k])]
    for cp in tables:
        cp.start()
    for cp in tables:
        cp.wait()

    def x_gather(tok, slot, n_rows=1):
        return pltpu.make_async_copy(xm_hbm.at[b, pl.ds(tok, n_rows), :], xg.at[pl.ds(slot, n_rows), :], sem.at[0])

    def row_gather(tok, slot, n_rows=1):
        return pltpu.make_async_copy(xs_hbm.at[b, pl.ds(tok, n_rows), :], rows.at[pl.ds(slot, n_rows), :], sem.at[1])

    def row_scatter(tok, slot, n_rows=1):
        return pltpu.make_async_copy(rows.at[pl.ds(slot, n_rows), :], xs_hbm.at[b, pl.ds(tok, n_rows), :], sem.at[2])

    slot0 = 0
    for k, (row0, nc, cap, _) in enumerate(sets):
        l_s, cnt_s, off_s = tab_s[3 * k:3 * k + 3]

        def chunk(c, _, row0=row0, l_s=l_s, cnt_s=cnt_s, off_s=off_s, slot0=slot0):
            base = row0 + c * LANE
            off_c = slot0 + off_s[c, e]

            def one(j, _):
                idx_s[off_c + j] = base + l_s[c, j]
                return 0

            lax.fori_loop(0, cnt_s[c, e], one, 0)
            return 0

        lax.fori_loop(0, nc, chunk, 0)
        slot0 += cap

    def start_all(copy):
        def one(s, _):
            copy(idx_s[s], s).start()
            return 0

        lax.fori_loop(0, total, one, 0, unroll=8)

    start_all(x_gather)

    @pl.when(step > 0)
    def _():
        row_scatter(0, 0, total).wait()

    start_all(row_gather)
    x_gather(0, 0, total).wait()

    tn = _tile(d, 512)
    half = d // 2
    th = _tile(half, 512)
    lane = lax.broadcasted_iota(I32, (1, LANE), 1)

    def hidden(rs, size):
        for kc in range(half // th):
            ks = slice(kc * th, (kc + 1) * th)
            lo, hi = _unpack_bf16_pair(xg[rs, ks])
            xb[0:size, ks] = lo
            xb[0:size, half + kc * th:half + (kc + 1) * th] = hi
        h1 = _dot(xb[0:size, :], w1_ref[...])
        h3 = _dot(xb[0:size, :], w3_ref[...])
        aff = lax.bitcast_convert_type(xg[rs, half:half + LANE], F32)
        gate = jnp.sum(jnp.where(lane == e, aff, 0.0), axis=-1, keepdims=True)
        hdn_s[rs, :] = (h1 * jax.nn.sigmoid(h1) * h3 * gate).astype(BF16)

    def down(rs, g_row):
        g2_row = g_row if g_row is not None else b
        for c in range(d // tn):
            sl = slice(c * tn, (c + 1) * tn)
            rows[rs, sl] = rows[rs, sl] + g2_ref[pl.ds(g2_row, 1), sl] * _dot(hdn_s[rs, :], w2_ref[:, sl])

    def for_blocks(fn):
        slot0 = 0
        for _, _, cap, g_row in sets:
            rc = min(cap, MOE_ROWS)
            if cap // rc > 1:
                def body(r, _, slot0=slot0, rc=rc, g_row=g_row):
                    fn(pl.ds(pl.multiple_of(slot0 + r * rc, rc), rc), rc, g_row)
                    return 0

                lax.fori_loop(0, cap // rc, body, 0)
            else:
                fn(slice(slot0, slot0 + rc), rc, g_row)
            slot0 += cap

    for_blocks(lambda rs, size, g_row: hidden(rs, size))
    row_gather(0, 0, total).wait()
    for_blocks(lambda rs, size, g_row: down(rs, g_row))
    start_all(row_scatter)

    @pl.when(step == last_step)
    def _():
        row_scatter(0, 0, total).wait()


def _moe(token_sets, xma, xs, modl, w1, w3, w2, layer):
    n_batch, nt, d = xs.shape
    sets = tuple((row0, tabs[0].shape[2], cap, g_row) for tabs, row0, cap, g_row in token_sets)
    total = sum(cap for _, _, cap, _ in sets)
    for _, _, cap, _ in sets:
        assert cap % min(cap, MOE_ROWS) == 0 and cap % 8 == 0
    rc = min(max(cap for _, _, cap, _ in sets), MOE_ROWS)
    kern = functools.partial(_moe_kernel, sets=sets, d=d)
    any_ = pl.BlockSpec(memory_space=pl.ANY)
    tables = [t for tabs, _, _, _ in token_sets for t in tabs]
    smem_tables = [pltpu.SMEM((nc, LANE), I32) for _, nc, _, _ in sets for _ in range(3)]
    return pl.pallas_call(
        kern,
        grid=(n_batch, N_EXPERTS),
        in_specs=[any_] * len(tables) + [
            any_, any_,
            pl.BlockSpec((MOD_ROWS, d), lambda b, e: (0, 5)),
            pl.BlockSpec((None, None, d, EXPERT_FF), lambda b, e: (layer, e, 0, 0)),
            pl.BlockSpec((None, None, d, EXPERT_FF), lambda b, e: (layer, e, 0, 0)),
            pl.BlockSpec((None, None, EXPERT_FF, d), lambda b, e: (layer, e, 0, 0))],
        out_specs=any_,
        out_shape=jax.ShapeDtypeStruct(xs.shape, F32),
        scratch_shapes=smem_tables + [
            pltpu.SMEM((total,), I32), pltpu.VMEM((total, d // 2 + LANE), I32), pltpu.VMEM((total, d), F32),
            pltpu.VMEM((rc, d), BF16), pltpu.VMEM((total, EXPERT_FF), BF16),
            pltpu.SemaphoreType.DMA((3 + 3 * len(sets),))],
        input_output_aliases={len(tables) + 1: 0},
        compiler_params=_params(("arbitrary", "arbitrary"), 56),
        name="moe_ffn",
    )(*tables, xma, xs, modl, w1, w3, w2)


def _rope_tables(n, lc):
    t = jnp.arange(n)
    pos = jnp.stack([t // GRID_W, t % GRID_W], axis=-1).astype(F32)

    def tables(dh):
        quarter = dh // 4
        inv_freq = ROPE_THETA ** (-jnp.arange(quarter, dtype=F32) / quarter)
        ang = pos[:, :, None] * inv_freq
        cos, sin = jnp.cos(ang), jnp.sin(ang)
        c = jnp.concatenate([cos[:, 0], cos[:, 0], cos[:, 1], cos[:, 1]], axis=-1)
        s = jnp.concatenate([-sin[:, 0], sin[:, 0], -sin[:, 1], sin[:, 1]], axis=-1)
        reps = LANE // dh
        c, s = jnp.tile(c, (1, reps)), jnp.tile(s, (1, reps))
        return (jnp.concatenate([jnp.ones((lc, LANE), F32), c], axis=0),
                jnp.concatenate([jnp.zeros((lc, LANE), F32), s], axis=0))

    return tables(HEAD_DIM) + tables(DIFF_QK_DIM)


def kernel(x, c, ctx, c_ctx, w_ada, b_ada, norm_g, w_in, w_out, qk_g_na, na_rpb, qk_g_diff, diff_lambda,
           diff_subln_g, qk_g_swa, swa_sink, lru_conv_w, lru_conv_b, lru_gate_w, lru_gate_b, lru_lam,
           w_router, w1, w3, w2):
    n_batch, n, d = x.shape
    lc = ctx.shape[1]
    depth = w_ada.shape[0]
    assert n_batch < MOD_ROWS and w_in.shape[2] == D_IN

    cvec = jnp.zeros((MOD_ROWS, d), F32).at[:n_batch].set(c).at[n_batch].set(c_ctx)
    mod = _ada(cvec, w_ada, b_ada)
    cs, ss, cd, sd = _rope_tables(n, lc)
    xs = jnp.concatenate([ctx, x], axis=1)

    cap_lat = CAPACITY_FACTOR * n // N_EXPERTS
    cap_ctx = CAPACITY_FACTOR * lc // N_EXPERTS
    w_in_b, w_out_b = w_in.astype(BF16), w_out.astype(BF16)
    w1b, w3b, w2b = w1.astype(BF16), w3.astype(BF16), w2.astype(BF16)

    for l in range(depth):
        modl = mod[l]
        u = _inproj(xs, norm_g[l, 0].reshape(1, d), modl, w_in_b, l, lc)

        pvec = jnp.concatenate([qk_g_na[l], jnp.tile(qk_g_diff[l], (1, 2)), qk_g_swa[l],
                                diff_subln_g[l].reshape(1, LANE), jnp.zeros((1, LANE), F32)], axis=0)
        qa, ka, va, qd, kd, vd, qs, ks, vs = _prep(u, pvec, cs, ss, cd, sd)

        gmax = lambda g: jnp.max(jnp.abs(g[0])) * jnp.max(jnp.abs(g[1]))
        b_na = (BOUND_SLACK * HEAD_DIM * ATTN_Q_SCALE * gmax(qk_g_na[l])
                + LOG2E * jnp.max(jnp.abs(na_rpb[l])))
        b_swa = jnp.maximum(BOUND_SLACK * HEAD_DIM * ATTN_Q_SCALE * gmax(qk_g_swa[l]),
                            LOG2E * jnp.max(jnp.abs(swa_sink[l])))
        b_diff = BOUND_SLACK * DIFF_QK_DIM * DIFF_Q_SCALE * gmax(qk_g_diff[l])

        o_a = _na(b_na, qa, ka, va, _na_bias_tables(na_rpb[l], n), lc)

        lambda_init = 0.8 - 0.6 * math.exp(-0.3 * l)
        dl = diff_lambda[l].astype(F32)
        lam = jnp.exp(jnp.sum(dl[0] * dl[1])) - jnp.exp(jnp.sum(dl[2] * dl[3])) + lambda_init
        o_d = _diff(jnp.stack([lam, b_diff]).astype(F32), qd, kd, vd, pvec, lc, 1.0 - lambda_init)

        o_s = _swa(b_swa, qs, ks, vs, swa_sink[l], lc)

        af, bf, ab, bb = _lru_gates(u, lru_conv_w[l], lru_conv_b[l], lru_gate_w[l], lru_gate_b[l], lru_lam[l], lc)
        hf, hb = _lru_scan(af, bf, ab, bb, lc)

        xs = _outproj(o_a, o_d, o_s, hf, hb, u, modl, w_out_b, l, xs, lc)

        wr_pad = jnp.zeros((d, LANE), F32).at[:, :N_EXPERTS].set(w_router[l])
        xma, affp = _router(xs, norm_g[l, 1].reshape(1, d), modl, wr_pad, lc)
        token_sets = [(_select(affp, lc, n, cap_lat), lc, cap_lat, None)]
        if l < depth - 1:
            token_sets.append((_select(affp, 0, lc, cap_ctx), 0, cap_ctx, n_batch))
        xs = _moe(tuple(token_sets), xma, xs, modl, w1b, w3b, w2b, l)
    return xs[:, lc:, :]
```

```python
import functools
import math

import numpy as np
import jax
import jax.numpy as jnp
from jax import lax
from jax.experimental import pallas as pl
from jax.experimental.pallas import tpu as pltpu

GRID_W = 64
HEAD_DIM = 128
NA_HEADS = 8
NA_KH = 8
NA_KW = 16
DIFF_HEADS = 8
DIFF_QK_DIM = 64
SWA_Q_HEADS = 8
SWA_KV_HEADS = 2
SWA_GROUP = SWA_Q_HEADS // SWA_KV_HEADS
SWA_WINDOW = 128
LRU_WIDTH = 1024
LRU_BLOCKS = 8
LRU_C = 8.0
N_EXPERTS = 16
EXPERT_FF = 256
CAPACITY_FACTOR = 2
ROPE_THETA = 10000.0
NEG_INF = -1e30
EPS = 1e-6

OFF_QA, OFF_KA, OFF_VA = 0, 1024, 2048
OFF_QD, OFF_KD, OFF_VD = 3072, 4096, 5120
OFF_QS, OFF_KS, OFF_VS = 6144, 7168, 7424
OFF_XR, OFF_GR = 7680, 8704
D_ATT = 7680
D_IN = 9728
D_MIX = 4096

LOG2E = 1.4426950408889634
DIFF_Q_SCALE = DIFF_QK_DIM ** -0.5 * LOG2E
ATTN_FAST_BOUND = 60.0
ATTN_Q_SCALE = HEAD_DIM ** -0.5 * LOG2E
BOUND_SLACK = 1.02

LANE = 128
MOD_ROWS = 8
V7X_VMEM_BYTES = 64 * 1024 * 1024

F32 = jnp.float32
BF16 = jnp.bfloat16
I32 = jnp.int32


def _tile(n, target, mult=LANE):
    best = None
    for t in range(mult, min(n, target) + 1, mult):
        if n % t == 0:
            best = t
    assert best is not None, (n, target, mult)
    return best


def _params(sem, vmem_mb):
    return pltpu.CompilerParams(dimension_semantics=sem, vmem_limit_bytes=min(vmem_mb, 60) * 1024 * 1024)


def _dot_nt(a, b):
    return lax.dot_general(a, b, (((1,), (1,)), ((), ())), preferred_element_type=F32)


def _dot(a, b):
    return jnp.dot(a, b, preferred_element_type=F32)


def _ada_kernel(c_ref, w_ref, b_ref, o_ref):
    cv = c_ref[...]
    s = cv * jax.nn.sigmoid(cv)
    s_hi = s.astype(BF16)
    s_lo = (s - s_hi.astype(F32)).astype(BF16)
    w = w_ref[...].astype(BF16)
    o_ref[...] = _dot(s_hi, w) + _dot(s_lo, w) + b_ref[...]


def _ada(cvec, w_ada, b_ada):
    depth, d, n6 = w_ada.shape
    tn = _tile(n6, 1024)
    return pl.pallas_call(
        _ada_kernel,
        grid=(depth, n6 // tn),
        in_specs=[pl.BlockSpec((MOD_ROWS, d), lambda l, j: (0, 0)),
                  pl.BlockSpec((None, d, tn), lambda l, j: (l, 0, j)),
                  pl.BlockSpec((None, 1, tn), lambda l, j: (l, 0, j))],
        out_specs=pl.BlockSpec((None, MOD_ROWS, tn), lambda l, j: (l, 0, j)),
        out_shape=jax.ShapeDtypeStruct((depth, MOD_ROWS, n6), F32),
        compiler_params=_params(("arbitrary", "arbitrary"), 56),
        name="ada_mod",
    )(cvec, w_ada, b_ada.reshape(depth, 1, n6))


ROW_CHUNK = 64


def _mod_row(ref, row0, b, n_batch, lc):
    return jnp.where(row0 < lc, ref[pl.ds(n_batch, 1), :], ref[pl.ds(b, 1), :])


def _norm_modulate(x, g, sh, sc):
    ms = jnp.mean(x * x, axis=-1, keepdims=True)
    return x * lax.rsqrt(ms + EPS) * g * (1.0 + sc) + sh


def _for_row_chunks(tm, fn):
    assert tm % ROW_CHUNK == 0

    def body(c, _):
        fn(pl.multiple_of(c * ROW_CHUNK, ROW_CHUNK))
        return 0

    lax.fori_loop(0, tm // ROW_CHUNK, body, 0)


def _inproj_kernel(x_ref, g_ref, sh_ref, sc_ref, w_ref, att_ref, lru_ref, xn_ref, *, n_batch, lc, tm, n_att):
    b = pl.program_id(0)
    i = pl.program_id(1)
    j = pl.program_id(2)

    @pl.when(j == 0)
    def _():
        def chunk(r0):
            sh = _mod_row(sh_ref, i * tm + r0, b, n_batch, lc)
            sc = _mod_row(sc_ref, i * tm + r0, b, n_batch, lc)
            xn = _norm_modulate(x_ref[pl.ds(r0, ROW_CHUNK), :], g_ref[...], sh, sc)
            xn_ref[pl.ds(r0, ROW_CHUNK), :] = xn.astype(BF16)

        _for_row_chunks(tm, chunk)

    @pl.when(j < n_att)
    def _():
        att_ref[...] = _dot(xn_ref[...], w_ref[...]).astype(BF16)

    @pl.when(j >= n_att)
    def _():
        lru_ref[...] = _dot(xn_ref[...], w_ref[...])


def _inproj(xs, g, modl, w_bf, layer, lc):
    n_batch, nt, d = xs.shape
    d_in = w_bf.shape[2]
    tm = _tile(nt, 768)
    tn = _tile(math.gcd(D_ATT, d_in - D_ATT), 512)
    n_att = D_ATT // tn
    assert lc % ROW_CHUNK == 0
    kern = functools.partial(_inproj_kernel, n_batch=n_batch, lc=lc, tm=tm, n_att=n_att)
    return pl.pallas_call(
        kern,
        grid=(n_batch, nt // tm, d_in // tn),
        in_specs=[pl.BlockSpec((None, tm, d), lambda b, i, j: (b, i, 0)),
                  pl.BlockSpec((1, d), lambda b, i, j: (0, 0)),
                  pl.BlockSpec((MOD_ROWS, d), lambda b, i, j: (0, 0)),
                  pl.BlockSpec((MOD_ROWS, d), lambda b, i, j: (0, 1)),
                  pl.BlockSpec((None, d, tn), lambda b, i, j: (layer, 0, j))],
        out_specs=[pl.BlockSpec((None, tm, tn), lambda b, i, j: (b, i, jnp.minimum(j, n_att - 1))),
                   pl.BlockSpec((None, tm, tn), lambda b, i, j: (b, i, jnp.maximum(j - n_att, 0)))],
        out_shape=[jax.ShapeDtypeStruct((n_batch, nt, D_ATT), BF16),
                   jax.ShapeDtypeStruct((n_batch, nt, d_in - D_ATT), F32)],
        scratch_shapes=[pltpu.VMEM((tm, d), BF16)],
        compiler_params=_params(("arbitrary", "arbitrary", "arbitrary"), 56),
        name="in_proj",
    )(xs, g, modl, modl, w_bf)


def _rms_full(x, g):
    ms = jnp.mean(x * x, axis=-1, keepdims=True)
    return x * lax.rsqrt(ms + EPS) * g


def _group_ones(width):
    r = lax.broadcasted_iota(I32, (LANE, LANE), 0) // width
    c = lax.broadcasted_iota(I32, (LANE, LANE), 1) // width
    return jnp.where(r == c, 1.0, 0.0).astype(BF16)


def _rms_groups(x, g, ones, width):
    x2 = x * x
    hi = x2.astype(BF16)
    lo = (x2 - hi.astype(F32)).astype(BF16)
    ms = (_dot(hi, ones) + _dot(lo, ones)) * (1.0 / width)
    return x * lax.rsqrt(ms + EPS) * g


def _rope(x, cos, sin_signed, half):
    lane = lax.broadcasted_iota(I32, (1, LANE), 1)
    up = pltpu.roll(x, LANE - half, axis=1)
    dn = pltpu.roll(x, half, axis=1)
    partner = jnp.where((lane & half) == 0, up, dn)
    return x * cos + partner * sin_signed


def _prep_kernel(u_ref, p_ref, cs_ref, ss_ref, cd_ref, sd_ref,
                 qa_ref, ka_ref, va_ref, qd_ref, kd_ref, vd_ref, qs_ref, ks_ref, vs_ref):
    g_na_q, g_na_k = p_ref[0:1, :], p_ref[1:2, :]
    g_d_q, g_d_k = p_ref[2:3, :], p_ref[3:4, :]
    g_s_q, g_s_k = p_ref[4:5, :], p_ref[5:6, :]
    cs, ss, cd, sd = cs_ref[...], ss_ref[...], cd_ref[...], sd_ref[...]
    lo = lax.broadcasted_iota(I32, (1, LANE), 1) < DIFF_QK_DIM

    def col(off, h):
        return u_ref[:, off + h * LANE: off + (h + 1) * LANE].astype(F32)

    def put(ref, h, val):
        ref[:, h * LANE:(h + 1) * LANE] = val.astype(BF16)

    ones_head = _group_ones(HEAD_DIM)
    ones_half = _group_ones(DIFF_QK_DIM)
    rms_head = lambda x, g: _rms_groups(x, g, ones_head, HEAD_DIM)
    rms_half = lambda x, g: _rms_groups(x, g, ones_half, DIFF_QK_DIM)

    for h in range(NA_HEADS):
        put(qa_ref, h, rms_head(col(OFF_QA, h), g_na_q) * ATTN_Q_SCALE)
        put(ka_ref, h, rms_head(col(OFF_KA, h), g_na_k))
        put(va_ref, h, col(OFF_VA, h))
    ones_col = jnp.where(lax.broadcasted_iota(I32, (u_ref.shape[0], LANE), 1) == 0, 1.0, 0.0).astype(BF16)
    for h in range(DIFF_HEADS):
        q = _rope(rms_half(col(OFF_QD, h), g_d_q), cd, sd, DIFF_QK_DIM // 4) * DIFF_Q_SCALE
        qd_ref[0, :, h * LANE:(h + 1) * LANE] = jnp.where(lo, q, 0.0).astype(BF16)
        qd_ref[1, :, h * LANE:(h + 1) * LANE] = jnp.where(lo, 0.0, q).astype(BF16)
        put(kd_ref, h, _rope(rms_half(col(OFF_KD, h), g_d_k), cd, sd, DIFF_QK_DIM // 4))
        put(vd_ref, 2 * h, col(OFF_VD, h))
        vd_ref[:, (2 * h + 1) * LANE:(2 * h + 2) * LANE] = ones_col
    for h in range(SWA_Q_HEADS):
        put(qs_ref, h, _rope(rms_head(col(OFF_QS, h), g_s_q), cs, ss, HEAD_DIM // 4) * ATTN_Q_SCALE)
    for h in range(SWA_KV_HEADS):
        put(ks_ref, h, _rope(rms_head(col(OFF_KS, h), g_s_k), cs, ss, HEAD_DIM // 4))
        put(vs_ref, h, col(OFF_VS, h))


def _prep(u, pvec, cs, ss, cd, sd):
    n_batch, nt, _ = u.shape
    tr = _tile(nt, 256)
    row = lambda w: pl.BlockSpec((None, tr, w), lambda b, i: (b, i, 0))
    tab = pl.BlockSpec((tr, LANE), lambda b, i: (i, 0))
    sd_ = lambda w: jax.ShapeDtypeStruct((n_batch, nt, w), BF16)
    return pl.pallas_call(
        _prep_kernel,
        grid=(n_batch, nt // tr),
        in_specs=[row(D_ATT), pl.BlockSpec((8, LANE), lambda b, i: (0, 0)), tab, tab, tab, tab],
        out_specs=[row(1024), row(1024), row(1024),
                   pl.BlockSpec((None, 2, tr, 1024), lambda b, i: (b, 0, i, 0)),
                   row(1024), row(2048), row(1024), row(256), row(256)],
        out_shape=[sd_(1024), sd_(1024), sd_(1024),
                   jax.ShapeDtypeStruct((n_batch, 2, nt, 1024), BF16),
                   sd_(1024), sd_(2048), sd_(1024), sd_(256), sd_(256)],
        compiler_params=_params(("arbitrary", "arbitrary"), 48),
        name="qkv_prep",
    )(u, pvec, cs, ss, cd, sd)


NA_QBLK = 2 * GRID_W
NA_KROWS = 10
NA_KBLK = NA_KROWS * GRID_W


def _skewed(n_chains, scores, numerators, output):
    s_prev = p_prev = None
    for c in range(n_chains + 2):
        s_new = scores(c) if c < n_chains else None
        p_new = numerators(c - 1, s_prev) if s_prev is not None else None
        if p_prev is not None:
            output(c - 2, p_prev)
        s_prev, p_prev = s_new, p_new


def _na_kernel(sc_ref, q_ref, k_ref, v_ref, *rest, lc, nlb, group, online):
    bias_refs, o_ref = rest[:group], rest[group]
    i0 = pl.program_id(2) * group
    kc = k_ref[0:lc, :]
    vc = v_ref[0:lc, :]

    def window_start(g):
        li = i0 + g - lc // NA_QBLK
        kb = jnp.clip(li - 2, 0, nlb - NA_KROWS // 2)
        return pl.multiple_of(lc + kb * NA_QBLK, NA_QBLK)

    def scores(g):
        q = q_ref[g * NA_QBLK:(g + 1) * NA_QBLK, :]
        return _dot_nt(q, k_ref[pl.ds(window_start(g), NA_KBLK), :]) + bias_refs[g][...], _dot_nt(q, kc)

    def numerators(g, s):
        s_n, s_c = s
        if online:
            m = jnp.maximum(jnp.max(s_n, axis=-1, keepdims=True), jnp.max(s_c, axis=-1, keepdims=True))
        else:
            m = sc_ref[0]
        p_n = jnp.exp2(s_n - m)
        p_c = jnp.exp2(s_c - m)
        l = jnp.sum(p_n, axis=-1, keepdims=True) + jnp.sum(p_c, axis=-1, keepdims=True)
        return p_n.astype(BF16), p_c.astype(BF16), l

    def output(g, p):
        o = _dot(p[0], v_ref[pl.ds(window_start(g), NA_KBLK), :]) + _dot(p[1], vc)
        o_ref[g * NA_QBLK:(g + 1) * NA_QBLK, :] = (o / p[2]).astype(BF16)

    _skewed(group, scores, numerators, output)


def _na_bias_tables(rpb, n):
    rows = n // GRID_W
    nlb = rows // 2
    reps = [0, 1, 2, nlb - 2, nlb - 1]
    h = rpb.shape[0]
    cols = np.arange(GRID_W)
    cstart = np.clip(cols - NA_KW // 2, 0, GRID_W - NA_KW)
    col_ok = (cols[None, :] >= cstart[:, None]) & (cols[None, :] < cstart[:, None] + NA_KW)
    dx = cols[None, :] - cols[:, None] + (NA_KW - 1)
    sel = (dx[None] == np.arange(2 * NA_KW - 1)[:, None, None]) & col_ok[None]
    tx = jnp.einsum('hyx,xqk->hyqk', rpb.astype(F32), jnp.asarray(sel, F32), precision=lax.Precision.HIGHEST)
    dy_i = np.zeros((5, 2, NA_KROWS), np.int32)
    row_ok = np.zeros((5, 2, NA_KROWS), bool)
    for c, li in enumerate(reps):
        kb = int(np.clip(li - 2, 0, nlb - NA_KROWS // 2))
        qr = 2 * li + np.arange(2)
        kr = 2 * kb + np.arange(NA_KROWS)
        rs = np.clip(qr - NA_KH // 2, 0, rows - NA_KH)
        row_ok[c] = (kr[None, :] >= rs[:, None]) & (kr[None, :] < rs[:, None] + NA_KH)
        dy_i[c] = np.clip(kr[None, :] - qr[:, None] + (NA_KH - 1), 0, 2 * NA_KH - 2)
    blocks = jnp.take(tx, jnp.asarray(dy_i.reshape(-1)), axis=1)
    blocks = blocks.reshape(h, 5, 2, NA_KROWS, GRID_W, GRID_W)
    ok = row_ok[:, :, :, None, None] & col_ok[None, None, None]
    tbl = jnp.where(ok[None], blocks * LOG2E, NEG_INF)
    tbl = tbl.transpose(1, 0, 2, 4, 3, 5).reshape(5, h, NA_QBLK, NA_KBLK)
    dead = jnp.full((1,) + tbl.shape[1:], NEG_INF, F32)
    return jnp.concatenate([tbl, dead], axis=0)


def _group(nblocks, target):
    return max(g for g in range(1, target + 1) if nblocks % g == 0)


def _softmax_dispatch(bound, call):
    sc = jnp.reshape(bound, (1,)).astype(F32)
    return lax.cond(bound <= ATTN_FAST_BOUND, lambda s: call(False)(s), lambda s: call(True)(s), sc)


def _na(bound, qa, ka, va, bias, lc):
    n_batch, nt, _ = qa.shape
    n = nt - lc
    nlb = n // NA_QBLK
    assert nlb >= 5 and lc % NA_QBLK == 0
    ncb = lc // NA_QBLK
    group = _group(nt // NA_QBLK, 11)

    def case(i):
        li = i - ncb
        return jnp.where(li < 0, 5, jnp.where(li < 2, li, jnp.where(li >= nlb - 2, li - (nlb - 5), 2)))

    kv = pl.BlockSpec((None, nt, LANE), lambda b, h, i: (b, 0, h))
    qo = pl.BlockSpec((None, group * NA_QBLK, LANE), lambda b, h, i: (b, i, h))
    bias_specs = [pl.BlockSpec((None, None, NA_QBLK, NA_KBLK),
                               functools.partial(lambda b, h, i, g: (case(i * group + g), h, 0, 0), g=g))
                  for g in range(group)]

    def call(online):
        kern = functools.partial(_na_kernel, lc=lc, nlb=nlb, group=group, online=online)
        return lambda sc: pl.pallas_call(
            kern,
            grid=(n_batch, NA_HEADS, nt // (group * NA_QBLK)),
            in_specs=[pl.BlockSpec(memory_space=pltpu.SMEM), qo, kv, kv] + bias_specs,
            out_specs=qo,
            out_shape=jax.ShapeDtypeStruct((n_batch, nt, NA_HEADS * HEAD_DIM), BF16),
            compiler_params=_params(("arbitrary",) * 3, 40),
            name="na_attn_online" if online else "na_attn_offset",
        )(sc, qa, ka, va, *([bias] * group))

    return _softmax_dispatch(bound, call)


SWA_BLK = 128
SWA_WIN = 3 * SWA_BLK


def _swa_kernel(sc_ref, q_ref, k_ref, v_ref, sink_ref, *rest, lc, n, group, online):
    mask_refs, o_ref = rest[:group], rest[group]
    i0 = pl.program_id(2) * group
    kc = k_ref[0:lc, :]
    vc = v_ref[0:lc, :]

    def window_start(c):
        li = i0 + c // SWA_GROUP - lc // SWA_BLK
        ws = jnp.clip((li - 1) * SWA_BLK, 0, n - SWA_WIN)
        return pl.multiple_of(lc + ws, SWA_BLK)

    def block(c):
        gq, g = divmod(c, SWA_GROUP)
        return slice(gq * SWA_BLK, (gq + 1) * SWA_BLK), slice(g * LANE, (g + 1) * LANE)

    def scores(c):
        rows, cols = block(c)
        q = q_ref[rows, cols]
        return _dot_nt(q, k_ref[pl.ds(window_start(c), SWA_WIN), :]), _dot_nt(q, kc)

    def numerators(c, s):
        s_w, s_c = s
        g = c % SWA_GROUP
        mask = mask_refs[c // SWA_GROUP][...]
        sink = sink_ref[g * SWA_BLK:(g + 1) * SWA_BLK, 0:1]
        if online:
            s_w = jnp.where(mask > 0.0, s_w, NEG_INF)
            m = jnp.maximum(jnp.maximum(jnp.max(s_w, axis=-1, keepdims=True),
                                        jnp.max(s_c, axis=-1, keepdims=True)), sink)
            p_w = jnp.exp2(s_w - m)
        else:
            m = sc_ref[0]
            p_w = jnp.exp2(s_w - m) * mask
        p_c = jnp.exp2(s_c - m)
        l = jnp.sum(p_w, axis=-1, keepdims=True) + jnp.sum(p_c, axis=-1, keepdims=True) + jnp.exp2(sink - m)
        return p_w.astype(BF16), p_c.astype(BF16), l

    def output(c, p):
        rows, cols = block(c)
        o = (_dot(p[0], v_ref[pl.ds(window_start(c), SWA_WIN), :]) + _dot(p[1], vc)) / p[2]
        o_ref[rows, cols] = o.astype(BF16)

    _skewed(group * SWA_GROUP, scores, numerators, output)


def _swa_masks():
    r = np.arange(SWA_BLK)[:, None]
    j = np.arange(SWA_WIN)[None, :]
    rel = [j - r, j - r - SWA_BLK, j - r - 2 * SWA_BLK]
    tabs = [(np.abs(d) <= SWA_WINDOW) for d in rel] + [np.zeros((SWA_BLK, SWA_WIN), bool)]
    return jnp.asarray(np.stack(tabs), F32)


def _swa(bound, qs, ks, vs, sink, lc):
    n_batch, nt, _ = qs.shape
    n = nt - lc
    nb = n // SWA_BLK
    assert nb >= 3 and lc % SWA_BLK == 0
    ncb = lc // SWA_BLK
    rows = SWA_GROUP * SWA_BLK
    group = _group(nt // SWA_BLK, 6)
    sink_rows = jnp.broadcast_to(
        jnp.repeat((sink.astype(F32) * LOG2E).reshape(SWA_KV_HEADS, SWA_GROUP), SWA_BLK, axis=1)[:, :, None],
        (SWA_KV_HEADS, rows, LANE))
    masks = _swa_masks()

    def case(i):
        li = i - ncb
        return jnp.where(li < 0, 3, jnp.where(li == 0, 0, jnp.where(li == nb - 1, 2, 1)))

    kv = pl.BlockSpec((None, nt, LANE), lambda b, k, i: (b, 0, k))
    qo = pl.BlockSpec((None, group * SWA_BLK, SWA_GROUP * LANE), lambda b, k, i: (b, i, k))
    mask_specs = [pl.BlockSpec((None, SWA_BLK, SWA_WIN),
                               functools.partial(lambda b, k, i, g: (case(i * group + g), 0, 0), g=g))
                  for g in range(group)]

    def call(online):
        kern = functools.partial(_swa_kernel, lc=lc, n=n, group=group, online=online)
        return lambda sc: pl.pallas_call(
            kern,
            grid=(n_batch, SWA_KV_HEADS, nt // (group * SWA_BLK)),
            in_specs=[pl.BlockSpec(memory_space=pltpu.SMEM), qo, kv, kv,
                      pl.BlockSpec((None, rows, LANE), lambda b, k, i: (k, 0, 0))] + mask_specs,
            out_specs=qo,
            out_shape=jax.ShapeDtypeStruct((n_batch, nt, SWA_Q_HEADS * HEAD_DIM), BF16),
            compiler_params=_params(("arbitrary",) * 3, 32),
            name="swa_attn_online" if online else "swa_attn_offset",
        )(sc, qs, ks, vs, sink_rows, *([masks] * group))

    return _softmax_dispatch(bound, call)


def _diff_finish(sc_ref, g_ref, o_ref, num, den, tq, post_scale):
    o2 = num / den
    o = o2[0:tq, :] - sc_ref[0] * o2[tq:2 * tq, :]
    o_ref[...] = (_rms_full(o, g_ref[6:7, :]) * post_scale).astype(BF16)


def _diff_online_kernel(sc_ref, q_ref, k_ref, v_ref, g_ref, o_ref, *, lc, nt, tq, tk, post_scale):
    qi = pl.program_id(2)
    q = q_ref[...].reshape(2 * tq, LANE)
    nk = jnp.where(qi < lc // tq, lc // tk, nt // tk)

    def body(kt, carry):
        m, l, acc = carry
        k0 = pl.multiple_of(kt * tk, tk)
        k = k_ref[pl.ds(k0, tk), :]
        v = v_ref[pl.ds(k0, tk), :]
        s = _dot_nt(q, k)
        m_new = jnp.maximum(m, jnp.max(s, axis=-1, keepdims=True))
        alpha = jnp.exp2(m - m_new)
        p = jnp.exp2(s - m_new)
        l = alpha * l + jnp.sum(p, axis=-1, keepdims=True)
        acc = alpha * acc + _dot(p.astype(BF16), v)
        return m_new, l, acc

    init = (jnp.full((2 * tq, 1), NEG_INF, F32), jnp.zeros((2 * tq, 1), F32), jnp.zeros((2 * tq, LANE), F32))
    _, l, acc = lax.fori_loop(0, nk, body, init)
    _diff_finish(sc_ref, g_ref, o_ref, acc, l, tq, post_scale)


def _diff_offset_kernel(sc_ref, q_ref, qn_ref, k_ref, v_ref, g_ref, o_ref, acc_ref, p_ref,
                        *, lc, nt, tq, tk, post_scale):
    qi = pl.program_id(2)
    rows2 = 2 * tq
    nk = nt // tk
    ncq = lc // tq
    q = q_ref[...].reshape(rows2, LANE)
    off = sc_ref[1]

    def probs(qq, k0, tkk):
        return jnp.exp2(_dot_nt(qq, k_ref[k0:k0 + tkk, :]) - off).astype(BF16)

    def next_tile_probs():
        return probs(qn_ref[...].reshape(rows2, LANE), 0, tk)

    @pl.when(qi < ncq)
    def _():
        acc_ref[...] = _dot(probs(q, 0, lc), v_ref[0:lc, :])

        @pl.when(qi == ncq - 1)
        def _():
            p_ref[0] = next_tile_probs()

    def latent(first_slot):
        tot = None
        for t in range(nk):
            pv = _dot(p_ref[(t + first_slot) % 2], v_ref[t * tk:(t + 1) * tk, :])
            p_ref[(t + 1 + first_slot) % 2] = probs(q, (t + 1) * tk, tk) if t + 1 < nk else next_tile_probs()
            tot = pv if tot is None else tot + pv
        acc_ref[...] = tot

    first_slot = ((qi - ncq) * nk) % 2
    for slot in range(2 if nk % 2 else 1):
        @pl.when((qi >= ncq) & (first_slot == slot))
        def _():
            latent(slot)

    acc = acc_ref[...]
    _diff_finish(sc_ref, g_ref, o_ref, acc[:, 0:LANE], acc[:, LANE:LANE + 1], tq, post_scale)


def _diff(sc, qd, kd, vde, pvec, lc, post_scale):
    n_batch, _, nt, _ = qd.shape
    tq = _tile(math.gcd(lc, nt), 256)
    assert lc % tq == 0
    grid = (n_batch, DIFF_HEADS, nt // tq)
    smem = pl.BlockSpec(memory_space=pltpu.SMEM)
    qspec = pl.BlockSpec((None, 2, tq, LANE), lambda b, h, i: (b, 0, i, h))
    kspec = pl.BlockSpec((None, nt, LANE), lambda b, h, i: (b, 0, h))
    gspec = pl.BlockSpec((8, LANE), lambda b, h, i: (0, 0))
    ospec = pl.BlockSpec((None, tq, LANE), lambda b, h, i: (b, i, h))
    oshape = jax.ShapeDtypeStruct((n_batch, nt, DIFF_HEADS * HEAD_DIM), BF16)

    def offset(sc, qd, kd, vde, pvec):
        tk = _tile(nt, 768)
        kern = functools.partial(_diff_offset_kernel, lc=lc, nt=nt, tq=tq, tk=tk, post_scale=post_scale)
        nq = nt // tq
        qnext = pl.BlockSpec((None, 2, tq, LANE), lambda b, h, i: (b, 0, jnp.minimum(i + 1, nq - 1), h))
        return pl.pallas_call(
            kern, grid=grid,
            in_specs=[smem, qspec, qnext, kspec, pl.BlockSpec((None, nt, 2 * LANE), lambda b, h, i: (b, 0, h)),
                      gspec],
            out_specs=ospec, out_shape=oshape,
            scratch_shapes=[pltpu.VMEM((2 * tq, 2 * LANE), F32), pltpu.VMEM((2, 2 * tq, tk), BF16)],
            compiler_params=_params(("arbitrary",) * 3, 40),
            name="diff_attn_offset",
        )(sc, qd, qd, kd, vde, pvec)

    def online(sc, qd, kd, vde, pvec):
        tk = _tile(math.gcd(lc, nt), 256)
        kern = functools.partial(_diff_online_kernel, lc=lc, nt=nt, tq=tq, tk=tk, post_scale=post_scale)
        return pl.pallas_call(
            kern, grid=grid,
            in_specs=[smem, qspec, kspec, pl.BlockSpec((None, nt, LANE), lambda b, h, i: (b, 0, 2 * h)), gspec],
            out_specs=ospec, out_shape=oshape,
            compiler_params=_params(("arbitrary",) * 3, 32),
            name="diff_attn_online",
        )(sc, qd, kd, vde, pvec)

    return lax.cond(sc[1] <= ATTN_FAST_BOUND, offset, online, sc, qd, kd, vde, pvec)


LRU_HALO = 8


LRU_STEP_BLOCKS = 4


def _lru_gate_kernel(xp_ref, x_ref, xn_ref, cw_ref, cb_ref, gw_ref, gb_ref, lam_ref,
                     af_ref, bf_ref, ab_ref, bb_ref, *, t, seg_starts, seg_ends):
    k = pl.program_id(2)
    is_start = functools.reduce(jnp.logical_or, [k == s for s in seg_starts])
    is_end = functools.reduce(jnp.logical_or, [k == s for s in seg_ends])
    h = LRU_HALO
    outs = ((af_ref, bf_ref), (ab_ref, bb_ref))
    sigmoid = lambda z: 0.5 + 0.5 * jnp.tanh(0.5 * z)
    for c in range(LRU_STEP_BLOCKS):
        cs = slice(c * LANE, (c + 1) * LANE)
        cur = x_ref[:, cs]
        prev = jnp.where(is_start, 0.0, xp_ref[:, cs])
        nxt = jnp.where(is_end, 0.0, xn_ref[:, cs])
        ext = jnp.concatenate([prev, cur, nxt], axis=0)
        xl = (cb_ref[:, cs] + cw_ref[0:1, cs] * ext[h - 2:h - 2 + t, :] + cw_ref[1:2, cs] * ext[h - 1:h - 1 + t, :]
              + cw_ref[2:3, cs] * cur + cw_ref[3:4, cs] * ext[h + 1:h + 1 + t, :])
        xb = xl.astype(BF16)
        for d in range(2):
            r = sigmoid(_dot(xb, gw_ref[d, 0, c].astype(BF16)) + gb_ref[2 * d:2 * d + 1, cs])
            ig = sigmoid(_dot(xb, gw_ref[d, 1, c].astype(BF16)) + gb_ref[2 * d + 1:2 * d + 2, cs])
            a = jnp.exp(LRU_C * r * jax.nn.log_sigmoid(lam_ref[d:d + 1, cs]))
            outs[d][0][:, cs] = a
            outs[d][1][:, cs] = jnp.sqrt(1.0 - a * a) * (ig * xl)


def _lru_gates(u, conv_w, conv_b, gate_w, gate_b, lam, lc):
    n_batch, nt, _ = u.shape
    t = _tile(math.gcd(lc, nt), 256)
    nk = nt // t
    hb = t // LRU_HALO
    seg_starts = (0, lc // t)
    seg_ends = (lc // t - 1, nk - 1)
    w = LRU_STEP_BLOCKS * LANE
    assert OFF_XR % w == 0 and LRU_WIDTH % w == 0
    cb0 = (OFF_XR - D_ATT) // w
    kern = functools.partial(_lru_gate_kernel, t=t, seg_starts=seg_starts, seg_ends=seg_ends)
    out = pl.BlockSpec((None, t, w), lambda b, c, k: (b, k, c))
    osd = jax.ShapeDtypeStruct((n_batch, nt, LRU_WIDTH), F32)
    return pl.pallas_call(
        kern,
        grid=(n_batch, LRU_WIDTH // w, nk),
        in_specs=[pl.BlockSpec((None, LRU_HALO, w), lambda b, c, k: (b, jnp.maximum(k * hb - 1, 0), cb0 + c)),
                  pl.BlockSpec((None, t, w), lambda b, c, k: (b, k, cb0 + c)),
                  pl.BlockSpec((None, LRU_HALO, w),
                               lambda b, c, k: (b, jnp.minimum((k + 1) * hb, nt // LRU_HALO - 1), cb0 + c)),
                  pl.BlockSpec((4, w), lambda b, c, k: (0, c)),
                  pl.BlockSpec((1, w), lambda b, c, k: (0, c)),
                  pl.BlockSpec((2, 2, LRU_STEP_BLOCKS, LANE, LANE), lambda b, c, k: (0, 0, c, 0, 0)),
                  pl.BlockSpec((4, w), lambda b, c, k: (0, c)),
                  pl.BlockSpec((2, w), lambda b, c, k: (0, c))],
        out_specs=[out, out, out, out],
        out_shape=[osd, osd, osd, osd],
        compiler_params=_params(("arbitrary",) * 3, 24),
        name="lru_gates",
    )(u, u, u, conv_w, conv_b.reshape(1, LRU_WIDTH), gate_w, gate_b.reshape(4, LRU_WIDTH), lam)


def _lru_scan_kernel(af_ref, bf_ref, ab_ref, bb_ref, hf_ref, hb_ref, cf_ref, cbk_ref, *, t):
    @pl.when(pl.program_id(1) == 0)
    def _():
        cf_ref[...] = jnp.zeros_like(cf_ref)
        cbk_ref[...] = jnp.zeros_like(cbk_ref)

    ng = t // 8

    def body(g, carry):
        hf, hb = carry
        r0 = pl.multiple_of(g * 8, 8)
        a = af_ref[pl.ds(r0, 8), :]
        bt = bf_ref[pl.ds(r0, 8), :]
        rows = []
        for r in range(8):
            hf = a[r:r + 1, :] * hf + bt[r:r + 1, :]
            rows.append(hf)
        hf_ref[pl.ds(r0, 8), :] = jnp.concatenate(rows, axis=0)
        r1 = pl.multiple_of((ng - 1 - g) * 8, 8)
        a = ab_ref[pl.ds(r1, 8), :]
        bt = bb_ref[pl.ds(r1, 8), :]
        rows = []
        for r in range(7, -1, -1):
            hb = a[r:r + 1, :] * hb + bt[r:r + 1, :]
            rows.append(hb)
        hb_ref[pl.ds(r1, 8), :] = jnp.concatenate(rows[::-1], axis=0)
        return hf, hb

    hf, hb = lax.fori_loop(0, ng, body, (cf_ref[...], cbk_ref[...]))
    cf_ref[...] = hf
    cbk_ref[...] = hb


def _lru_scan(af, bf, ab, bb, lc):
    n_batch, nt, w = af.shape
    t = lc
    assert nt % t == 0
    nk = nt // t
    kern = functools.partial(_lru_scan_kernel, t=t)
    fwd = pl.BlockSpec((None, t, w), lambda b, k: (b, k, 0))
    bwd = pl.BlockSpec((None, t, w), lambda b, k: (b, jnp.where(k == 0, 0, nk - k), 0))
    osd = jax.ShapeDtypeStruct((n_batch, nt, w), F32)
    return pl.pallas_call(
        kern,
        grid=(n_batch, nk),
        in_specs=[fwd, fwd, bwd, bwd],
        out_specs=[fwd, bwd],
        out_shape=[osd, osd],
        scratch_shapes=[pltpu.VMEM((1, w), F32), pltpu.VMEM((1, w), F32)],
        compiler_params=_params(("arbitrary", "arbitrary"), 24),
        name="lru_scan",
    )(af, bf, ab, bb)


def _outproj_kernel(oa_ref, od_ref, os_ref, hf_ref, hb_ref, g0_ref, g1_ref, gate_ref, w_ref, x_ref,
                    o_ref, lhs_ref, *, n_batch, lc, tm):
    b = pl.program_id(0)
    i = pl.program_id(1)

    @pl.when(pl.program_id(2) == 0)
    def _():
        def chunk(r0):
            rs = pl.ds(r0, ROW_CHUNK)
            lhs_ref[rs, 0:1024] = oa_ref[rs, :]
            lhs_ref[rs, 1024:2048] = od_ref[rs, :]
            lhs_ref[rs, 2048:3072] = os_ref[rs, :]
            for c, g_ref in enumerate((g0_ref, g1_ref)):
                cs = slice(c * 512, (c + 1) * 512)
                y = (hf_ref[rs, cs] + hb_ref[rs, cs]) * jax.nn.gelu(g_ref[rs, :])
                lhs_ref[rs, 3072 + c * 512:3584 + c * 512] = y.astype(BF16)

        _for_row_chunks(tm, chunk)

    row = i * tm + lax.broadcasted_iota(I32, (tm, 1), 0)
    gate = jnp.where(row < lc, gate_ref[pl.ds(n_batch, 1), :], gate_ref[pl.ds(b, 1), :])
    o_ref[...] = x_ref[...] + gate * _dot(lhs_ref[...], w_ref[...])


def _outproj(o_a, o_d, o_s, hf, hb, u, modl, w_bf, layer, xs, lc):
    n_batch, nt, d = xs.shape
    tm = _tile(nt, 768)
    tn = _tile(d, 512)
    ntile = d // tn
    kern = functools.partial(_outproj_kernel, n_batch=n_batch, lc=lc, tm=tm)
    mix = pl.BlockSpec((None, tm, 1024), lambda b, i, j: (b, i, 0))
    gr = lambda c: pl.BlockSpec((None, tm, 512), lambda b, i, j: (b, i, (OFF_GR - D_ATT) // 512 + c))
    xblk = pl.BlockSpec((None, tm, tn), lambda b, i, j: (b, i, j))
    return pl.pallas_call(
        kern,
        grid=(n_batch, nt // tm, ntile),
        in_specs=[mix, mix, mix, mix, mix, gr(0), gr(1),
                  pl.BlockSpec((MOD_ROWS, tn), lambda b, i, j: (0, 2 * ntile + j)),
                  pl.BlockSpec((None, D_MIX, tn), lambda b, i, j: (layer, 0, j)), xblk],
        out_specs=xblk,
        out_shape=jax.ShapeDtypeStruct(xs.shape, F32),
        scratch_shapes=[pltpu.VMEM((tm, D_MIX), BF16)],
        input_output_aliases={9: 0},
        compiler_params=_params(("arbitrary",) * 3, 56),
        name="out_proj",
    )(o_a, o_d, o_s, hf, hb, u, u, modl, w_bf, xs)


def _split2(x):
    hi = x.astype(BF16)
    return hi, (x - hi.astype(F32)).astype(BF16)


ROUTER_KC = 512


HI16 = -65536


def _pack_bf16_pair(lo, hi):
    bits = lambda v: lax.bitcast_convert_type(v.astype(BF16).astype(F32), I32)
    return (bits(hi) & HI16) | lax.shift_right_logical(bits(lo), 16)


def _unpack_bf16_pair(w):
    lo = lax.bitcast_convert_type(lax.shift_left(w, 16), F32)
    hi = lax.bitcast_convert_type(w & HI16, F32)
    return lo.astype(BF16), hi.astype(BF16)


def _router_kernel(x_ref, g_ref, sh_ref, sc_ref, w0_ref, w1_ref, xm_ref, aff_ref, xf_ref, *, n_batch, lc, tm):
    b = pl.program_id(0)
    i = pl.program_id(1)
    d = x_ref.shape[1]
    half = d // 2

    def chunk(r0):
        rs = pl.ds(r0, ROW_CHUNK)
        sh = _mod_row(sh_ref, i * tm + r0, b, n_batch, lc)
        sc = _mod_row(sc_ref, i * tm + r0, b, n_batch, lc)
        xm = _norm_modulate(x_ref[rs, :], g_ref[...], sh, sc)
        xf_ref[rs, :] = xm
        xm_ref[rs, 0:half] = _pack_bf16_pair(xm[:, 0:half], xm[:, half:d])

    _for_row_chunks(tm, chunk)

    logits = jnp.zeros((tm, LANE), F32)
    tk = _tile(d, ROUTER_KC)
    for kc in range(d // tk):
        ks = slice(kc * tk, (kc + 1) * tk)
        x0, x1 = _split2(xf_ref[:, ks])
        logits = logits + (_dot(x0, w0_ref[ks, :]) + (_dot(x0, w1_ref[ks, :]) + _dot(x1, w0_ref[ks, :])))
    lane = lax.broadcasted_iota(I32, (1, LANE), 1)
    logits = jnp.where(lane < N_EXPERTS, logits, NEG_INF)
    m = jnp.max(logits, axis=-1, keepdims=True)
    p = jnp.exp(logits - m)
    aff = p / jnp.sum(p, axis=-1, keepdims=True)
    aff_ref[...] = aff
    xm_ref[:, half:half + LANE] = lax.bitcast_convert_type(aff, I32)


def _router(xs, g, modl, wr_pad, lc):
    n_batch, nt, d = xs.shape
    tm = _tile(nt, 256)
    assert lc % ROW_CHUNK == 0
    kern = functools.partial(_router_kernel, n_batch=n_batch, lc=lc, tm=tm)
    wspec = pl.BlockSpec((d, LANE), lambda b, i: (0, 0))
    return pl.pallas_call(
        kern,
        grid=(n_batch, nt // tm),
        in_specs=[pl.BlockSpec((None, tm, d), lambda b, i: (b, i, 0)),
                  pl.BlockSpec((1, d), lambda b, i: (0, 0)),
                  pl.BlockSpec((MOD_ROWS, d), lambda b, i: (0, 3)),
                  pl.BlockSpec((MOD_ROWS, d), lambda b, i: (0, 4)),
                  wspec, wspec],
        out_specs=[pl.BlockSpec((None, tm, d // 2 + LANE), lambda b, i: (b, i, 0)),
                   pl.BlockSpec((None, tm, LANE), lambda b, i: (b, i, 0))],
        out_shape=[jax.ShapeDtypeStruct((n_batch, nt, d // 2 + LANE), I32),
                   jax.ShapeDtypeStruct((n_batch, nt, LANE), F32)],
        scratch_shapes=[pltpu.VMEM((tm, d), F32)],
        compiler_params=_params(("arbitrary", "arbitrary"), 48),
        name="norm2_router",
    )(xs, g, modl, modl, *_split2(wr_pad))


def _select_kernel(aff_ref, l_ref, cnt_ref, off_ref, bits_ref, *, row0, n, cap):
    nc = n // LANE
    bits_ref[...] = lax.bitcast_convert_type(aff_ref[row0:row0 + n, :], I32)

    def bit_step(i, thr):
        cand = thr | (jnp.int32(1) << (30 - i))
        cnt = jnp.sum((bits_ref[...] >= cand).astype(I32), axis=0, keepdims=True)
        return jnp.where(cnt >= cap, cand, thr)

    thr = lax.fori_loop(0, 31, bit_step, jnp.zeros((1, LANE), I32))
    n_gt = jnp.sum((bits_ref[...] > thr).astype(I32), axis=0, keepdims=True)
    need = (cap - n_gt).astype(F32)

    r_io = lax.broadcasted_iota(I32, (LANE, LANE), 0)
    c_io = lax.broadcasted_iota(I32, (LANE, LANE), 1)
    ltri = (c_io <= r_io).astype(BF16)
    t_col = r_io.astype(F32)
    j_row = c_io.astype(F32)

    def chunk(c, carry):
        run_eq, run_sel = carry
        r0 = pl.multiple_of(c * LANE, LANE)
        bc = bits_ref[pl.ds(r0, LANE), :]
        eq = bc == thr
        eq_f = eq.astype(F32)
        incl_eq = _dot(ltri, eq_f.astype(BF16))
        rank = run_eq + incl_eq - eq_f
        sel = (bc > thr) | (eq & (rank < need))
        sel_f = sel.astype(F32)
        incl = _dot(ltri, sel_f.astype(BF16))
        cnt_c = incl[LANE - 1:LANE, :]
        qm = jnp.where(sel, incl - sel_f, -1.0)
        for e in range(N_EXPERTS):
            pos = jnp.broadcast_to(qm[:, e:e + 1], (LANE, LANE))
            local = jnp.sum(jnp.where(pos == j_row, t_col, 0.0), axis=0, keepdims=True)
            l_ref[e, pl.ds(c, 1), :] = local.astype(I32)
        cnt_ref[pl.ds(c, 1), :] = cnt_c.astype(I32)
        off_ref[pl.ds(c, 1), :] = run_sel.astype(I32)
        return run_eq + incl_eq[LANE - 1:LANE, :], run_sel + cnt_c

    zero = jnp.zeros((1, LANE), F32)
    lax.fori_loop(0, nc, chunk, (zero, zero))


def _select(affp, row0, n, cap):
    n_batch, nt, _ = affp.shape
    nc = n // LANE
    kern = functools.partial(_select_kernel, row0=row0, n=n, cap=cap)
    return pl.pallas_call(
        kern,
        grid=(n_batch,),
        in_specs=[pl.BlockSpec((None, nt, LANE), lambda b: (b, 0, 0))],
        out_specs=[pl.BlockSpec((None, N_EXPERTS, nc, LANE), lambda b: (b, 0, 0, 0)),
                   pl.BlockSpec((None, nc, LANE), lambda b: (b, 0, 0)),
                   pl.BlockSpec((None, nc, LANE), lambda b: (b, 0, 0))],
        out_shape=[jax.ShapeDtypeStruct((n_batch, N_EXPERTS, nc, LANE), I32),
                   jax.ShapeDtypeStruct((n_batch, nc, LANE), I32),
                   jax.ShapeDtypeStruct((n_batch, nc, LANE), I32)],
        scratch_shapes=[pltpu.VMEM((n, LANE), I32)],
        compiler_params=_params(("arbitrary",), 40),
        name="ec_select",
    )(affp)


MOE_ROWS = 256


def _moe_kernel(*refs, sets, d):
    ns = len(sets)
    tab_hbm = refs[0:3 * ns]
    xm_hbm, _, g2_ref, w1_ref, w3_ref, w2_ref, xs_hbm = refs[3 * ns:3 * ns + 7]
    tab_s = refs[3 * ns + 7:6 * ns + 7]
    idx_s, xg, rows, xb, hdn_s, sem = refs[6 * ns + 7:]
    total = sum(cap for _, _, cap, _ in sets)
    b = pl.program_id(0)
    e = pl.program_id(1)
    step = b * pl.num_programs(1) + e
    last_step = pl.num_programs(0) * pl.num_programs(1) - 1

    tables = []
    for k in range(ns):
        l_hbm, cnt_hbm, off_hbm = tab_hbm[3 * k:3 * k + 3]
        l_s, cnt_s, off_s = tab_s[3 * k:3 * k + 3]
        tables += [pltpu.make_async_copy(l_hbm.at[b, e], l_s, sem.at[3 + 3 * k]),
                   pltpu.make_async_copy(cnt_hbm.at[b], cnt_s, sem.at[4 + 3 * k]),
                   pltpu.make_async_copy(off_hbm.at[b], off_s, sem.at[5 + 3 * k])]
    for cp in tables:
        cp.start()
    for cp in tables:
        cp.wait()

    def x_gather(tok, slot, n_rows=1):
        return pltpu.make_async_copy(xm_hbm.at[b, pl.ds(tok, n_rows), :], xg.at[pl.ds(slot, n_rows), :], sem.at[0])

    def row_gather(tok, slot, n_rows=1):
        return pltpu.make_async_copy(xs_hbm.at[b, pl.ds(tok, n_rows), :], rows.at[pl.ds(slot, n_rows), :], sem.at[1])

    def row_scatter(tok, slot, n_rows=1):
        return pltpu.make_async_copy(rows.at[pl.ds(slot, n_rows), :], xs_hbm.at[b, pl.ds(tok, n_rows), :], sem.at[2])

    slot0 = 0
    for k, (row0, nc, cap, _) in enumerate(sets):
        l_s, cnt_s, off_s = tab_s[3 * k:3 * k + 3]

        def chunk(c, _, row0=row0, l_s=l_s, cnt_s=cnt_s, off_s=off_s, slot0=slot0):
            base = row0 + c * LANE
            off_c = slot0 + off_s[c, e]

            def one(j, _):
                idx_s[off_c + j] = base + l_s[c, j]
                return 0

            lax.fori_loop(0, cnt_s[c, e], one, 0)
            return 0

        lax.fori_loop(0, nc, chunk, 0)
        slot0 += cap

    def start_all(copy):
        def one(s, _):
            copy(idx_s[s], s).start()
            return 0

        lax.fori_loop(0, total, one, 0, unroll=8)

    start_all(x_gather)

    @pl.when(step > 0)
    def _():
        row_scatter(0, 0, total).wait()

    start_all(row_gather)
    x_gather(0, 0, total).wait()

    tn = _tile(d, 512)
    half = d // 2
    th = _tile(half, 512)
    lane = lax.broadcasted_iota(I32, (1, LANE), 1)

    def hidden(rs, size):
        for kc in range(half // th):
            ks = slice(kc * th, (kc + 1) * th)
            lo, hi = _unpack_bf16_pair(xg[rs, ks])
            xb[0:size, ks] = lo
            xb[0:size, half + kc * th:half + (kc + 1) * th] = hi
        h1 = _dot(xb[0:size, :], w1_ref[...])
        h3 = _dot(xb[0:size, :], w3_ref[...])
        aff = lax.bitcast_convert_type(xg[rs, half:half + LANE], F32)
        gate = jnp.sum(jnp.where(lane == e, aff, 0.0), axis=-1, keepdims=True)
        hdn_s[rs, :] = (h1 * jax.nn.sigmoid(h1) * h3 * gate).astype(BF16)

    def down(rs, g_row):
        g2_row = g_row if g_row is not None else b
        for c in range(d // tn):
            sl = slice(c * tn, (c + 1) * tn)
            rows[rs, sl] = rows[rs, sl] + g2_ref[pl.ds(g2_row, 1), sl] * _dot(hdn_s[rs, :], w2_ref[:, sl])

    def for_blocks(fn):
        slot0 = 0
        for _, _, cap, g_row in sets:
            rc = min(cap, MOE_ROWS)
            if cap // rc > 1:
                def body(r, _, slot0=slot0, rc=rc, g_row=g_row):
                    fn(pl.ds(pl.multiple_of(slot0 + r * rc, rc), rc), rc, g_row)
                    return 0

                lax.fori_loop(0, cap // rc, body, 0)
            else:
                fn(slice(slot0, slot0 + rc), rc, g_row)
            slot0 += cap

    for_blocks(lambda rs, size, g_row: hidden(rs, size))
    row_gather(0, 0, total).wait()
    for_blocks(lambda rs, size, g_row: down(rs, g_row))
    start_all(row_scatter)

    @pl.when(step == last_step)
    def _():
        row_scatter(0, 0, total).wait()


def _moe(token_sets, xma, xs, modl, w1, w3, w2, layer):
    n_batch, nt, d = xs.shape
    sets = tuple((row0, tabs[0].shape[2], cap, g_row) for tabs, row0, cap, g_row in token_sets)
    total = sum(cap for _, _, cap, _ in sets)
    for _, _, cap, _ in sets:
        assert cap % min(cap, MOE_ROWS) == 0 and cap % 8 == 0
    rc = min(max(cap for _, _, cap, _ in sets), MOE_ROWS)
    kern = functools.partial(_moe_kernel, sets=sets, d=d)
    any_ = pl.BlockSpec(memory_space=pl.ANY)
    tables = [t for tabs, _, _, _ in token_sets for t in tabs]
    smem_tables = [pltpu.SMEM((nc, LANE), I32) for _, nc, _, _ in sets for _ in range(3)]
    return pl.pallas_call(
        kern,
        grid=(n_batch, N_EXPERTS),
        in_specs=[any_] * len(tables) + [
            any_, any_,
            pl.BlockSpec((MOD_ROWS, d), lambda b, e: (0, 5)),
            pl.BlockSpec((None, None, d, EXPERT_FF), lambda b, e: (layer, e, 0, 0)),
            pl.BlockSpec((None, None, d, EXPERT_FF), lambda b, e: (layer, e, 0, 0)),
            pl.BlockSpec((None, None, EXPERT_FF, d), lambda b, e: (layer, e, 0, 0))],
        out_specs=any_,
        out_shape=jax.ShapeDtypeStruct(xs.shape, F32),
        scratch_shapes=smem_tables + [
            pltpu.SMEM((total,), I32), pltpu.VMEM((total, d // 2 + LANE), I32), pltpu.VMEM((total, d), F32),
            pltpu.VMEM((rc, d), BF16), pltpu.VMEM((total, EXPERT_FF), BF16),
            pltpu.SemaphoreType.DMA((3 + 3 * len(sets),))],
        input_output_aliases={len(tables) + 1: 0},
        compiler_params=_params(("arbitrary", "arbitrary"), 56),
        name="moe_ffn",
    )(*tables, xma, xs, modl, w1, w3, w2)


def _rope_tables(n, lc):
    t = jnp.arange(n)
    pos = jnp.stack([t // GRID_W, t % GRID_W], axis=-1).astype(F32)

    def tables(dh):
        quarter = dh // 4
        inv_freq = ROPE_THETA ** (-jnp.arange(quarter, dtype=F32) / quarter)
        ang = pos[:, :, None] * inv_freq
        cos, sin = jnp.cos(ang), jnp.sin(ang)
        c = jnp.concatenate([cos[:, 0], cos[:, 0], cos[:, 1], cos[:, 1]], axis=-1)
        s = jnp.concatenate([-sin[:, 0], sin[:, 0], -sin[:, 1], sin[:, 1]], axis=-1)
        reps = LANE // dh
        c, s = jnp.tile(c, (1, reps)), jnp.tile(s, (1, reps))
        return (jnp.concatenate([jnp.ones((lc, LANE), F32), c], axis=0),
                jnp.concatenate([jnp.zeros((lc, LANE), F32), s], axis=0))

    return tables(HEAD_DIM) + tables(DIFF_QK_DIM)


def kernel(x, c, ctx, c_ctx, w_ada, b_ada, norm_g, w_in, w_out, qk_g_na, na_rpb, qk_g_diff, diff_lambda,
           diff_subln_g, qk_g_swa, swa_sink, lru_conv_w, lru_conv_b, lru_gate_w, lru_gate_b, lru_lam,
           w_router, w1, w3, w2):
    n_batch, n, d = x.shape
    lc = ctx.shape[1]
    depth = w_ada.shape[0]
    assert n_batch < MOD_ROWS and w_in.shape[2] == D_IN

    cvec = jnp.zeros((MOD_ROWS, d), F32).at[:n_batch].set(c).at[n_batch].set(c_ctx)
    mod = _ada(cvec, w_ada, b_ada)
    cs, ss, cd, sd = _rope_tables(n, lc)
    xs = jnp.concatenate([ctx, x], axis=1)

    cap_lat = CAPACITY_FACTOR * n // N_EXPERTS
    cap_ctx = CAPACITY_FACTOR * lc // N_EXPERTS
    w_in_b, w_out_b = w_in.astype(BF16), w_out.astype(BF16)
    w1b, w3b, w2b = w1.astype(BF16), w3.astype(BF16), w2.astype(BF16)

    for l in range(depth):
        modl = mod[l]
        u_att, u_lru = _inproj(xs, norm_g[l, 0].reshape(1, d), modl, w_in_b, l, lc)

        pvec = jnp.concatenate([qk_g_na[l], jnp.tile(qk_g_diff[l], (1, 2)), qk_g_swa[l],
                                diff_subln_g[l].reshape(1, LANE), jnp.zeros((1, LANE), F32)], axis=0)
        qa, ka, va, qd, kd, vd, qs, ks, vs = _prep(u_att, pvec, cs, ss, cd, sd)

        gmax = lambda g: jnp.max(jnp.abs(g[0])) * jnp.max(jnp.abs(g[1]))
        b_na = (BOUND_SLACK * HEAD_DIM * ATTN_Q_SCALE * gmax(qk_g_na[l])
                + LOG2E * jnp.max(jnp.abs(na_rpb[l])))
        b_swa = jnp.maximum(BOUND_SLACK * HEAD_DIM * ATTN_Q_SCALE * gmax(qk_g_swa[l]),
                            LOG2E * jnp.max(jnp.abs(swa_sink[l])))
        b_diff = BOUND_SLACK * DIFF_QK_DIM * DIFF_Q_SCALE * gmax(qk_g_diff[l])

        o_a = _na(b_na, qa, ka, va, _na_bias_tables(na_rpb[l], n), lc)

        lambda_init = 0.8 - 0.6 * math.exp(-0.3 * l)
        dl = diff_lambda[l].astype(F32)
        lam = jnp.exp(jnp.sum(dl[0] * dl[1])) - jnp.exp(jnp.sum(dl[2] * dl[3])) + lambda_init
        o_d = _diff(jnp.stack([lam, b_diff]).astype(F32), qd, kd, vd, pvec, lc, 1.0 - lambda_init)

        o_s = _swa(b_swa, qs, ks, vs, swa_sink[l], lc)

        af, bf, ab, bb = _lru_gates(u_lru, lru_conv_w[l], lru_conv_b[l], lru_gate_w[l], lru_gate_b[l], lru_lam[l], lc)
        hf, hb = _lru_scan(af, bf, ab, bb, lc)

        xs = _outproj(o_a, o_d, o_s, hf, hb, u_lru, modl, w_out_b, l, xs, lc)

        wr_pad = jnp.zeros((d, LANE), F32).at[:, :N_EXPERTS].set(w_router[l])
        xma, affp = _router(xs, norm_g[l, 1].reshape(1, d), modl, wr_pad, lc)
        token_sets = [(_select(affp, lc, n, cap_lat), lc, cap_lat, None)]
        if l < depth - 1:
            token_sets.append((_select(affp, 0, lc, cap_ctx), 0, cap_ctx, n_batch))
        xs = _moe(tuple(token_sets), xma, xs, modl, w1b, w3b, w2b, l)
    return xs[:, lc:, :]
```

```python
import functools
import math

import numpy as np
import jax
import jax.numpy as jnp
from jax import lax
from jax.experimental import pallas as pl
from jax.experimental.pallas import tpu as pltpu

GRID_W = 64
HEAD_DIM = 128
NA_HEADS = 8
NA_KH = 8
NA_KW = 16
DIFF_HEADS = 8
DIFF_QK_DIM = 64
SWA_Q_HEADS = 8
SWA_KV_HEADS = 2
SWA_GROUP = SWA_Q_HEADS // SWA_KV_HEADS
SWA_WINDOW = 128
LRU_WIDTH = 1024
LRU_BLOCKS = 8
LRU_C = 8.0
N_EXPERTS = 16
EXPERT_FF = 256
CAPACITY_FACTOR = 2
ROPE_THETA = 10000.0
NEG_INF = -1e30
EPS = 1e-6

OFF_QA, OFF_KA, OFF_VA = 0, 1024, 2048
OFF_QD, OFF_KD, OFF_VD = 3072, 4096, 5120
OFF_QS, OFF_KS, OFF_VS = 6144, 7168, 7424
OFF_XR, OFF_GR = 7680, 8704
D_ATT = 7680
D_IN = 9728
D_MIX = 4096

LOG2E = 1.4426950408889634
DIFF_Q_SCALE = DIFF_QK_DIM ** -0.5 * LOG2E
ATTN_FAST_BOUND = 60.0
ATTN_Q_SCALE = HEAD_DIM ** -0.5 * LOG2E
BOUND_SLACK = 1.02

LANE = 128
MOD_ROWS = 8
V7X_VMEM_BYTES = 64 * 1024 * 1024

F32 = jnp.float32
BF16 = jnp.bfloat16
I32 = jnp.int32


def _tile(n, target, mult=LANE):
    best = None
    for t in range(mult, min(n, target) + 1, mult):
        if n % t == 0:
            best = t
    assert best is not None, (n, target, mult)
    return best


def _params(sem, vmem_mb):
    return pltpu.CompilerParams(dimension_semantics=sem, vmem_limit_bytes=min(vmem_mb, 60) * 1024 * 1024)


def _dot_nt(a, b):
    return lax.dot_general(a, b, (((1,), (1,)), ((), ())), preferred_element_type=F32)


def _dot(a, b):
    return jnp.dot(a, b, preferred_element_type=F32)


def _ada_kernel(c_ref, w_ref, b_ref, o_ref):
    cv = c_ref[...]
    s = cv * jax.nn.sigmoid(cv)
    s_hi = s.astype(BF16)
    s_lo = (s - s_hi.astype(F32)).astype(BF16)
    w = w_ref[...].astype(BF16)
    o_ref[...] = _dot(s_hi, w) + _dot(s_lo, w) + b_ref[...]


def _ada(cvec, w_ada, b_ada):
    depth, d, n6 = w_ada.shape
    tn = _tile(n6, 1024)
    return pl.pallas_call(
        _ada_kernel,
        grid=(depth, n6 // tn),
        in_specs=[pl.BlockSpec((MOD_ROWS, d), lambda l, j: (0, 0)),
                  pl.BlockSpec((None, d, tn), lambda l, j: (l, 0, j)),
                  pl.BlockSpec((None, 1, tn), lambda l, j: (l, 0, j))],
        out_specs=pl.BlockSpec((None, MOD_ROWS, tn), lambda l, j: (l, 0, j)),
        out_shape=jax.ShapeDtypeStruct((depth, MOD_ROWS, n6), F32),
        compiler_params=_params(("arbitrary", "arbitrary"), 56),
        name="ada_mod",
    )(cvec, w_ada, b_ada.reshape(depth, 1, n6))


ROW_CHUNK = 64


def _mod_row(ref, row0, b, n_batch, lc):
    return jnp.where(row0 < lc, ref[pl.ds(n_batch, 1), :], ref[pl.ds(b, 1), :])


def _norm_modulate(x, g, sh, sc):
    ms = jnp.mean(x * x, axis=-1, keepdims=True)
    return x * lax.rsqrt(ms + EPS) * g * (1.0 + sc) + sh


def _for_row_chunks(tm, fn):
    assert tm % ROW_CHUNK == 0

    def body(c, _):
        fn(pl.multiple_of(c * ROW_CHUNK, ROW_CHUNK))
        return 0

    lax.fori_loop(0, tm // ROW_CHUNK, body, 0)


def _inproj_kernel(x_ref, g_ref, sh_ref, sc_ref, w_ref, att_ref, lru_ref, xn_ref, *, n_batch, lc, tm, n_att):
    b = pl.program_id(0)
    i = pl.program_id(1)
    j = pl.program_id(2)

    @pl.when(j == 0)
    def _():
        def chunk(r0):
            sh = _mod_row(sh_ref, i * tm + r0, b, n_batch, lc)
            sc = _mod_row(sc_ref, i * tm + r0, b, n_batch, lc)
            xn = _norm_modulate(x_ref[pl.ds(r0, ROW_CHUNK), :], g_ref[...], sh, sc)
            xn_ref[pl.ds(r0, ROW_CHUNK), :] = xn.astype(BF16)

        _for_row_chunks(tm, chunk)

    @pl.when(j < n_att)
    def _():
        att_ref[...] = _dot(xn_ref[...], w_ref[...]).astype(BF16)

    @pl.when(j >= n_att)
    def _():
        lru_ref[...] = _dot(xn_ref[...], w_ref[...])


def _inproj(xs, g, modl, w_bf, layer, lc):
    n_batch, nt, d = xs.shape
    d_in = w_bf.shape[2]
    tm = _tile(nt, 768)
    tn = _tile(math.gcd(D_ATT, d_in - D_ATT), 512)
    n_att = D_ATT // tn
    assert lc % ROW_CHUNK == 0
    kern = functools.partial(_inproj_kernel, n_batch=n_batch, lc=lc, tm=tm, n_att=n_att)
    return pl.pallas_call(
        kern,
        grid=(n_batch, nt // tm, d_in // tn),
        in_specs=[pl.BlockSpec((None, tm, d), lambda b, i, j: (b, i, 0)),
                  pl.BlockSpec((1, d), lambda b, i, j: (0, 0)),
                  pl.BlockSpec((MOD_ROWS, d), lambda b, i, j: (0, 0)),
                  pl.BlockSpec((MOD_ROWS, d), lambda b, i, j: (0, 1)),
                  pl.BlockSpec((None, d, tn), lambda b, i, j: (layer, 0, j))],
        out_specs=[pl.BlockSpec((None, tm, tn), lambda b, i, j: (b, i, jnp.minimum(j, n_att - 1))),
                   pl.BlockSpec((None, tm, tn), lambda b, i, j: (b, i, jnp.maximum(j - n_att, 0)))],
        out_shape=[jax.ShapeDtypeStruct((n_batch, nt, D_ATT), BF16),
                   jax.ShapeDtypeStruct((n_batch, nt, d_in - D_ATT), F32)],
        scratch_shapes=[pltpu.VMEM((tm, d), BF16)],
        compiler_params=_params(("arbitrary", "arbitrary", "arbitrary"), 56),
        name="in_proj",
    )(xs, g, modl, modl, w_bf)


def _rms_full(x, g):
    ms = jnp.mean(x * x, axis=-1, keepdims=True)
    return x * lax.rsqrt(ms + EPS) * g


def _group_ones(width):
    r = lax.broadcasted_iota(I32, (LANE, LANE), 0) // width
    c = lax.broadcasted_iota(I32, (LANE, LANE), 1) // width
    return jnp.where(r == c, 1.0, 0.0).astype(BF16)


def _rms_groups(x, g, ones, width):
    x2 = x * x
    hi = x2.astype(BF16)
    lo = (x2 - hi.astype(F32)).astype(BF16)
    ms = (_dot(hi, ones) + _dot(lo, ones)) * (1.0 / width)
    return x * lax.rsqrt(ms + EPS) * g


def _rope(x, cos, sin_signed, half):
    lane = lax.broadcasted_iota(I32, (1, LANE), 1)
    up = pltpu.roll(x, LANE - half, axis=1)
    dn = pltpu.roll(x, half, axis=1)
    partner = jnp.where((lane & half) == 0, up, dn)
    return x * cos + partner * sin_signed


def _prep_kernel(u_ref, p_ref, cs_ref, ss_ref, cd_ref, sd_ref,
                 qa_ref, ka_ref, va_ref, qd_ref, kd_ref, vd_ref, qs_ref, ks_ref, vs_ref):
    g_na_q, g_na_k = p_ref[0:1, :], p_ref[1:2, :]
    g_d_q, g_d_k = p_ref[2:3, :], p_ref[3:4, :]
    g_s_q, g_s_k = p_ref[4:5, :], p_ref[5:6, :]
    cs, ss, cd, sd = cs_ref[...], ss_ref[...], cd_ref[...], sd_ref[...]
    lo = lax.broadcasted_iota(I32, (1, LANE), 1) < DIFF_QK_DIM

    def col(off, h):
        return u_ref[:, off + h * LANE: off + (h + 1) * LANE].astype(F32)

    def put(ref, h, val):
        ref[:, h * LANE:(h + 1) * LANE] = val.astype(BF16)

    ones_head = _group_ones(HEAD_DIM)
    ones_half = _group_ones(DIFF_QK_DIM)
    rms_head = lambda x, g: _rms_groups(x, g, ones_head, HEAD_DIM)
    rms_half = lambda x, g: _rms_groups(x, g, ones_half, DIFF_QK_DIM)

    for h in range(NA_HEADS):
        put(qa_ref, h, rms_head(col(OFF_QA, h), g_na_q) * ATTN_Q_SCALE)
        put(ka_ref, h, rms_head(col(OFF_KA, h), g_na_k))
        put(va_ref, h, col(OFF_VA, h))
    ones_col = jnp.where(lax.broadcasted_iota(I32, (u_ref.shape[0], LANE), 1) == 0, 1.0, 0.0).astype(BF16)
    for h in range(DIFF_HEADS):
        q = _rope(rms_half(col(OFF_QD, h), g_d_q), cd, sd, DIFF_QK_DIM // 4) * DIFF_Q_SCALE
        qd_ref[0, :, h * LANE:(h + 1) * LANE] = jnp.where(lo, q, 0.0).astype(BF16)
        qd_ref[1, :, h * LANE:(h + 1) * LANE] = jnp.where(lo, 0.0, q).astype(BF16)
        put(kd_ref, h, _rope(rms_half(col(OFF_KD, h), g_d_k), cd, sd, DIFF_QK_DIM // 4))
        put(vd_ref, 2 * h, col(OFF_VD, h))
        vd_ref[:, (2 * h + 1) * LANE:(2 * h + 2) * LANE] = ones_col
    for h in range(SWA_Q_HEADS):
        put(qs_ref, h, _rope(rms_head(col(OFF_QS, h), g_s_q), cs, ss, HEAD_DIM // 4) * ATTN_Q_SCALE)
    for h in range(SWA_KV_HEADS):
        put(ks_ref, h, _rope(rms_head(col(OFF_KS, h), g_s_k), cs, ss, HEAD_DIM // 4))
        put(vs_ref, h, col(OFF_VS, h))


def _prep(u, pvec, cs, ss, cd, sd):
    n_batch, nt, _ = u.shape
    tr = _tile(nt, 256)
    row = lambda w: pl.BlockSpec((None, tr, w), lambda b, i: (b, i, 0))
    tab = pl.BlockSpec((tr, LANE), lambda b, i: (i, 0))
    sd_ = lambda w: jax.ShapeDtypeStruct((n_batch, nt, w), BF16)
    return pl.pallas_call(
        _prep_kernel,
        grid=(n_batch, nt // tr),
        in_specs=[row(D_ATT), pl.BlockSpec((8, LANE), lambda b, i: (0, 0)), tab, tab, tab, tab],
        out_specs=[row(1024), row(1024), row(1024),
                   pl.BlockSpec((None, 2, tr, 1024), lambda b, i: (b, 0, i, 0)),
                   row(1024), row(2048), row(1024), row(256), row(256)],
        out_shape=[sd_(1024), sd_(1024), sd_(1024),
                   jax.ShapeDtypeStruct((n_batch, 2, nt, 1024), BF16),
                   sd_(1024), sd_(2048), sd_(1024), sd_(256), sd_(256)],
        compiler_params=_params(("arbitrary", "arbitrary"), 48),
        name="qkv_prep",
    )(u, pvec, cs, ss, cd, sd)


NA_QBLK = 2 * GRID_W
NA_KROWS = 10
NA_KBLK = NA_KROWS * GRID_W


def _skewed(n_chains, scores, numerators, output):
    s_prev = p_prev = None
    for c in range(n_chains + 2):
        s_new = scores(c) if c < n_chains else None
        p_new = numerators(c - 1, s_prev) if s_prev is not None else None
        if p_prev is not None:
            output(c - 2, p_prev)
        s_prev, p_prev = s_new, p_new


def _na_kernel(sc_ref, q_ref, k_ref, v_ref, *rest, lc, nlb, group, online):
    bias_refs, o_ref = rest[:group], rest[group]
    i0 = pl.program_id(2) * group
    kc = k_ref[0:lc, :]
    vc = v_ref[0:lc, :]

    def window_start(g):
        li = i0 + g - lc // NA_QBLK
        kb = jnp.clip(li - 2, 0, nlb - NA_KROWS // 2)
        return pl.multiple_of(lc + kb * NA_QBLK, NA_QBLK)

    def scores(g):
        q = q_ref[g * NA_QBLK:(g + 1) * NA_QBLK, :]
        return _dot_nt(q, k_ref[pl.ds(window_start(g), NA_KBLK), :]) + bias_refs[g][...], _dot_nt(q, kc)

    def numerators(g, s):
        s_n, s_c = s
        if online:
            m = jnp.maximum(jnp.max(s_n, axis=-1, keepdims=True), jnp.max(s_c, axis=-1, keepdims=True))
        else:
            m = sc_ref[0]
        p_n = jnp.exp2(s_n - m)
        p_c = jnp.exp2(s_c - m)
        l = jnp.sum(p_n, axis=-1, keepdims=True) + jnp.sum(p_c, axis=-1, keepdims=True)
        return p_n.astype(BF16), p_c.astype(BF16), l

    def output(g, p):
        o = _dot(p[0], v_ref[pl.ds(window_start(g), NA_KBLK), :]) + _dot(p[1], vc)
        o_ref[g * NA_QBLK:(g + 1) * NA_QBLK, :] = (o / p[2]).astype(BF16)

    _skewed(group, scores, numerators, output)


def _na_bias_tables(rpb, n):
    rows = n // GRID_W
    nlb = rows // 2
    reps = [0, 1, 2, nlb - 2, nlb - 1]
    h = rpb.shape[0]
    cols = np.arange(GRID_W)
    cstart = np.clip(cols - NA_KW // 2, 0, GRID_W - NA_KW)
    col_ok = (cols[None, :] >= cstart[:, None]) & (cols[None, :] < cstart[:, None] + NA_KW)
    dx = cols[None, :] - cols[:, None] + (NA_KW - 1)
    sel = (dx[None] == np.arange(2 * NA_KW - 1)[:, None, None]) & col_ok[None]
    tx = jnp.einsum('hyx,xqk->hyqk', rpb.astype(F32), jnp.asarray(sel, F32), precision=lax.Precision.HIGHEST)
    dy_i = np.zeros((5, 2, NA_KROWS), np.int32)
    row_ok = np.zeros((5, 2, NA_KROWS), bool)
    for c, li in enumerate(reps):
        kb = int(np.clip(li - 2, 0, nlb - NA_KROWS // 2))
        qr = 2 * li + np.arange(2)
        kr = 2 * kb + np.arange(NA_KROWS)
        rs = np.clip(qr - NA_KH // 2, 0, rows - NA_KH)
        row_ok[c] = (kr[None, :] >= rs[:, None]) & (kr[None, :] < rs[:, None] + NA_KH)
        dy_i[c] = np.clip(kr[None, :] - qr[:, None] + (NA_KH - 1), 0, 2 * NA_KH - 2)
    blocks = jnp.take(tx, jnp.asarray(dy_i.reshape(-1)), axis=1)
    blocks = blocks.reshape(h, 5, 2, NA_KROWS, GRID_W, GRID_W)
    ok = row_ok[:, :, :, None, None] & col_ok[None, None, None]
    tbl = jnp.where(ok[None], blocks * LOG2E, NEG_INF)
    tbl = tbl.transpose(1, 0, 2, 4, 3, 5).reshape(5, h, NA_QBLK, NA_KBLK)
    dead = jnp.full((1,) + tbl.shape[1:], NEG_INF, F32)
    return jnp.concatenate([tbl, dead], axis=0)


def _group(nblocks, target):
    return max(g for g in range(1, target + 1) if nblocks % g == 0)


def _softmax_dispatch(bound, call):
    sc = jnp.reshape(bound, (1,)).astype(F32)
    return lax.cond(bound <= ATTN_FAST_BOUND, lambda s: call(False)(s), lambda s: call(True)(s), sc)


def _na(bound, qa, ka, va, bias, lc):
    n_batch, nt, _ = qa.shape
    n = nt - lc
    nlb = n // NA_QBLK
    assert nlb >= 5 and lc % NA_QBLK == 0
    ncb = lc // NA_QBLK
    group = _group(nt // NA_QBLK, 11)

    def case(i):
        li = i - ncb
        return jnp.where(li < 0, 5, jnp.where(li < 2, li, jnp.where(li >= nlb - 2, li - (nlb - 5), 2)))

    kv = pl.BlockSpec((None, nt, LANE), lambda b, h, i: (b, 0, h))
    qo = pl.BlockSpec((None, group * NA_QBLK, LANE), lambda b, h, i: (b, i, h))
    bias_specs = [pl.BlockSpec((None, None, NA_QBLK, NA_KBLK),
                               functools.partial(lambda b, h, i, g: (case(i * group + g), h, 0, 0), g=g))
                  for g in range(group)]

    def call(online):
        kern = functools.partial(_na_kernel, lc=lc, nlb=nlb, group=group, online=online)
        return lambda sc: pl.pallas_call(
            kern,
            grid=(n_batch, NA_HEADS, nt // (group * NA_QBLK)),
            in_specs=[pl.BlockSpec(memory_space=pltpu.SMEM), qo, kv, kv] + bias_specs,
            out_specs=qo,
            out_shape=jax.ShapeDtypeStruct((n_batch, nt, NA_HEADS * HEAD_DIM), BF16),
            compiler_params=_params(("arbitrary",) * 3, 40),
            name="na_attn_online" if online else "na_attn_offset",
        )(sc, qa, ka, va, *([bias] * group))

    return _softmax_dispatch(bound, call)


SWA_BLK = 128
SWA_WIN = 3 * SWA_BLK


def _swa_kernel(sc_ref, q_ref, k_ref, v_ref, sink_ref, *rest, lc, n, group, online):
    mask_refs, o_ref = rest[:group], rest[group]
    i0 = pl.program_id(2) * group
    kc = k_ref[0:lc, :]
    vc = v_ref[0:lc, :]

    def window_start(c):
        li = i0 + c // SWA_GROUP - lc // SWA_BLK
        ws = jnp.clip((li - 1) * SWA_BLK, 0, n - SWA_WIN)
        return pl.multiple_of(lc + ws, SWA_BLK)

    def block(c):
        gq, g = divmod(c, SWA_GROUP)
        return slice(gq * SWA_BLK, (gq + 1) * SWA_BLK), slice(g * LANE, (g + 1) * LANE)

    def scores(c):
        rows, cols = block(c)
        q = q_ref[rows, cols]
        return _dot_nt(q, k_ref[pl.ds(window_start(c), SWA_WIN), :]), _dot_nt(q, kc)

    def numerators(c, s):
        s_w, s_c = s
        g = c % SWA_GROUP
        mask = mask_refs[c // SWA_GROUP][...]
        sink = sink_ref[g * SWA_BLK:(g + 1) * SWA_BLK, 0:1]
        if online:
            s_w = jnp.where(mask > 0.0, s_w, NEG_INF)
            m = jnp.maximum(jnp.maximum(jnp.max(s_w, axis=-1, keepdims=True),
                                        jnp.max(s_c, axis=-1, keepdims=True)), sink)
            p_w = jnp.exp2(s_w - m)
        else:
            m = sc_ref[0]
            p_w = jnp.exp2(s_w - m) * mask
        p_c = jnp.exp2(s_c - m)
        l = jnp.sum(p_w, axis=-1, keepdims=True) + jnp.sum(p_c, axis=-1, keepdims=True) + jnp.exp2(sink - m)
        return p_w.astype(BF16), p_c.astype(BF16), l

    def output(c, p):
        rows, cols = block(c)
        o = (_dot(p[0], v_ref[pl.ds(window_start(c), SWA_WIN), :]) + _dot(p[1], vc)) / p[2]
        o_ref[rows, cols] = o.astype(BF16)

    _skewed(group * SWA_GROUP, scores, numerators, output)


def _swa_masks():
    r = np.arange(SWA_BLK)[:, None]
    j = np.arange(SWA_WIN)[None, :]
    rel = [j - r, j - r - SWA_BLK, j - r - 2 * SWA_BLK]
    tabs = [(np.abs(d) <= SWA_WINDOW) for d in rel] + [np.zeros((SWA_BLK, SWA_WIN), bool)]
    return jnp.asarray(np.stack(tabs), F32)


def _swa(bound, qs, ks, vs, sink, lc):
    n_batch, nt, _ = qs.shape
    n = nt - lc
    nb = n // SWA_BLK
    assert nb >= 3 and lc % SWA_BLK == 0
    ncb = lc // SWA_BLK
    rows = SWA_GROUP * SWA_BLK
    group = _group(nt // SWA_BLK, 6)
    sink_rows = jnp.broadcast_to(
        jnp.repeat((sink.astype(F32) * LOG2E).reshape(SWA_KV_HEADS, SWA_GROUP), SWA_BLK, axis=1)[:, :, None],
        (SWA_KV_HEADS, rows, LANE))
    masks = _swa_masks()

    def case(i):
        li = i - ncb
        return jnp.where(li < 0, 3, jnp.where(li == 0, 0, jnp.where(li == nb - 1, 2, 1)))

    kv = pl.BlockSpec((None, nt, LANE), lambda b, k, i: (b, 0, k))
    qo = pl.BlockSpec((None, group * SWA_BLK, SWA_GROUP * LANE), lambda b, k, i: (b, i, k))
    mask_specs = [pl.BlockSpec((None, SWA_BLK, SWA_WIN),
                               functools.partial(lambda b, k, i, g: (case(i * group + g), 0, 0), g=g))
                  for g in range(group)]

    def call(online):
        kern = functools.partial(_swa_kernel, lc=lc, n=n, group=group, online=online)
        return lambda sc: pl.pallas_call(
            kern,
            grid=(n_batch, SWA_KV_HEADS, nt // (group * SWA_BLK)),
            in_specs=[pl.BlockSpec(memory_space=pltpu.SMEM), qo, kv, kv,
                      pl.BlockSpec((None, rows, LANE), lambda b, k, i: (k, 0, 0))] + mask_specs,
            out_specs=qo,
            out_shape=jax.ShapeDtypeStruct((n_batch, nt, SWA_Q_HEADS * HEAD_DIM), BF16),
            compiler_params=_params(("arbitrary",) * 3, 32),
            name="swa_attn_online" if online else "swa_attn_offset",
        )(sc, qs, ks, vs, sink_rows, *([masks] * group))

    return _softmax_dispatch(bound, call)


def _diff_finish(sc_ref, g_ref, o_ref, num, den, tq, post_scale):
    o2 = num / den
    o = o2[0:tq, :] - sc_ref[0] * o2[tq:2 * tq, :]
    o_ref[...] = (_rms_full(o, g_ref[6:7, :]) * post_scale).astype(BF16)


def _diff_online_kernel(sc_ref, q_ref, k_ref, v_ref, g_ref, o_ref, *, lc, nt, tq, tk, post_scale):
    qi = pl.program_id(2)
    q = q_ref[...].reshape(2 * tq, LANE)
    nk = jnp.where(qi < lc // tq, lc // tk, nt // tk)

    def body(kt, carry):
        m, l, acc = carry
        k0 = pl.multiple_of(kt * tk, tk)
        k = k_ref[pl.ds(k0, tk), :]
        v = v_ref[pl.ds(k0, tk), :]
        s = _dot_nt(q, k)
        m_new = jnp.maximum(m, jnp.max(s, axis=-1, keepdims=True))
        alpha = jnp.exp2(m - m_new)
        p = jnp.exp2(s - m_new)
        l = alpha * l + jnp.sum(p, axis=-1, keepdims=True)
        acc = alpha * acc + _dot(p.astype(BF16), v)
        return m_new, l, acc

    init = (jnp.full((2 * tq, 1), NEG_INF, F32), jnp.zeros((2 * tq, 1), F32), jnp.zeros((2 * tq, LANE), F32))
    _, l, acc = lax.fori_loop(0, nk, body, init)
    _diff_finish(sc_ref, g_ref, o_ref, acc, l, tq, post_scale)


def _diff_offset_kernel(sc_ref, q_ref, qn_ref, k_ref, v_ref, g_ref, o_ref, acc_ref, p_ref,
                        *, lc, nt, tq, tk, post_scale):
    qi = pl.program_id(2)
    rows2 = 2 * tq
    nk = nt // tk
    ncq = lc // tq
    q = q_ref[...].reshape(rows2, LANE)
    off = sc_ref[1]

    def probs(qq, k0, tkk):
        return jnp.exp2(_dot_nt(qq, k_ref[k0:k0 + tkk, :]) - off).astype(BF16)

    def next_tile_probs():
        return probs(qn_ref[...].reshape(rows2, LANE), 0, tk)

    @pl.when(qi < ncq)
    def _():
        acc_ref[...] = _dot(probs(q, 0, lc), v_ref[0:lc, :])

        @pl.when(qi == ncq - 1)
        def _():
            p_ref[0] = next_tile_probs()

    def latent(first_slot):
        tot = None
        for t in range(nk):
            pv = _dot(p_ref[(t + first_slot) % 2], v_ref[t * tk:(t + 1) * tk, :])
            p_ref[(t + 1 + first_slot) % 2] = probs(q, (t + 1) * tk, tk) if t + 1 < nk else next_tile_probs()
            tot = pv if tot is None else tot + pv
        acc_ref[...] = tot

    first_slot = ((qi - ncq) * nk) % 2
    for slot in range(2 if nk % 2 else 1):
        @pl.when((qi >= ncq) & (first_slot == slot))
        def _():
            latent(slot)

    acc = acc_ref[...]
    _diff_finish(sc_ref, g_ref, o_ref, acc[:, 0:LANE], acc[:, LANE:LANE + 1], tq, post_scale)


def _diff(sc, qd, kd, vde, pvec, lc, post_scale):
    n_batch, _, nt, _ = qd.shape
    tq = _tile(math.gcd(lc, nt), 256)
    assert lc % tq == 0
    grid = (n_batch, DIFF_HEADS, nt // tq)
    smem = pl.BlockSpec(memory_space=pltpu.SMEM)
    qspec = pl.BlockSpec((None, 2, tq, LANE), lambda b, h, i: (b, 0, i, h))
    kspec = pl.BlockSpec((None, nt, LANE), lambda b, h, i: (b, 0, h))
    gspec = pl.BlockSpec((8, LANE), lambda b, h, i: (0, 0))
    ospec = pl.BlockSpec((None, tq, LANE), lambda b, h, i: (b, i, h))
    oshape = jax.ShapeDtypeStruct((n_batch, nt, DIFF_HEADS * HEAD_DIM), BF16)

    def offset(sc, qd, kd, vde, pvec):
        tk = _tile(nt, 768)
        kern = functools.partial(_diff_offset_kernel, lc=lc, nt=nt, tq=tq, tk=tk, post_scale=post_scale)
        nq = nt // tq
        qnext = pl.BlockSpec((None, 2, tq, LANE), lambda b, h, i: (b, 0, jnp.minimum(i + 1, nq - 1), h))
        return pl.pallas_call(
            kern, grid=grid,
            in_specs=[smem, qspec, qnext, kspec, pl.BlockSpec((None, nt, 2 * LANE), lambda b, h, i: (b, 0, h)),
                      gspec],
            out_specs=ospec, out_shape=oshape,
            scratch_shapes=[pltpu.VMEM((2 * tq, 2 * LANE), F32), pltpu.VMEM((2, 2 * tq, tk), BF16)],
            compiler_params=_params(("arbitrary",) * 3, 40),
            name="diff_attn_offset",
        )(sc, qd, qd, kd, vde, pvec)

    def online(sc, qd, kd, vde, pvec):
        tk = _tile(math.gcd(lc, nt), 256)
        kern = functools.partial(_diff_online_kernel, lc=lc, nt=nt, tq=tq, tk=tk, post_scale=post_scale)
        return pl.pallas_call(
            kern, grid=grid,
            in_specs=[smem, qspec, kspec, pl.BlockSpec((None, nt, LANE), lambda b, h, i: (b, 0, 2 * h)), gspec],
            out_specs=ospec, out_shape=oshape,
            compiler_params=_params(("arbitrary",) * 3, 32),
            name="diff_attn_online",
        )(sc, qd, kd, vde, pvec)

    return lax.cond(sc[1] <= ATTN_FAST_BOUND, offset, online, sc, qd, kd, vde, pvec)


LRU_HALO = 8


LRU_STEP_BLOCKS = 4


def _lru_gate_kernel(xp_ref, x_ref, xn_ref, cw_ref, cb_ref, gw_ref, gb_ref, lam_ref,
                     af_ref, bf_ref, ab_ref, bb_ref, *, t, seg_starts, seg_ends):
    k = pl.program_id(2)
    is_start = functools.reduce(jnp.logical_or, [k == s for s in seg_starts])
    is_end = functools.reduce(jnp.logical_or, [k == s for s in seg_ends])
    h = LRU_HALO
    outs = ((af_ref, bf_ref), (ab_ref, bb_ref))
    sigmoid = lambda z: 0.5 + 0.5 * jnp.tanh(0.5 * z)
    for c in range(LRU_STEP_BLOCKS):
        cs = slice(c * LANE, (c + 1) * LANE)
        cur = x_ref[:, cs]
        prev = jnp.where(is_start, 0.0, xp_ref[:, cs])
        nxt = jnp.where(is_end, 0.0, xn_ref[:, cs])
        ext = jnp.concatenate([prev, cur, nxt], axis=0)
        xl = (cb_ref[:, cs] + cw_ref[0:1, cs] * ext[h - 2:h - 2 + t, :] + cw_ref[1:2, cs] * ext[h - 1:h - 1 + t, :]
              + cw_ref[2:3, cs] * cur + cw_ref[3:4, cs] * ext[h + 1:h + 1 + t, :])
        xb = xl.astype(BF16)
        for d in range(2):
            r = sigmoid(_dot(xb, gw_ref[d, 0, c].astype(BF16)) + gb_ref[2 * d:2 * d + 1, cs])
            ig = sigmoid(_dot(xb, gw_ref[d, 1, c].astype(BF16)) + gb_ref[2 * d + 1:2 * d + 2, cs])
            a = jnp.exp(LRU_C * r * jax.nn.log_sigmoid(lam_ref[d:d + 1, cs]))
            outs[d][0][:, cs] = a
            outs[d][1][:, cs] = jnp.sqrt(1.0 - a * a) * (ig * xl)


def _lru_gates(u, conv_w, conv_b, gate_w, gate_b, lam, lc):
    n_batch, nt, _ = u.shape
    t = _tile(math.gcd(lc, nt), 256)
    nk = nt // t
    hb = t // LRU_HALO
    seg_starts = (0, lc // t)
    seg_ends = (lc // t - 1, nk - 1)
    w = LRU_STEP_BLOCKS * LANE
    assert OFF_XR % w == 0 and LRU_WIDTH % w == 0
    cb0 = (OFF_XR - D_ATT) // w
    kern = functools.partial(_lru_gate_kernel, t=t, seg_starts=seg_starts, seg_ends=seg_ends)
    out = pl.BlockSpec((None, t, w), lambda b, c, k: (b, k, c))
    osd = jax.ShapeDtypeStruct((n_batch, nt, LRU_WIDTH), F32)
    return pl.pallas_call(
        kern,
        grid=(n_batch, LRU_WIDTH // w, nk),
        in_specs=[pl.BlockSpec((None, LRU_HALO, w), lambda b, c, k: (b, jnp.maximum(k * hb - 1, 0), cb0 + c)),
                  pl.BlockSpec((None, t, w), lambda b, c, k: (b, k, cb0 + c)),
                  pl.BlockSpec((None, LRU_HALO, w),
                               lambda b, c, k: (b, jnp.minimum((k + 1) * hb, nt // LRU_HALO - 1), cb0 + c)),
                  pl.BlockSpec((4, w), lambda b, c, k: (0, c)),
                  pl.BlockSpec((1, w), lambda b, c, k: (0, c)),
                  pl.BlockSpec((2, 2, LRU_STEP_BLOCKS, LANE, LANE), lambda b, c, k: (0, 0, c, 0, 0)),
                  pl.BlockSpec((4, w), lambda b, c, k: (0, c)),
                  pl.BlockSpec((2, w), lambda b, c, k: (0, c))],
        out_specs=[out, out, out, out],
        out_shape=[osd, osd, osd, osd],
        compiler_params=_params(("arbitrary",) * 3, 24),
        name="lru_gates",
    )(u, u, u, conv_w, conv_b.reshape(1, LRU_WIDTH), gate_w, gate_b.reshape(4, LRU_WIDTH), lam)


def _lru_scan_kernel(af_ref, bf_ref, ab_ref, bb_ref, hf_ref, hb_ref, cf_ref, cbk_ref, *, t):
    @pl.when(pl.program_id(1) == 0)
    def _():
        cf_ref[...] = jnp.zeros_like(cf_ref)
        cbk_ref[...] = jnp.zeros_like(cbk_ref)

    ng = t // 8
    row = lax.broadcasted_iota(I32, (8, 1), 0)

    def scan8(a, bt, h_in, reverse):
        for s in (1, 2, 4):
            shift = (8 - s) if reverse else s
            has = (row < 8 - s) if reverse else (row >= s)
            a_n = jnp.where(has, pltpu.roll(a, shift, axis=0), 1.0)
            b_n = jnp.where(has, pltpu.roll(bt, shift, axis=0), 0.0)
            bt = a * b_n + bt
            a = a * a_n
        h = a * h_in + bt
        return h, (h[0:1, :] if reverse else h[7:8, :])

    def body(g, carry):
        hf, hb = carry
        r0 = pl.multiple_of(g * 8, 8)
        h, hf = scan8(af_ref[pl.ds(r0, 8), :], bf_ref[pl.ds(r0, 8), :], hf, False)
        hf_ref[pl.ds(r0, 8), :] = h
        r1 = pl.multiple_of((ng - 1 - g) * 8, 8)
        h, hb = scan8(ab_ref[pl.ds(r1, 8), :], bb_ref[pl.ds(r1, 8), :], hb, True)
        hb_ref[pl.ds(r1, 8), :] = h
        return hf, hb

    hf, hb = lax.fori_loop(0, ng, body, (cf_ref[...], cbk_ref[...]))
    cf_ref[...] = hf
    cbk_ref[...] = hb


def _lru_scan(af, bf, ab, bb, lc):
    n_batch, nt, w = af.shape
    t = lc
    assert nt % t == 0
    nk = nt // t
    kern = functools.partial(_lru_scan_kernel, t=t)
    fwd = pl.BlockSpec((None, t, w), lambda b, k: (b, k, 0))
    bwd = pl.BlockSpec((None, t, w), lambda b, k: (b, jnp.where(k == 0, 0, nk - k), 0))
    osd = jax.ShapeDtypeStruct((n_batch, nt, w), F32)
    return pl.pallas_call(
        kern,
        grid=(n_batch, nk),
        in_specs=[fwd, fwd, bwd, bwd],
        out_specs=[fwd, bwd],
        out_shape=[osd, osd],
        scratch_shapes=[pltpu.VMEM((1, w), F32), pltpu.VMEM((1, w), F32)],
        compiler_params=_params(("arbitrary", "arbitrary"), 24),
        name="lru_scan",
    )(af, bf, ab, bb)


def _outproj_kernel(oa_ref, od_ref, os_ref, hf_ref, hb_ref, g0_ref, g1_ref, gate_ref, w_ref, x_ref,
                    o_ref, lhs_ref, *, n_batch, lc, tm):
    b = pl.program_id(0)
    i = pl.program_id(1)

    @pl.when(pl.program_id(2) == 0)
    def _():
        def chunk(r0):
            rs = pl.ds(r0, ROW_CHUNK)
            lhs_ref[rs, 0:1024] = oa_ref[rs, :]
            lhs_ref[rs, 1024:2048] = od_ref[rs, :]
            lhs_ref[rs, 2048:3072] = os_ref[rs, :]
            for c, g_ref in enumerate((g0_ref, g1_ref)):
                cs = slice(c * 512, (c + 1) * 512)
                y = (hf_ref[rs, cs] + hb_ref[rs, cs]) * jax.nn.gelu(g_ref[rs, :])
                lhs_ref[rs, 3072 + c * 512:3584 + c * 512] = y.astype(BF16)

        _for_row_chunks(tm, chunk)

    row = i * tm + lax.broadcasted_iota(I32, (tm, 1), 0)
    gate = jnp.where(row < lc, gate_ref[pl.ds(n_batch, 1), :], gate_ref[pl.ds(b, 1), :])
    o_ref[...] = x_ref[...] + gate * _dot(lhs_ref[...], w_ref[...])


def _outproj(o_a, o_d, o_s, hf, hb, u, modl, w_bf, layer, xs, lc):
    n_batch, nt, d = xs.shape
    tm = _tile(nt, 768)
    tn = _tile(d, 512)
    ntile = d // tn
    kern = functools.partial(_outproj_kernel, n_batch=n_batch, lc=lc, tm=tm)
    mix = pl.BlockSpec((None, tm, 1024), lambda b, i, j: (b, i, 0))
    gr = lambda c: pl.BlockSpec((None, tm, 512), lambda b, i, j: (b, i, (OFF_GR - D_ATT) // 512 + c))
    xblk = pl.BlockSpec((None, tm, tn), lambda b, i, j: (b, i, j))
    return pl.pallas_call(
        kern,
        grid=(n_batch, nt // tm, ntile),
        in_specs=[mix, mix, mix, mix, mix, gr(0), gr(1),
                  pl.BlockSpec((MOD_ROWS, tn), lambda b, i, j: (0, 2 * ntile + j)),
                  pl.BlockSpec((None, D_MIX, tn), lambda b, i, j: (layer, 0, j)), xblk],
        out_specs=xblk,
        out_shape=jax.ShapeDtypeStruct(xs.shape, F32),
        scratch_shapes=[pltpu.VMEM((tm, D_MIX), BF16)],
        input_output_aliases={9: 0},
        compiler_params=_params(("arbitrary",) * 3, 56),
        name="out_proj",
    )(o_a, o_d, o_s, hf, hb, u, u, modl, w_bf, xs)


def _split2(x):
    hi = x.astype(BF16)
    return hi, (x - hi.astype(F32)).astype(BF16)


ROUTER_KC = 512


HI16 = -65536


def _pack_bf16_pair(lo, hi):
    bits = lambda v: lax.bitcast_convert_type(v.astype(BF16).astype(F32), I32)
    return (bits(hi) & HI16) | lax.shift_right_logical(bits(lo), 16)


def _unpack_bf16_pair(w):
    lo = lax.bitcast_convert_type(lax.shift_left(w, 16), F32)
    hi = lax.bitcast_convert_type(w & HI16, F32)
    return lo.astype(BF16), hi.astype(BF16)


def _router_kernel(x_ref, g_ref, sh_ref, sc_ref, w0_ref, w1_ref, xm_ref, aff_ref, xf_ref, *, n_batch, lc, tm):
    b = pl.program_id(0)
    i = pl.program_id(1)
    d = x_ref.shape[1]
    half = d // 2

    def chunk(r0):
        rs = pl.ds(r0, ROW_CHUNK)
        sh = _mod_row(sh_ref, i * tm + r0, b, n_batch, lc)
        sc = _mod_row(sc_ref, i * tm + r0, b, n_batch, lc)
        xm = _norm_modulate(x_ref[rs, :], g_ref[...], sh, sc)
        xf_ref[rs, :] = xm
        xm_ref[rs, 0:half] = _pack_bf16_pair(xm[:, 0:half], xm[:, half:d])

    _for_row_chunks(tm, chunk)

    logits = jnp.zeros((tm, LANE), F32)
    tk = _tile(d, ROUTER_KC)
    for kc in range(d // tk):
        ks = slice(kc * tk, (kc + 1) * tk)
        x0, x1 = _split2(xf_ref[:, ks])
        logits = logits + (_dot(x0, w0_ref[ks, :]) + (_dot(x0, w1_ref[ks, :]) + _dot(x1, w0_ref[ks, :])))
    lane = lax.broadcasted_iota(I32, (1, LANE), 1)
    logits = jnp.where(lane < N_EXPERTS, logits, NEG_INF)
    m = jnp.max(logits, axis=-1, keepdims=True)
    p = jnp.exp(logits - m)
    aff = p / jnp.sum(p, axis=-1, keepdims=True)
    aff_ref[...] = aff
    xm_ref[:, half:half + LANE] = lax.bitcast_convert_type(aff, I32)


def _router(xs, g, modl, wr_pad, lc):
    n_batch, nt, d = xs.shape
    tm = _tile(nt, 256)
    assert lc % ROW_CHUNK == 0
    kern = functools.partial(_router_kernel, n_batch=n_batch, lc=lc, tm=tm)
    wspec = pl.BlockSpec((d, LANE), lambda b, i: (0, 0))
    return pl.pallas_call(
        kern,
        grid=(n_batch, nt // tm),
        in_specs=[pl.BlockSpec((None, tm, d), lambda b, i: (b, i, 0)),
                  pl.BlockSpec((1, d), lambda b, i: (0, 0)),
                  pl.BlockSpec((MOD_ROWS, d), lambda b, i: (0, 3)),
                  pl.BlockSpec((MOD_ROWS, d), lambda b, i: (0, 4)),
                  wspec, wspec],
        out_specs=[pl.BlockSpec((None, tm, d // 2 + LANE), lambda b, i: (b, i, 0)),
                   pl.BlockSpec((None, tm, LANE), lambda b, i: (b, i, 0))],
        out_shape=[jax.ShapeDtypeStruct((n_batch, nt, d // 2 + LANE), I32),
                   jax.ShapeDtypeStruct((n_batch, nt, LANE), F32)],
        scratch_shapes=[pltpu.VMEM((tm, d), F32)],
        compiler_params=_params(("arbitrary", "arbitrary"), 48),
        name="norm2_router",
    )(xs, g, modl, modl, *_split2(wr_pad))


def _select_kernel(aff_ref, l_ref, cnt_ref, off_ref, bits_ref, *, row0, n, cap):
    nc = n // LANE
    bits_ref[...] = lax.bitcast_convert_type(aff_ref[row0:row0 + n, :], I32)

    def bit_step(i, thr):
        cand = thr | (jnp.int32(1) << (30 - i))
        cnt = jnp.sum((bits_ref[...] >= cand).astype(I32), axis=0, keepdims=True)
        return jnp.where(cnt >= cap, cand, thr)

    thr = lax.fori_loop(0, 31, bit_step, jnp.zeros((1, LANE), I32))
    n_gt = jnp.sum((bits_ref[...] > thr).astype(I32), axis=0, keepdims=True)
    need = (cap - n_gt).astype(F32)

    r_io = lax.broadcasted_iota(I32, (LANE, LANE), 0)
    c_io = lax.broadcasted_iota(I32, (LANE, LANE), 1)
    ltri = (c_io <= r_io).astype(BF16)
    t_col = r_io.astype(F32)
    j_row = c_io.astype(F32)

    def chunk(c, carry):
        run_eq, run_sel = carry
        r0 = pl.multiple_of(c * LANE, LANE)
        bc = bits_ref[pl.ds(r0, LANE), :]
        eq = bc == thr
        eq_f = eq.astype(F32)
        incl_eq = _dot(ltri, eq_f.astype(BF16))
        rank = run_eq + incl_eq - eq_f
        sel = (bc > thr) | (eq & (rank < need))
        sel_f = sel.astype(F32)
        incl = _dot(ltri, sel_f.astype(BF16))
        cnt_c = incl[LANE - 1:LANE, :]
        qm = jnp.where(sel, incl - sel_f, -1.0)
        for e in range(N_EXPERTS):
            pos = jnp.broadcast_to(qm[:, e:e + 1], (LANE, LANE))
            local = jnp.sum(jnp.where(pos == j_row, t_col, 0.0), axis=0, keepdims=True)
            l_ref[e, pl.ds(c, 1), :] = local.astype(I32)
        cnt_ref[pl.ds(c, 1), :] = cnt_c.astype(I32)
        off_ref[pl.ds(c, 1), :] = run_sel.astype(I32)
        return run_eq + incl_eq[LANE - 1:LANE, :], run_sel + cnt_c

    zero = jnp.zeros((1, LANE), F32)
    lax.fori_loop(0, nc, chunk, (zero, zero))


def _select(affp, row0, n, cap):
    n_batch, nt, _ = affp.shape
    nc = n // LANE
    kern = functools.partial(_select_kernel, row0=row0, n=n, cap=cap)
    return pl.pallas_call(
        kern,
        grid=(n_batch,),
        in_specs=[pl.BlockSpec((None, nt, LANE), lambda b: (b, 0, 0))],
        out_specs=[pl.BlockSpec((None, N_EXPERTS, nc, LANE), lambda b: (b, 0, 0, 0)),
                   pl.BlockSpec((None, nc, LANE), lambda b: (b, 0, 0)),
                   pl.BlockSpec((None, nc, LANE), lambda b: (b, 0, 0))],
        out_shape=[jax.ShapeDtypeStruct((n_batch, N_EXPERTS, nc, LANE), I32),
                   jax.ShapeDtypeStruct((n_batch, nc, LANE), I32),
                   jax.ShapeDtypeStruct((n_batch, nc, LANE), I32)],
        scratch_shapes=[pltpu.VMEM((n, LANE), I32)],
        compiler_params=_params(("arbitrary",), 40),
        name="ec_select",
    )(affp)


MOE_ROWS = 256


def _moe_kernel(*refs, sets, d):
    ns = len(sets)
    tab_hbm = refs[0:3 * ns]
    xm_hbm, _, g2_ref, w1_ref, w3_ref, w2_ref, xs_hbm = refs[3 * ns:3 * ns + 7]
    tab_s = refs[3 * ns + 7:6 * ns + 7]
    idx_s, xg, rows, xb, hdn_s, sem = refs[6 * ns + 7:]
    total = sum(cap for _, _, cap, _ in sets)
    b = pl.program_id(0)
    e = pl.program_id(1)
    step = b * pl.num_programs(1) + e
    last_step = pl.num_programs(0) * pl.num_programs(1) - 1

    tables = []
    for k in range(ns):
        l_hbm, cnt_hbm, off_hbm = tab_hbm[3 * k:3 * k + 3]
        l_s, cnt_s, off_s = tab_s[3 * k:3 * k + 3]
        tables += [pltpu.make_async_copy(l_hbm.at[b, e], l_s, sem.at[3 + 3 * k]),
                   pltpu.make_async_copy(cnt_hbm.at[b], cnt_s, sem.at[4 + 3 * k]),
                   pltpu.make_async_copy(off_hbm.at[b], off_s, sem.at[5 + 3 * k])]
    for cp in tables:
        cp.start()
    for cp in tables:
        cp.wait()

    def x_gather(tok, slot, n_rows=1):
        return pltpu.make_async_copy(xm_hbm.at[b, pl.ds(tok, n_rows), :], xg.at[pl.ds(slot, n_rows), :], sem.at[0])

    def row_gather(tok, slot, n_rows=1):
        return pltpu.make_async_copy(xs_hbm.at[b, pl.ds(tok, n_rows), :], rows.at[pl.ds(slot, n_rows), :], sem.at[1])

    def row_scatter(tok, slot, n_rows=1):
        return pltpu.make_async_copy(rows.at[pl.ds(slot, n_rows), :], xs_hbm.at[b, pl.ds(tok, n_rows), :], sem.at[2])

    slot0 = 0
    for k, (row0, nc, cap, _) in enumerate(sets):
        l_s, cnt_s, off_s = tab_s[3 * k:3 * k + 3]

        def chunk(c, _, row0=row0, l_s=l_s, cnt_s=cnt_s, off_s=off_s, slot0=slot0):
            base = row0 + c * LANE
            off_c = slot0 + off_s[c, e]

            def one(j, _):
                idx_s[off_c + j] = base + l_s[c, j]
                return 0

            lax.fori_loop(0, cnt_s[c, e], one, 0)
            return 0

        lax.fori_loop(0, nc, chunk, 0)
        slot0 += cap

    def start_all(copy):
        def one(s, _):
            copy(idx_s[s], s).start()
            return 0

        lax.fori_loop(0, total, one, 0, unroll=8)

    start_all(x_gather)

    @pl.when(step > 0)
    def _():
        row_scatter(0, 0, total).wait()

    start_all(row_gather)
    x_gather(0, 0, total).wait()

    tn = _tile(d, 512)
    half = d // 2
    th = _tile(half, 512)
    lane = lax.broadcasted_iota(I32, (1, LANE), 1)

    def hidden(rs, size):
        for kc in range(half // th):
            ks = slice(kc * th, (kc + 1) * th)
            lo, hi = _unpack_bf16_pair(xg[rs, ks])
            xb[0:size, ks] = lo
            xb[0:size, half + kc * th:half + (kc + 1) * th] = hi
        h1 = _dot(xb[0:size, :], w1_ref[...])
        h3 = _dot(xb[0:size, :], w3_ref[...])
        aff = lax.bitcast_convert_type(xg[rs, half:half + LANE], F32)
        gate = jnp.sum(jnp.where(lane == e, aff, 0.0), axis=-1, keepdims=True)
        hdn_s[rs, :] = (h1 * jax.nn.sigmoid(h1) * h3 * gate).astype(BF16)

    def down(rs, g_row):
        g2_row = g_row if g_row is not None else b
        for c in range(d // tn):
            sl = slice(c * tn, (c + 1) * tn)
            rows[rs, sl] = rows[rs, sl] + g2_ref[pl.ds(g2_row, 1), sl] * _dot(hdn_s[rs, :], w2_ref[:, sl])

    def for_blocks(fn):
        slot0 = 0
        for _, _, cap, g_row in sets:
            rc = min(cap, MOE_ROWS)
            if cap // rc > 1:
                def body(r, _, slot0=slot0, rc=rc, g_row=g_row):
                    fn(pl.ds(pl.multiple_of(slot0 + r * rc, rc), rc), rc, g_row)
                    return 0

                lax.fori_loop(0, cap // rc, body, 0)
            else:
                fn(slice(slot0, slot0 + rc), rc, g_row)
            slot0 += cap

    for_blocks(lambda rs, size, g_row: hidden(rs, size))
    row_gather(0, 0, total).wait()
    for_blocks(lambda rs, size, g_row: down(rs, g_row))
    start_all(row_scatter)

    @pl.when(step == last_step)
    def _():
        row_scatter(0, 0, total).wait()


def _moe(token_sets, xma, xs, modl, w1, w3, w2, layer):
    n_batch, nt, d = xs.shape
    sets = tuple((row0, tabs[0].shape[2], cap, g_row) for tabs, row0, cap, g_row in token_sets)
    total = sum(cap for _, _, cap, _ in sets)
    for _, _, cap, _ in sets:
        assert cap % min(cap, MOE_ROWS) == 0 and cap % 8 == 0
    rc = min(max(cap for _, _, cap, _ in sets), MOE_ROWS)
    kern = functools.partial(_moe_kernel, sets=sets, d=d)
    any_ = pl.BlockSpec(memory_space=pl.ANY)
    tables = [t for tabs, _, _, _ in token_sets for t in tabs]
    smem_tables = [pltpu.SMEM((nc, LANE), I32) for _, nc, _, _ in sets for _ in range(3)]
    return pl.pallas_call(
        kern,
        grid=(n_batch, N_EXPERTS),
        in_specs=[any_] * len(tables) + [
            any_, any_,
            pl.BlockSpec((MOD_ROWS, d), lambda b, e: (0, 5)),
            pl.BlockSpec((None, None, d, EXPERT_FF), lambda b, e: (layer, e, 0, 0)),
            pl.BlockSpec((None, None, d, EXPERT_FF), lambda b, e: (layer, e, 0, 0)),
            pl.BlockSpec((None, None, EXPERT_FF, d), lambda b, e: (layer, e, 0, 0))],
        out_specs=any_,
        out_shape=jax.ShapeDtypeStruct(xs.shape, F32),
        scratch_shapes=smem_tables + [
            pltpu.SMEM((total,), I32), pltpu.VMEM((total, d // 2 + LANE), I32), pltpu.VMEM((total, d), F32),
            pltpu.VMEM((rc, d), BF16), pltpu.VMEM((total, EXPERT_FF), BF16),
            pltpu.SemaphoreType.DMA((3 + 3 * len(sets),))],
        input_output_aliases={len(tables) + 1: 0},
        compiler_params=_params(("arbitrary", "arbitrary"), 56),
        name="moe_ffn",
    )(*tables, xma, xs, modl, w1, w3, w2)


def _rope_tables(n, lc):
    t = jnp.arange(n)
    pos = jnp.stack([t // GRID_W, t % GRID_W], axis=-1).astype(F32)

    def tables(dh):
        quarter = dh // 4
        inv_freq = ROPE_THETA ** (-jnp.arange(quarter, dtype=F32) / quarter)
        ang = pos[:, :, None] * inv_freq
        cos, sin = jnp.cos(ang), jnp.sin(ang)
        c = jnp.concatenate([cos[:, 0], cos[:, 0], cos[:, 1], cos[:, 1]], axis=-1)
        s = jnp.concatenate([-sin[:, 0], sin[:, 0], -sin[:, 1], sin[:, 1]], axis=-1)
        reps = LANE // dh
        c, s = jnp.tile(c, (1, reps)), jnp.tile(s, (1, reps))
        return (jnp.concatenate([jnp.ones((lc, LANE), F32), c], axis=0),
                jnp.concatenate([jnp.zeros((lc, LANE), F32), s], axis=0))

    return tables(HEAD_DIM) + tables(DIFF_QK_DIM)


def kernel(x, c, ctx, c_ctx, w_ada, b_ada, norm_g, w_in, w_out, qk_g_na, na_rpb, qk_g_diff, diff_lambda,
           diff_subln_g, qk_g_swa, swa_sink, lru_conv_w, lru_conv_b, lru_gate_w, lru_gate_b, lru_lam,
           w_router, w1, w3, w2):
    n_batch, n, d = x.shape
    lc = ctx.shape[1]
    depth = w_ada.shape[0]
    assert n_batch < MOD_ROWS and w_in.shape[2] == D_IN

    cvec = jnp.zeros((MOD_ROWS, d), F32).at[:n_batch].set(c).at[n_batch].set(c_ctx)
    mod = _ada(cvec, w_ada, b_ada)
    cs, ss, cd, sd = _rope_tables(n, lc)
    xs = jnp.concatenate([ctx, x], axis=1)

    cap_lat = CAPACITY_FACTOR * n // N_EXPERTS
    cap_ctx = CAPACITY_FACTOR * lc // N_EXPERTS
    w_in_b, w_out_b = w_in.astype(BF16), w_out.astype(BF16)
    w1b, w3b, w2b = w1.astype(BF16), w3.astype(BF16), w2.astype(BF16)

    for l in range(depth):
        modl = mod[l]
        u_att, u_lru = _inproj(xs, norm_g[l, 0].reshape(1, d), modl, w_in_b, l, lc)

        pvec = jnp.concatenate([qk_g_na[l], jnp.tile(qk_g_diff[l], (1, 2)), qk_g_swa[l],
                                diff_subln_g[l].reshape(1, LANE), jnp.zeros((1, LANE), F32)], axis=0)
        qa, ka, va, qd, kd, vd, qs, ks, vs = _prep(u_att, pvec, cs, ss, cd, sd)

        gmax = lambda g: jnp.max(jnp.abs(g[0])) * jnp.max(jnp.abs(g[1]))
        b_na = (BOUND_SLACK * HEAD_DIM * ATTN_Q_SCALE * gmax(qk_g_na[l])
                + LOG2E * jnp.max(jnp.abs(na_rpb[l])))
        b_swa = jnp.maximum(BOUND_SLACK * HEAD_DIM * ATTN_Q_SCALE * gmax(qk_g_swa[l]),
                            LOG2E * jnp.max(jnp.abs(swa_sink[l])))
        b_diff = BOUND_SLACK * DIFF_QK_DIM * DIFF_Q_SCALE * gmax(qk_g_diff[l])

        o_a = _na(b_na, qa, ka, va, _na_bias_tables(na_rpb[l], n), lc)

        lambda_init = 0.8 - 0.6 * math.exp(-0.3 * l)
        dl = diff_lambda[l].astype(F32)
        lam = jnp.exp(jnp.sum(dl[0] * dl[1])) - jnp.exp(jnp.sum(dl[2] * dl[3])) + lambda_init
        o_d = _diff(jnp.stack([lam, b_diff]).astype(F32), qd, kd, vd, pvec, lc, 1.0 - lambda_init)

        o_s = _swa(b_swa, qs, ks, vs, swa_sink[l], lc)

        af, bf, ab, bb = _lru_gates(u_lru, lru_conv_w[l], lru_conv_b[l], lru_gate_w[l], lru_gate_b[l], lru_lam[l], lc)
        hf, hb = _lru_scan(af, bf, ab, bb, lc)

        xs = _outproj(o_a, o_d, o_s, hf, hb, u_lru, modl, w_out_b, l, xs, lc)

        wr_pad = jnp.zeros((d, LANE), F32).at[:, :N_EXPERTS].set(w_router[l])
        xma, affp = _router(xs, norm_g[l, 1].reshape(1, d), modl, wr_pad, lc)
        token_sets = [(_select(affp, lc, n, cap_lat), lc, cap_lat, None)]
        if l < depth - 1:
            token_sets.append((_select(affp, 0, lc, cap_ctx), 0, cap_ctx, n_batch))
        xs = _moe(tuple(token_sets), xma, xs, modl, w1b, w3b, w2b, l)
    return xs[:, lc:, :]
```

```python
import functools
import math

import numpy as np
import jax
import jax.numpy as jnp
from jax import lax
from jax.experimental import pallas as pl
from jax.experimental.pallas import tpu as pltpu

GRID_W = 64
HEAD_DIM = 128
NA_HEADS = 8
NA_KH = 8
NA_KW = 16
DIFF_HEADS = 8
DIFF_QK_DIM = 64
SWA_Q_HEADS = 8
SWA_KV_HEADS = 2
SWA_GROUP = SWA_Q_HEADS // SWA_KV_HEADS
SWA_WINDOW = 128
LRU_WIDTH = 1024
LRU_BLOCKS = 8
LRU_C = 8.0
N_EXPERTS = 16
EXPERT_FF = 256
CAPACITY_FACTOR = 2
ROPE_THETA = 10000.0
NEG_INF = -1e30
EPS = 1e-6

OFF_QA, OFF_KA, OFF_VA = 0, 1024, 2048
OFF_QD, OFF_KD, OFF_VD = 3072, 4096, 5120
OFF_QS, OFF_KS, OFF_VS = 6144, 7168, 7424
OFF_XR, OFF_GR = 7680, 8704
D_ATT = 7680
D_IN = 9728
D_MIX = 4096

LOG2E = 1.4426950408889634
DIFF_Q_SCALE = DIFF_QK_DIM ** -0.5 * LOG2E
ATTN_FAST_BOUND = 60.0
ATTN_Q_SCALE = HEAD_DIM ** -0.5 * LOG2E
BOUND_SLACK = 1.02

LANE = 128
MOD_ROWS = 8
V7X_VMEM_BYTES = 64 * 1024 * 1024

F32 = jnp.float32
BF16 = jnp.bfloat16
I32 = jnp.int32


def _tile(n, target, mult=LANE):
    best = None
    for t in range(mult, min(n, target) + 1, mult):
        if n % t == 0:
            best = t
    assert best is not None, (n, target, mult)
    return best


def _params(sem, vmem_mb):
    return pltpu.CompilerParams(dimension_semantics=sem, vmem_limit_bytes=min(vmem_mb, 60) * 1024 * 1024)


def _dot_nt(a, b):
    return lax.dot_general(a, b, (((1,), (1,)), ((), ())), preferred_element_type=F32)


def _dot(a, b):
    return jnp.dot(a, b, preferred_element_type=F32)


def _ada_kernel(c_ref, w_ref, b_ref, o_ref):
    cv = c_ref[...]
    s = cv * jax.nn.sigmoid(cv)
    s_hi = s.astype(BF16)
    s_lo = (s - s_hi.astype(F32)).astype(BF16)
    w = w_ref[...].astype(BF16)
    o_ref[...] = _dot(s_hi, w) + _dot(s_lo, w) + b_ref[...]


def _ada(cvec, w_ada, b_ada):
    depth, d, n6 = w_ada.shape
    tn = _tile(n6, 1024)
    return pl.pallas_call(
        _ada_kernel,
        grid=(depth, n6 // tn),
        in_specs=[pl.BlockSpec((MOD_ROWS, d), lambda l, j: (0, 0)),
                  pl.BlockSpec((None, d, tn), lambda l, j: (l, 0, j)),
                  pl.BlockSpec((None, 1, tn), lambda l, j: (l, 0, j))],
        out_specs=pl.BlockSpec((None, MOD_ROWS, tn), lambda l, j: (l, 0, j)),
        out_shape=jax.ShapeDtypeStruct((depth, MOD_ROWS, n6), F32),
        compiler_params=_params(("arbitrary", "arbitrary"), 56),
        name="ada_mod",
    )(cvec, w_ada, b_ada.reshape(depth, 1, n6))


ROW_CHUNK = 64


def _mod_row(ref, row0, b, n_batch, lc):
    return jnp.where(row0 < lc, ref[pl.ds(n_batch, 1), :], ref[pl.ds(b, 1), :])


def _norm_modulate(x, g, sh, sc):
    ms = jnp.mean(x * x, axis=-1, keepdims=True)
    return x * lax.rsqrt(ms + EPS) * g * (1.0 + sc) + sh


def _for_row_chunks(tm, fn):
    assert tm % ROW_CHUNK == 0

    def body(c, _):
        fn(pl.multiple_of(c * ROW_CHUNK, ROW_CHUNK))
        return 0

    lax.fori_loop(0, tm // ROW_CHUNK, body, 0)


def _inproj_kernel(x_ref, g_ref, sh_ref, sc_ref, w_ref, att_ref, lru_ref, xn_ref, *, n_batch, lc, tm, n_att):
    b = pl.program_id(0)
    i = pl.program_id(1)
    j = pl.program_id(2)

    @pl.when(j == 0)
    def _():
        def chunk(r0):
            sh = _mod_row(sh_ref, i * tm + r0, b, n_batch, lc)
            sc = _mod_row(sc_ref, i * tm + r0, b, n_batch, lc)
            xn = _norm_modulate(x_ref[pl.ds(r0, ROW_CHUNK), :], g_ref[...], sh, sc)
            xn_ref[pl.ds(r0, ROW_CHUNK), :] = xn.astype(BF16)

        _for_row_chunks(tm, chunk)

    @pl.when(j < n_att)
    def _():
        att_ref[...] = _dot(xn_ref[...], w_ref[...]).astype(BF16)

    @pl.when(j >= n_att)
    def _():
        lru_ref[...] = _dot(xn_ref[...], w_ref[...])


def _inproj(xs, g, modl, w_bf, layer, lc):
    n_batch, nt, d = xs.shape
    d_in = w_bf.shape[2]
    tm = _tile(nt, 768)
    tn = _tile(math.gcd(D_ATT, d_in - D_ATT), 512)
    n_att = D_ATT // tn
    assert lc % ROW_CHUNK == 0
    kern = functools.partial(_inproj_kernel, n_batch=n_batch, lc=lc, tm=tm, n_att=n_att)
    return pl.pallas_call(
        kern,
        grid=(n_batch, nt // tm, d_in // tn),
        in_specs=[pl.BlockSpec((None, tm, d), lambda b, i, j: (b, i, 0)),
                  pl.BlockSpec((1, d), lambda b, i, j: (0, 0)),
                  pl.BlockSpec((MOD_ROWS, d), lambda b, i, j: (0, 0)),
                  pl.BlockSpec((MOD_ROWS, d), lambda b, i, j: (0, 1)),
                  pl.BlockSpec((None, d, tn), lambda b, i, j: (layer, 0, j))],
        out_specs=[pl.BlockSpec((None, tm, tn), lambda b, i, j: (b, i, jnp.minimum(j, n_att - 1))),
                   pl.BlockSpec((None, tm, tn), lambda b, i, j: (b, i, jnp.maximum(j - n_att, 0)))],
        out_shape=[jax.ShapeDtypeStruct((n_batch, nt, D_ATT), BF16),
                   jax.ShapeDtypeStruct((n_batch, nt, d_in - D_ATT), F32)],
        scratch_shapes=[pltpu.VMEM((tm, d), BF16)],
        compiler_params=_params(("arbitrary", "arbitrary", "arbitrary"), 56),
        name="in_proj",
    )(xs, g, modl, modl, w_bf)


def _rms_full(x, g):
    ms = jnp.mean(x * x, axis=-1, keepdims=True)
    return x * lax.rsqrt(ms + EPS) * g


def _group_ones(width):
    r = lax.broadcasted_iota(I32, (LANE, LANE), 0) // width
    c = lax.broadcasted_iota(I32, (LANE, LANE), 1) // width
    return jnp.where(r == c, 1.0, 0.0).astype(BF16)


def _rms_groups(x, g, ones, width):
    x2 = x * x
    hi = x2.astype(BF16)
    lo = (x2 - hi.astype(F32)).astype(BF16)
    ms = (_dot(hi, ones) + _dot(lo, ones)) * (1.0 / width)
    return x * lax.rsqrt(ms + EPS) * g


def _rope(x, cos, sin_signed, half):
    lane = lax.broadcasted_iota(I32, (1, LANE), 1)
    up = pltpu.roll(x, LANE - half, axis=1)
    dn = pltpu.roll(x, half, axis=1)
    partner = jnp.where((lane & half) == 0, up, dn)
    return x * cos + partner * sin_signed


def _prep_kernel(u_ref, p_ref, cs_ref, ss_ref, cd_ref, sd_ref,
                 qa_ref, ka_ref, va_ref, qd_ref, kd_ref, vd_ref, qs_ref, ks_ref, vs_ref):
    g_na_q, g_na_k = p_ref[0:1, :], p_ref[1:2, :]
    g_d_q, g_d_k = p_ref[2:3, :], p_ref[3:4, :]
    g_s_q, g_s_k = p_ref[4:5, :], p_ref[5:6, :]
    cs, ss, cd, sd = cs_ref[...], ss_ref[...], cd_ref[...], sd_ref[...]
    lo = lax.broadcasted_iota(I32, (1, LANE), 1) < DIFF_QK_DIM

    def col(off, h):
        return u_ref[:, off + h * LANE: off + (h + 1) * LANE].astype(F32)

    def put(ref, h, val):
        ref[:, h * LANE:(h + 1) * LANE] = val.astype(BF16)

    ones_head = _group_ones(HEAD_DIM)
    ones_half = _group_ones(DIFF_QK_DIM)
    rms_head = lambda x, g: _rms_groups(x, g, ones_head, HEAD_DIM)
    rms_half = lambda x, g: _rms_groups(x, g, ones_half, DIFF_QK_DIM)

    for h in range(NA_HEADS):
        put(qa_ref, h, rms_head(col(OFF_QA, h), g_na_q) * ATTN_Q_SCALE)
        put(ka_ref, h, rms_head(col(OFF_KA, h), g_na_k))
        put(va_ref, h, col(OFF_VA, h))
    ones_col = jnp.where(lax.broadcasted_iota(I32, (u_ref.shape[0], LANE), 1) == 0, 1.0, 0.0).astype(BF16)
    for h in range(DIFF_HEADS):
        q = _rope(rms_half(col(OFF_QD, h), g_d_q), cd, sd, DIFF_QK_DIM // 4) * DIFF_Q_SCALE
        qd_ref[0, :, h * LANE:(h + 1) * LANE] = jnp.where(lo, q, 0.0).astype(BF16)
        qd_ref[1, :, h * LANE:(h + 1) * LANE] = jnp.where(lo, 0.0, q).astype(BF16)
        put(kd_ref, h, _rope(rms_half(col(OFF_KD, h), g_d_k), cd, sd, DIFF_QK_DIM // 4))
        put(vd_ref, 2 * h, col(OFF_VD, h))
        vd_ref[:, (2 * h + 1) * LANE:(2 * h + 2) * LANE] = ones_col
    for h in range(SWA_Q_HEADS):
        put(qs_ref, h, _rope(rms_head(col(OFF_QS, h), g_s_q), cs, ss, HEAD_DIM // 4) * ATTN_Q_SCALE)
    for h in range(SWA_KV_HEADS):
        put(ks_ref, h, _rope(rms_head(col(OFF_KS, h), g_s_k), cs, ss, HEAD_DIM // 4))
        put(vs_ref, h, col(OFF_VS, h))


def _prep(u, pvec, cs, ss, cd, sd):
    n_batch, nt, _ = u.shape
    tr = _tile(nt, 256)
    row = lambda w: pl.BlockSpec((None, tr, w), lambda b, i: (b, i, 0))
    tab = pl.BlockSpec((tr, LANE), lambda b, i: (i, 0))
    sd_ = lambda w: jax.ShapeDtypeStruct((n_batch, nt, w), BF16)
    return pl.pallas_call(
        _prep_kernel,
        grid=(n_batch, nt // tr),
        in_specs=[row(D_ATT), pl.BlockSpec((8, LANE), lambda b, i: (0, 0)), tab, tab, tab, tab],
        out_specs=[row(1024), row(1024), row(1024),
                   pl.BlockSpec((None, 2, tr, 1024), lambda b, i: (b, 0, i, 0)),
                   row(1024), row(2048), row(1024), row(256), row(256)],
        out_shape=[sd_(1024), sd_(1024), sd_(1024),
                   jax.ShapeDtypeStruct((n_batch, 2, nt, 1024), BF16),
                   sd_(1024), sd_(2048), sd_(1024), sd_(256), sd_(256)],
        compiler_params=_params(("arbitrary", "arbitrary"), 48),
        name="qkv_prep",
    )(u, pvec, cs, ss, cd, sd)


NA_QBLK = 2 * GRID_W
NA_KROWS = 10
NA_KBLK = NA_KROWS * GRID_W


def _skewed(n_chains, scores, numerators, output):
    s_prev = p_prev = None
    for c in range(n_chains + 2):
        s_new = scores(c) if c < n_chains else None
        p_new = numerators(c - 1, s_prev) if s_prev is not None else None
        if p_prev is not None:
            output(c - 2, p_prev)
        s_prev, p_prev = s_new, p_new


def _na_kernel(sc_ref, q_ref, k_ref, v_ref, *rest, lc, nlb, group, online):
    bias_refs, o_ref = rest[:group], rest[group]
    i0 = pl.program_id(2) * group
    kc = k_ref[0:lc, :]
    vc = v_ref[0:lc, :]

    def window_start(g):
        li = i0 + g - lc // NA_QBLK
        kb = jnp.clip(li - 2, 0, nlb - NA_KROWS // 2)
        return pl.multiple_of(lc + kb * NA_QBLK, NA_QBLK)

    def scores(g):
        q = q_ref[g * NA_QBLK:(g + 1) * NA_QBLK, :]
        return _dot_nt(q, k_ref[pl.ds(window_start(g), NA_KBLK), :]) + bias_refs[g][...], _dot_nt(q, kc)

    def numerators(g, s):
        s_n, s_c = s
        if online:
            m = jnp.maximum(jnp.max(s_n, axis=-1, keepdims=True), jnp.max(s_c, axis=-1, keepdims=True))
        else:
            m = sc_ref[0]
        p_n = jnp.exp2(s_n - m)
        p_c = jnp.exp2(s_c - m)
        l = jnp.sum(p_n, axis=-1, keepdims=True) + jnp.sum(p_c, axis=-1, keepdims=True)
        return p_n.astype(BF16), p_c.astype(BF16), l

    def output(g, p):
        o = _dot(p[0], v_ref[pl.ds(window_start(g), NA_KBLK), :]) + _dot(p[1], vc)
        o_ref[g * NA_QBLK:(g + 1) * NA_QBLK, :] = (o / p[2]).astype(BF16)

    _skewed(group, scores, numerators, output)


def _na_bias_tables(rpb, n):
    rows = n // GRID_W
    nlb = rows // 2
    reps = [0, 1, 2, nlb - 2, nlb - 1]
    h = rpb.shape[0]
    cols = np.arange(GRID_W)
    cstart = np.clip(cols - NA_KW // 2, 0, GRID_W - NA_KW)
    col_ok = (cols[None, :] >= cstart[:, None]) & (cols[None, :] < cstart[:, None] + NA_KW)
    dx = cols[None, :] - cols[:, None] + (NA_KW - 1)
    sel = (dx[None] == np.arange(2 * NA_KW - 1)[:, None, None]) & col_ok[None]
    tx = jnp.einsum('hyx,xqk->hyqk', rpb.astype(F32), jnp.asarray(sel, F32), precision=lax.Precision.HIGHEST)
    dy_i = np.zeros((5, 2, NA_KROWS), np.int32)
    row_ok = np.zeros((5, 2, NA_KROWS), bool)
    for c, li in enumerate(reps):
        kb = int(np.clip(li - 2, 0, nlb - NA_KROWS // 2))
        qr = 2 * li + np.arange(2)
        kr = 2 * kb + np.arange(NA_KROWS)
        rs = np.clip(qr - NA_KH // 2, 0, rows - NA_KH)
        row_ok[c] = (kr[None, :] >= rs[:, None]) & (kr[None, :] < rs[:, None] + NA_KH)
        dy_i[c] = np.clip(kr[None, :] - qr[:, None] + (NA_KH - 1), 0, 2 * NA_KH - 2)
    blocks = jnp.take(tx, jnp.asarray(dy_i.reshape(-1)), axis=1)
    blocks = blocks.reshape(h, 5, 2, NA_KROWS, GRID_W, GRID_W)
    ok = row_ok[:, :, :, None, None] & col_ok[None, None, None]
    tbl = jnp.where(ok[None], blocks * LOG2E, NEG_INF)
    tbl = tbl.transpose(1, 0, 2, 4, 3, 5).reshape(5, h, NA_QBLK, NA_KBLK)
    dead = jnp.full((1,) + tbl.shape[1:], NEG_INF, F32)
    return jnp.concatenate([tbl, dead], axis=0)


def _group(nblocks, target):
    return max(g for g in range(1, target + 1) if nblocks % g == 0)


def _softmax_dispatch(bound, call):
    sc = jnp.reshape(bound, (1,)).astype(F32)
    return lax.cond(bound <= ATTN_FAST_BOUND, lambda s: call(False)(s), lambda s: call(True)(s), sc)


def _na(bound, qa, ka, va, bias, layer, lc):
    n_batch, nt, _ = qa.shape
    n = nt - lc
    nlb = n // NA_QBLK
    assert nlb >= 5 and lc % NA_QBLK == 0
    ncb = lc // NA_QBLK
    group = _group(nt // NA_QBLK, 11)

    def case(i):
        li = i - ncb
        return jnp.where(li < 0, 5, jnp.where(li < 2, li, jnp.where(li >= nlb - 2, li - (nlb - 5), 2)))

    kv = pl.BlockSpec((None, nt, LANE), lambda b, h, i: (b, 0, h))
    qo = pl.BlockSpec((None, group * NA_QBLK, LANE), lambda b, h, i: (b, i, h))
    bias_specs = [pl.BlockSpec((None, None, None, NA_QBLK, NA_KBLK),
                               functools.partial(lambda b, h, i, g: (layer, case(i * group + g), h, 0, 0), g=g))
                  for g in range(group)]

    def call(online):
        kern = functools.partial(_na_kernel, lc=lc, nlb=nlb, group=group, online=online)
        return lambda sc: pl.pallas_call(
            kern,
            grid=(n_batch, NA_HEADS, nt // (group * NA_QBLK)),
            in_specs=[pl.BlockSpec(memory_space=pltpu.SMEM), qo, kv, kv] + bias_specs,
            out_specs=qo,
            out_shape=jax.ShapeDtypeStruct((n_batch, nt, NA_HEADS * HEAD_DIM), BF16),
            compiler_params=_params(("arbitrary",) * 3, 40),
            name="na_attn_online" if online else "na_attn_offset",
        )(sc, qa, ka, va, *([bias] * group))

    return _softmax_dispatch(bound, call)


SWA_BLK = 128
SWA_WIN = 3 * SWA_BLK


def _swa_kernel(sc_ref, q_ref, k_ref, v_ref, sink_ref, *rest, lc, n, group, online):
    mask_refs, o_ref = rest[:group], rest[group]
    i0 = pl.program_id(2) * group
    kc = k_ref[0:lc, :]
    vc = v_ref[0:lc, :]

    def window_start(c):
        li = i0 + c // SWA_GROUP - lc // SWA_BLK
        ws = jnp.clip((li - 1) * SWA_BLK, 0, n - SWA_WIN)
        return pl.multiple_of(lc + ws, SWA_BLK)

    def block(c):
        gq, g = divmod(c, SWA_GROUP)
        return slice(gq * SWA_BLK, (gq + 1) * SWA_BLK), slice(g * LANE, (g + 1) * LANE)

    def scores(c):
        rows, cols = block(c)
        q = q_ref[rows, cols]
        return _dot_nt(q, k_ref[pl.ds(window_start(c), SWA_WIN), :]), _dot_nt(q, kc)

    def numerators(c, s):
        s_w, s_c = s
        g = c % SWA_GROUP
        mask = mask_refs[c // SWA_GROUP][...]
        sink = sink_ref[g * SWA_BLK:(g + 1) * SWA_BLK, 0:1]
        if online:
            s_w = jnp.where(mask > 0.0, s_w, NEG_INF)
            m = jnp.maximum(jnp.maximum(jnp.max(s_w, axis=-1, keepdims=True),
                                        jnp.max(s_c, axis=-1, keepdims=True)), sink)
            p_w = jnp.exp2(s_w - m)
        else:
            m = sc_ref[0]
            p_w = jnp.exp2(s_w - m) * mask
        p_c = jnp.exp2(s_c - m)
        l = jnp.sum(p_w, axis=-1, keepdims=True) + jnp.sum(p_c, axis=-1, keepdims=True) + jnp.exp2(sink - m)
        return p_w.astype(BF16), p_c.astype(BF16), l

    def output(c, p):
        rows, cols = block(c)
        o = (_dot(p[0], v_ref[pl.ds(window_start(c), SWA_WIN), :]) + _dot(p[1], vc)) / p[2]
        o_ref[rows, cols] = o.astype(BF16)

    _skewed(group * SWA_GROUP, scores, numerators, output)


def _swa_masks():
    r = np.arange(SWA_BLK)[:, None]
    j = np.arange(SWA_WIN)[None, :]
    rel = [j - r, j - r - SWA_BLK, j - r - 2 * SWA_BLK]
    tabs = [(np.abs(d) <= SWA_WINDOW) for d in rel] + [np.zeros((SWA_BLK, SWA_WIN), bool)]
    return jnp.asarray(np.stack(tabs), F32)


def _swa(bound, qs, ks, vs, sink, lc):
    n_batch, nt, _ = qs.shape
    n = nt - lc
    nb = n // SWA_BLK
    assert nb >= 3 and lc % SWA_BLK == 0
    ncb = lc // SWA_BLK
    rows = SWA_GROUP * SWA_BLK
    group = _group(nt // SWA_BLK, 6)
    sink_rows = jnp.broadcast_to(
        jnp.repeat((sink.astype(F32) * LOG2E).reshape(SWA_KV_HEADS, SWA_GROUP), SWA_BLK, axis=1)[:, :, None],
        (SWA_KV_HEADS, rows, LANE))
    masks = _swa_masks()

    def case(i):
        li = i - ncb
        return jnp.where(li < 0, 3, jnp.where(li == 0, 0, jnp.where(li == nb - 1, 2, 1)))

    kv = pl.BlockSpec((None, nt, LANE), lambda b, k, i: (b, 0, k))
    qo = pl.BlockSpec((None, group * SWA_BLK, SWA_GROUP * LANE), lambda b, k, i: (b, i, k))
    mask_specs = [pl.BlockSpec((None, SWA_BLK, SWA_WIN),
                               functools.partial(lambda b, k, i, g: (case(i * group + g), 0, 0), g=g))
                  for g in range(group)]

    def call(online):
        kern = functools.partial(_swa_kernel, lc=lc, n=n, group=group, online=online)
        return lambda sc: pl.pallas_call(
            kern,
            grid=(n_batch, SWA_KV_HEADS, nt // (group * SWA_BLK)),
            in_specs=[pl.BlockSpec(memory_space=pltpu.SMEM), qo, kv, kv,
                      pl.BlockSpec((None, rows, LANE), lambda b, k, i: (k, 0, 0))] + mask_specs,
            out_specs=qo,
            out_shape=jax.ShapeDtypeStruct((n_batch, nt, SWA_Q_HEADS * HEAD_DIM), BF16),
            compiler_params=_params(("arbitrary",) * 3, 32),
            name="swa_attn_online" if online else "swa_attn_offset",
        )(sc, qs, ks, vs, sink_rows, *([masks] * group))

    return _softmax_dispatch(bound, call)


def _diff_finish(sc_ref, g_ref, o_ref, num, den, tq, post_scale):
    o2 = num / den
    o = o2[0:tq, :] - sc_ref[0] * o2[tq:2 * tq, :]
    o_ref[...] = (_rms_full(o, g_ref[6:7, :]) * post_scale).astype(BF16)


def _diff_online_kernel(sc_ref, q_ref, k_ref, v_ref, g_ref, o_ref, *, lc, nt, tq, tk, post_scale):
    qi = pl.program_id(2)
    q = q_ref[...].reshape(2 * tq, LANE)
    nk = jnp.where(qi < lc // tq, lc // tk, nt // tk)

    def body(kt, carry):
        m, l, acc = carry
        k0 = pl.multiple_of(kt * tk, tk)
        k = k_ref[pl.ds(k0, tk), :]
        v = v_ref[pl.ds(k0, tk), :]
        s = _dot_nt(q, k)
        m_new = jnp.maximum(m, jnp.max(s, axis=-1, keepdims=True))
        alpha = jnp.exp2(m - m_new)
        p = jnp.exp2(s - m_new)
        l = alpha * l + jnp.sum(p, axis=-1, keepdims=True)
        acc = alpha * acc + _dot(p.astype(BF16), v)
        return m_new, l, acc

    init = (jnp.full((2 * tq, 1), NEG_INF, F32), jnp.zeros((2 * tq, 1), F32), jnp.zeros((2 * tq, LANE), F32))
    _, l, acc = lax.fori_loop(0, nk, body, init)
    _diff_finish(sc_ref, g_ref, o_ref, acc, l, tq, post_scale)


def _diff_offset_kernel(sc_ref, q_ref, qn_ref, k_ref, v_ref, g_ref, o_ref, acc_ref, p_ref,
                        *, lc, nt, tq, tk, post_scale):
    qi = pl.program_id(2)
    rows2 = 2 * tq
    nk = nt // tk
    ncq = lc // tq
    q = q_ref[...].reshape(rows2, LANE)
    off = sc_ref[1]

    def probs(qq, k0, tkk):
        return jnp.exp2(_dot_nt(qq, k_ref[k0:k0 + tkk, :]) - off).astype(BF16)

    def next_tile_probs():
        return probs(qn_ref[...].reshape(rows2, LANE), 0, tk)

    @pl.when(qi < ncq)
    def _():
        acc_ref[...] = _dot(probs(q, 0, lc), v_ref[0:lc, :])

        @pl.when(qi == ncq - 1)
        def _():
            p_ref[0] = next_tile_probs()

    def latent(first_slot):
        tot = None
        for t in range(nk):
            pv = _dot(p_ref[(t + first_slot) % 2], v_ref[t * tk:(t + 1) * tk, :])
            p_ref[(t + 1 + first_slot) % 2] = probs(q, (t + 1) * tk, tk) if t + 1 < nk else next_tile_probs()
            tot = pv if tot is None else tot + pv
        acc_ref[...] = tot

    first_slot = ((qi - ncq) * nk) % 2
    for slot in range(2 if nk % 2 else 1):
        @pl.when((qi >= ncq) & (first_slot == slot))
        def _():
            latent(slot)

    acc = acc_ref[...]
    _diff_finish(sc_ref, g_ref, o_ref, acc[:, 0:LANE], acc[:, LANE:LANE + 1], tq, post_scale)


def _diff(sc, qd, kd, vde, pvec, lc, post_scale):
    n_batch, _, nt, _ = qd.shape
    tq = _tile(math.gcd(lc, nt), 256)
    assert lc % tq == 0
    grid = (n_batch, DIFF_HEADS, nt // tq)
    smem = pl.BlockSpec(memory_space=pltpu.SMEM)
    qspec = pl.BlockSpec((None, 2, tq, LANE), lambda b, h, i: (b, 0, i, h))
    kspec = pl.BlockSpec((None, nt, LANE), lambda b, h, i: (b, 0, h))
    gspec = pl.BlockSpec((8, LANE), lambda b, h, i: (0, 0))
    ospec = pl.BlockSpec((None, tq, LANE), lambda b, h, i: (b, i, h))
    oshape = jax.ShapeDtypeStruct((n_batch, nt, DIFF_HEADS * HEAD_DIM), BF16)

    def offset(sc, qd, kd, vde, pvec):
        tk = _tile(nt, 768)
        kern = functools.partial(_diff_offset_kernel, lc=lc, nt=nt, tq=tq, tk=tk, post_scale=post_scale)
        nq = nt // tq
        qnext = pl.BlockSpec((None, 2, tq, LANE), lambda b, h, i: (b, 0, jnp.minimum(i + 1, nq - 1), h))
        return pl.pallas_call(
            kern, grid=grid,
            in_specs=[smem, qspec, qnext, kspec, pl.BlockSpec((None, nt, 2 * LANE), lambda b, h, i: (b, 0, h)),
                      gspec],
            out_specs=ospec, out_shape=oshape,
            scratch_shapes=[pltpu.VMEM((2 * tq, 2 * LANE), F32), pltpu.VMEM((2, 2 * tq, tk), BF16)],
            compiler_params=_params(("arbitrary",) * 3, 40),
            name="diff_attn_offset",
        )(sc, qd, qd, kd, vde, pvec)

    def online(sc, qd, kd, vde, pvec):
        tk = _tile(math.gcd(lc, nt), 256)
        kern = functools.partial(_diff_online_kernel, lc=lc, nt=nt, tq=tq, tk=tk, post_scale=post_scale)
        return pl.pallas_call(
            kern, grid=grid,
            in_specs=[smem, qspec, kspec, pl.BlockSpec((None, nt, LANE), lambda b, h, i: (b, 0, 2 * h)), gspec],
            out_specs=ospec, out_shape=oshape,
            compiler_params=_params(("arbitrary",) * 3, 32),
            name="diff_attn_online",
        )(sc, qd, kd, vde, pvec)

    return lax.cond(sc[1] <= ATTN_FAST_BOUND, offset, online, sc, qd, kd, vde, pvec)


LRU_HALO = 8


LRU_STEP_BLOCKS = 4


def _lru_gate_kernel(xp_ref, x_ref, xn_ref, cw_ref, cb_ref, gw_ref, gb_ref, lam_ref,
                     af_ref, bf_ref, ab_ref, bb_ref, *, t, seg_starts, seg_ends):
    k = pl.program_id(2)
    is_start = functools.reduce(jnp.logical_or, [k == s for s in seg_starts])
    is_end = functools.reduce(jnp.logical_or, [k == s for s in seg_ends])
    h = LRU_HALO
    outs = ((af_ref, bf_ref), (ab_ref, bb_ref))
    sigmoid = lambda z: 0.5 + 0.5 * jnp.tanh(0.5 * z)
    for c in range(LRU_STEP_BLOCKS):
        cs = slice(c * LANE, (c + 1) * LANE)
        cur = x_ref[:, cs]
        prev = jnp.where(is_start, 0.0, xp_ref[:, cs])
        nxt = jnp.where(is_end, 0.0, xn_ref[:, cs])
        ext = jnp.concatenate([prev, cur, nxt], axis=0)
        xl = (cb_ref[:, cs] + cw_ref[0:1, cs] * ext[h - 2:h - 2 + t, :] + cw_ref[1:2, cs] * ext[h - 1:h - 1 + t, :]
              + cw_ref[2:3, cs] * cur + cw_ref[3:4, cs] * ext[h + 1:h + 1 + t, :])
        xb = xl.astype(BF16)
        for d in range(2):
            r = sigmoid(_dot(xb, gw_ref[d, 0, c].astype(BF16)) + gb_ref[2 * d:2 * d + 1, cs])
            ig = sigmoid(_dot(xb, gw_ref[d, 1, c].astype(BF16)) + gb_ref[2 * d + 1:2 * d + 2, cs])
            a = jnp.exp(LRU_C * r * jax.nn.log_sigmoid(lam_ref[d:d + 1, cs]))
            outs[d][0][:, cs] = a
            outs[d][1][:, cs] = jnp.sqrt(1.0 - a * a) * (ig * xl)


def _lru_gates(u, conv_w, conv_b, gate_w, gate_b, lam, lc):
    n_batch, nt, _ = u.shape
    t = _tile(math.gcd(lc, nt), 256)
    nk = nt // t
    hb = t // LRU_HALO
    seg_starts = (0, lc // t)
    seg_ends = (lc // t - 1, nk - 1)
    w = LRU_STEP_BLOCKS * LANE
    assert OFF_XR % w == 0 and LRU_WIDTH % w == 0
    cb0 = (OFF_XR - D_ATT) // w
    kern = functools.partial(_lru_gate_kernel, t=t, seg_starts=seg_starts, seg_ends=seg_ends)
    out = pl.BlockSpec((None, t, w), lambda b, c, k: (b, k, c))
    osd = jax.ShapeDtypeStruct((n_batch, nt, LRU_WIDTH), F32)
    return pl.pallas_call(
        kern,
        grid=(n_batch, LRU_WIDTH // w, nk),
        in_specs=[pl.BlockSpec((None, LRU_HALO, w), lambda b, c, k: (b, jnp.maximum(k * hb - 1, 0), cb0 + c)),
                  pl.BlockSpec((None, t, w), lambda b, c, k: (b, k, cb0 + c)),
                  pl.BlockSpec((None, LRU_HALO, w),
                               lambda b, c, k: (b, jnp.minimum((k + 1) * hb, nt // LRU_HALO - 1), cb0 + c)),
                  pl.BlockSpec((4, w), lambda b, c, k: (0, c)),
                  pl.BlockSpec((1, w), lambda b, c, k: (0, c)),
                  pl.BlockSpec((2, 2, LRU_STEP_BLOCKS, LANE, LANE), lambda b, c, k: (0, 0, c, 0, 0)),
                  pl.BlockSpec((4, w), lambda b, c, k: (0, c)),
                  pl.BlockSpec((2, w), lambda b, c, k: (0, c))],
        out_specs=[out, out, out, out],
        out_shape=[osd, osd, osd, osd],
        compiler_params=_params(("arbitrary",) * 3, 24),
        name="lru_gates",
    )(u, u, u, conv_w, conv_b.reshape(1, LRU_WIDTH), gate_w, gate_b.reshape(4, LRU_WIDTH), lam)


def _lru_scan_kernel(af_ref, bf_ref, ab_ref, bb_ref, hf_ref, hb_ref, cf_ref, cbk_ref, *, t):
    @pl.when(pl.program_id(1) == 0)
    def _():
        cf_ref[...] = jnp.zeros_like(cf_ref)
        cbk_ref[...] = jnp.zeros_like(cbk_ref)

    ng = t // 8
    row = lax.broadcasted_iota(I32, (8, 1), 0)

    def scan8(a, bt, h_in, reverse):
        for s in (1, 2, 4):
            shift = (8 - s) if reverse else s
            has = (row < 8 - s) if reverse else (row >= s)
            a_n = jnp.where(has, pltpu.roll(a, shift, axis=0), 1.0)
            b_n = jnp.where(has, pltpu.roll(bt, shift, axis=0), 0.0)
            bt = a * b_n + bt
            a = a * a_n
        h = a * h_in + bt
        return h, (h[0:1, :] if reverse else h[7:8, :])

    def body(g, carry):
        hf, hb = carry
        r0 = pl.multiple_of(g * 8, 8)
        h, hf = scan8(af_ref[pl.ds(r0, 8), :], bf_ref[pl.ds(r0, 8), :], hf, False)
        hf_ref[pl.ds(r0, 8), :] = h
        r1 = pl.multiple_of((ng - 1 - g) * 8, 8)
        h, hb = scan8(ab_ref[pl.ds(r1, 8), :], bb_ref[pl.ds(r1, 8), :], hb, True)
        hb_ref[pl.ds(r1, 8), :] = h
        return hf, hb

    hf, hb = lax.fori_loop(0, ng, body, (cf_ref[...], cbk_ref[...]))
    cf_ref[...] = hf
    cbk_ref[...] = hb


def _lru_scan(af, bf, ab, bb, lc):
    n_batch, nt, w = af.shape
    t = lc
    assert nt % t == 0
    nk = nt // t
    kern = functools.partial(_lru_scan_kernel, t=t)
    fwd = pl.BlockSpec((None, t, w), lambda b, k: (b, k, 0))
    bwd = pl.BlockSpec((None, t, w), lambda b, k: (b, jnp.where(k == 0, 0, nk - k), 0))
    osd = jax.ShapeDtypeStruct((n_batch, nt, w), F32)
    return pl.pallas_call(
        kern,
        grid=(n_batch, nk),
        in_specs=[fwd, fwd, bwd, bwd],
        out_specs=[fwd, bwd],
        out_shape=[osd, osd],
        scratch_shapes=[pltpu.VMEM((1, w), F32), pltpu.VMEM((1, w), F32)],
        compiler_params=_params(("arbitrary", "arbitrary"), 24),
        name="lru_scan",
    )(af, bf, ab, bb)


def _outproj_kernel(oa_ref, od_ref, os_ref, hf_ref, hb_ref, g0_ref, g1_ref, gate_ref, w_ref, x_ref,
                    o_ref, lhs_ref, *, n_batch, lc, tm):
    b = pl.program_id(0)
    i = pl.program_id(1)

    @pl.when(pl.program_id(2) == 0)
    def _():
        def chunk(r0):
            rs = pl.ds(r0, ROW_CHUNK)
            lhs_ref[rs, 0:1024] = oa_ref[rs, :]
            lhs_ref[rs, 1024:2048] = od_ref[rs, :]
            lhs_ref[rs, 2048:3072] = os_ref[rs, :]
            for c, g_ref in enumerate((g0_ref, g1_ref)):
                cs = slice(c * 512, (c + 1) * 512)
                y = (hf_ref[rs, cs] + hb_ref[rs, cs]) * jax.nn.gelu(g_ref[rs, :])
                lhs_ref[rs, 3072 + c * 512:3584 + c * 512] = y.astype(BF16)

        _for_row_chunks(tm, chunk)

    row = i * tm + lax.broadcasted_iota(I32, (tm, 1), 0)
    gate = jnp.where(row < lc, gate_ref[pl.ds(n_batch, 1), :], gate_ref[pl.ds(b, 1), :])
    o_ref[...] = x_ref[...] + gate * _dot(lhs_ref[...], w_ref[...])


def _outproj(o_a, o_d, o_s, hf, hb, u, modl, w_bf, layer, xs, lc):
    n_batch, nt, d = xs.shape
    tm = _tile(nt, 768)
    tn = _tile(d, 512)
    ntile = d // tn
    kern = functools.partial(_outproj_kernel, n_batch=n_batch, lc=lc, tm=tm)
    mix = pl.BlockSpec((None, tm, 1024), lambda b, i, j: (b, i, 0))
    gr = lambda c: pl.BlockSpec((None, tm, 512), lambda b, i, j: (b, i, (OFF_GR - D_ATT) // 512 + c))
    xblk = pl.BlockSpec((None, tm, tn), lambda b, i, j: (b, i, j))
    return pl.pallas_call(
        kern,
        grid=(n_batch, nt // tm, ntile),
        in_specs=[mix, mix, mix, mix, mix, gr(0), gr(1),
                  pl.BlockSpec((MOD_ROWS, tn), lambda b, i, j: (0, 2 * ntile + j)),
                  pl.BlockSpec((None, D_MIX, tn), lambda b, i, j: (layer, 0, j)), xblk],
        out_specs=xblk,
        out_shape=jax.ShapeDtypeStruct(xs.shape, F32),
        scratch_shapes=[pltpu.VMEM((tm, D_MIX), BF16)],
        input_output_aliases={9: 0},
        compiler_params=_params(("arbitrary",) * 3, 56),
        name="out_proj",
    )(o_a, o_d, o_s, hf, hb, u, u, modl, w_bf, xs)


def _split2(x):
    hi = x.astype(BF16)
    return hi, (x - hi.astype(F32)).astype(BF16)


ROUTER_KC = 512


HI16 = -65536


def _pack_bf16_pair(lo, hi):
    bits = lambda v: lax.bitcast_convert_type(v.astype(BF16).astype(F32), I32)
    return (bits(hi) & HI16) | lax.shift_right_logical(bits(lo), 16)


def _unpack_bf16_pair(w):
    lo = lax.bitcast_convert_type(lax.shift_left(w, 16), F32)
    hi = lax.bitcast_convert_type(w & HI16, F32)
    return lo.astype(BF16), hi.astype(BF16)


def _router_kernel(x_ref, g_ref, sh_ref, sc_ref, w0_ref, w1_ref, xm_ref, aff_ref, xf_ref, *, n_batch, lc, tm):
    b = pl.program_id(0)
    i = pl.program_id(1)
    d = x_ref.shape[1]
    half = d // 2

    def chunk(r0):
        rs = pl.ds(r0, ROW_CHUNK)
        sh = _mod_row(sh_ref, i * tm + r0, b, n_batch, lc)
        sc = _mod_row(sc_ref, i * tm + r0, b, n_batch, lc)
        xm = _norm_modulate(x_ref[rs, :], g_ref[...], sh, sc)
        xf_ref[rs, :] = xm
        xm_ref[rs, 0:half] = _pack_bf16_pair(xm[:, 0:half], xm[:, half:d])

    _for_row_chunks(tm, chunk)

    logits = jnp.zeros((tm, LANE), F32)
    tk = _tile(d, ROUTER_KC)
    for kc in range(d // tk):
        ks = slice(kc * tk, (kc + 1) * tk)
        x0, x1 = _split2(xf_ref[:, ks])
        logits = logits + (_dot(x0, w0_ref[ks, :]) + (_dot(x0, w1_ref[ks, :]) + _dot(x1, w0_ref[ks, :])))
    lane = lax.broadcasted_iota(I32, (1, LANE), 1)
    logits = jnp.where(lane < N_EXPERTS, logits, NEG_INF)
    m = jnp.max(logits, axis=-1, keepdims=True)
    p = jnp.exp(logits - m)
    aff = p / jnp.sum(p, axis=-1, keepdims=True)
    aff_ref[...] = aff
    xm_ref[:, half:half + LANE] = lax.bitcast_convert_type(aff, I32)


def _router(xs, g, modl, wr_pad, lc):
    n_batch, nt, d = xs.shape
    tm = _tile(nt, 256)
    assert lc % ROW_CHUNK == 0
    kern = functools.partial(_router_kernel, n_batch=n_batch, lc=lc, tm=tm)
    wspec = pl.BlockSpec((d, LANE), lambda b, i: (0, 0))
    return pl.pallas_call(
        kern,
        grid=(n_batch, nt // tm),
        in_specs=[pl.BlockSpec((None, tm, d), lambda b, i: (b, i, 0)),
                  pl.BlockSpec((1, d), lambda b, i: (0, 0)),
                  pl.BlockSpec((MOD_ROWS, d), lambda b, i: (0, 3)),
                  pl.BlockSpec((MOD_ROWS, d), lambda b, i: (0, 4)),
                  wspec, wspec],
        out_specs=[pl.BlockSpec((None, tm, d // 2 + LANE), lambda b, i: (b, i, 0)),
                   pl.BlockSpec((None, tm, LANE), lambda b, i: (b, i, 0))],
        out_shape=[jax.ShapeDtypeStruct((n_batch, nt, d // 2 + LANE), I32),
                   jax.ShapeDtypeStruct((n_batch, nt, LANE), F32)],
        scratch_shapes=[pltpu.VMEM((tm, d), F32)],
        compiler_params=_params(("arbitrary", "arbitrary"), 48),
        name="norm2_router",
    )(xs, g, modl, modl, *_split2(wr_pad))


def _select_kernel(aff_ref, l_ref, cnt_ref, off_ref, bits_ref, *, row0, n, cap):
    nc = n // LANE
    bits_ref[...] = lax.bitcast_convert_type(aff_ref[row0:row0 + n, :], I32)

    def bit_step(i, thr):
        cand = thr | (jnp.int32(1) << (30 - i))
        cnt = jnp.sum((bits_ref[...] >= cand).astype(I32), axis=0, keepdims=True)
        return jnp.where(cnt >= cap, cand, thr)

    thr = lax.fori_loop(0, 31, bit_step, jnp.zeros((1, LANE), I32))
    n_gt = jnp.sum((bits_ref[...] > thr).astype(I32), axis=0, keepdims=True)
    need = (cap - n_gt).astype(F32)

    r_io = lax.broadcasted_iota(I32, (LANE, LANE), 0)
    c_io = lax.broadcasted_iota(I32, (LANE, LANE), 1)
    ltri = (c_io <= r_io).astype(BF16)
    t_col = r_io.astype(F32)
    j_row = c_io.astype(F32)

    def chunk(c, carry):
        run_eq, run_sel = carry
        r0 = pl.multiple_of(c * LANE, LANE)
        bc = bits_ref[pl.ds(r0, LANE), :]
        eq = bc == thr
        eq_f = eq.astype(F32)
        incl_eq = _dot(ltri, eq_f.astype(BF16))
        rank = run_eq + incl_eq - eq_f
        sel = (bc > thr) | (eq & (rank < need))
        sel_f = sel.astype(F32)
        incl = _dot(ltri, sel_f.astype(BF16))
        cnt_c = incl[LANE - 1:LANE, :]
        qm = jnp.where(sel, incl - sel_f, -1.0)
        for e in range(N_EXPERTS):
            pos = jnp.broadcast_to(qm[:, e:e + 1], (LANE, LANE))
            local = jnp.sum(jnp.where(pos == j_row, t_col, 0.0), axis=0, keepdims=True)
            l_ref[e, pl.ds(c, 1), :] = local.astype(I32)
        cnt_ref[pl.ds(c, 1), :] = cnt_c.astype(I32)
        off_ref[pl.ds(c, 1), :] = run_sel.astype(I32)
        return run_eq + incl_eq[LANE - 1:LANE, :], run_sel + cnt_c

    zero = jnp.zeros((1, LANE), F32)
    lax.fori_loop(0, nc, chunk, (zero, zero))


def _select(affp, row0, n, cap):
    n_batch, nt, _ = affp.shape
    nc = n // LANE
    kern = functools.partial(_select_kernel, row0=row0, n=n, cap=cap)
    return pl.pallas_call(
        kern,
        grid=(n_batch,),
        in_specs=[pl.BlockSpec((None, nt, LANE), lambda b: (b, 0, 0))],
        out_specs=[pl.BlockSpec((None, N_EXPERTS, nc, LANE), lambda b: (b, 0, 0, 0)),
                   pl.BlockSpec((None, nc, LANE), lambda b: (b, 0, 0)),
                   pl.BlockSpec((None, nc, LANE), lambda b: (b, 0, 0))],
        out_shape=[jax.ShapeDtypeStruct((n_batch, N_EXPERTS, nc, LANE), I32),
                   jax.ShapeDtypeStruct((n_batch, nc, LANE), I32),
                   jax.ShapeDtypeStruct((n_batch, nc, LANE), I32)],
        scratch_shapes=[pltpu.VMEM((n, LANE), I32)],
        compiler_params=_params(("arbitrary",), 40),
        name="ec_select",
    )(affp)


MOE_ROWS = 256


def _moe_kernel(*refs, sets, d):
    ns = len(sets)
    tab_hbm = refs[0:3 * ns]
    xm_hbm, _, g2_ref, w1_ref, w3_ref, w2_ref, xs_hbm = refs[3 * ns:3 * ns + 7]
    tab_s = refs[3 * ns + 7:6 * ns + 7]
    idx_s, xg, rows, xb, hdn_s, sem = refs[6 * ns + 7:]
    total = sum(cap for _, _, cap, _ in sets)
    b = pl.program_id(0)
    e = pl.program_id(1)
    step = b * pl.num_programs(1) + e
    last_step = pl.num_programs(0) * pl.num_programs(1) - 1

    tables = []
    for k in range(ns):
        l_hbm, cnt_hbm, off_hbm = tab_hbm[3 * k:3 * k + 3]
        l_s, cnt_s, off_s = tab_s[3 * k:3 * k + 3]
        tables += [pltpu.make_async_copy(l_hbm.at[b, e], l_s, sem.at[3 + 3 * k]),
                   pltpu.make_async_copy(cnt_hbm.at[b], cnt_s, sem.at[4 + 3 * k]),
                   pltpu.make_async_copy(off_hbm.at[b], off_s, sem.at[5 + 3 * k])]
    for cp in tables:
        cp.start()
    for cp in tables:
        cp.wait()

    def x_gather(tok, slot, n_rows=1):
        return pltpu.make_async_copy(xm_hbm.at[b, pl.ds(tok, n_rows), :], xg.at[pl.ds(slot, n_rows), :], sem.at[0])

    def row_gather(tok, slot, n_rows=1):
        return pltpu.make_async_copy(xs_hbm.at[b, pl.ds(tok, n_rows), :], rows.at[pl.ds(slot, n_rows), :], sem.at[1])

    def row_scatter(tok, slot, n_rows=1):
        return pltpu.make_async_copy(rows.at[pl.ds(slot, n_rows), :], xs_hbm.at[b, pl.ds(tok, n_rows), :], sem.at[2])

    slot0 = 0
    for k, (row0, nc, cap, _) in enumerate(sets):
        l_s, cnt_s, off_s = tab_s[3 * k:3 * k + 3]

        def chunk(c, _, row0=row0, l_s=l_s, cnt_s=cnt_s, off_s=off_s, slot0=slot0):
            base = row0 + c * LANE
            off_c = slot0 + off_s[c, e]

            def one(j, _):
                idx_s[off_c + j] = base + l_s[c, j]
                return 0

            lax.fori_loop(0, cnt_s[c, e], one, 0)
            return 0

        lax.fori_loop(0, nc, chunk, 0)
        slot0 += cap

    def start_all(copy):
        def one(s, _):
            copy(idx_s[s], s).start()
            return 0

        lax.fori_loop(0, total, one, 0, unroll=8)

    start_all(x_gather)

    @pl.when(step > 0)
    def _():
        row_scatter(0, 0, total).wait()

    start_all(row_gather)
    x_gather(0, 0, total).wait()

    tn = _tile(d, 512)
    half = d // 2
    th = _tile(half, 512)
    lane = lax.broadcasted_iota(I32, (1, LANE), 1)

    def hidden(rs, size):
        for kc in range(half // th):
            ks = slice(kc * th, (kc + 1) * th)
            lo, hi = _unpack_bf16_pair(xg[rs, ks])
            xb[0:size, ks] = lo
            xb[0:size, half + kc * th:half + (kc + 1) * th] = hi
        h1 = _dot(xb[0:size, :], w1_ref[...])
        h3 = _dot(xb[0:size, :], w3_ref[...])
        aff = lax.bitcast_convert_type(xg[rs, half:half + LANE], F32)
        gate = jnp.sum(jnp.where(lane == e, aff, 0.0), axis=-1, keepdims=True)
        hdn_s[rs, :] = (h1 * jax.nn.sigmoid(h1) * h3 * gate).astype(BF16)

    def down(rs, g_row):
        g2_row = g_row if g_row is not None else b
        for c in range(d // tn):
            sl = slice(c * tn, (c + 1) * tn)
            rows[rs, sl] = rows[rs, sl] + g2_ref[pl.ds(g2_row, 1), sl] * _dot(hdn_s[rs, :], w2_ref[:, sl])

    def for_blocks(fn):
        slot0 = 0
        for _, _, cap, g_row in sets:
            rc = min(cap, MOE_ROWS)
            if cap // rc > 1:
                def body(r, _, slot0=slot0, rc=rc, g_row=g_row):
                    fn(pl.ds(pl.multiple_of(slot0 + r * rc, rc), rc), rc, g_row)
                    return 0

                lax.fori_loop(0, cap // rc, body, 0)
            else:
                fn(slice(slot0, slot0 + rc), rc, g_row)
            slot0 += cap

    for_blocks(lambda rs, size, g_row: hidden(rs, size))
    row_gather(0, 0, total).wait()
    for_blocks(lambda rs, size, g_row: down(rs, g_row))
    start_all(row_scatter)

    @pl.when(step == last_step)
    def _():
        row_scatter(0, 0, total).wait()


def _moe(token_sets, xma, xs, modl, w1, w3, w2, layer):
    n_batch, nt, d = xs.shape
    sets = tuple((row0, tabs[0].shape[2], cap, g_row) for tabs, row0, cap, g_row in token_sets)
    total = sum(cap for _, _, cap, _ in sets)
    for _, _, cap, _ in sets:
        assert cap % min(cap, MOE_ROWS) == 0 and cap % 8 == 0
    rc = min(max(cap for _, _, cap, _ in sets), MOE_ROWS)
    kern = functools.partial(_moe_kernel, sets=sets, d=d)
    any_ = pl.BlockSpec(memory_space=pl.ANY)
    tables = [t for tabs, _, _, _ in token_sets for t in tabs]
    smem_tables = [pltpu.SMEM((nc, LANE), I32) for _, nc, _, _ in sets for _ in range(3)]
    return pl.pallas_call(
        kern,
        grid=(n_batch, N_EXPERTS),
        in_specs=[any_] * len(tables) + [
            any_, any_,
            pl.BlockSpec((MOD_ROWS, d), lambda b, e: (0, 5)),
            pl.BlockSpec((None, None, d, EXPERT_FF), lambda b, e: (layer, e, 0, 0)),
            pl.BlockSpec((None, None, d, EXPERT_FF), lambda b, e: (layer, e, 0, 0)),
            pl.BlockSpec((None, None, EXPERT_FF, d), lambda b, e: (layer, e, 0, 0))],
        out_specs=any_,
        out_shape=jax.ShapeDtypeStruct(xs.shape, F32),
        scratch_shapes=smem_tables + [
            pltpu.SMEM((total,), I32), pltpu.VMEM((total, d // 2 + LANE), I32), pltpu.VMEM((total, d), F32),
            pltpu.VMEM((rc, d), BF16), pltpu.VMEM((total, EXPERT_FF), BF16),
            pltpu.SemaphoreType.DMA((3 + 3 * len(sets),))],
        input_output_aliases={len(tables) + 1: 0},
        compiler_params=_params(("arbitrary", "arbitrary"), 56),
        name="moe_ffn",
    )(*tables, xma, xs, modl, w1, w3, w2)


def _rope_tables(n, lc):
    t = jnp.arange(n)
    pos = jnp.stack([t // GRID_W, t % GRID_W], axis=-1).astype(F32)

    def tables(dh):
        quarter = dh // 4
        inv_freq = ROPE_THETA ** (-jnp.arange(quarter, dtype=F32) / quarter)
        ang = pos[:, :, None] * inv_freq
        cos, sin = jnp.cos(ang), jnp.sin(ang)
        c = jnp.concatenate([cos[:, 0], cos[:, 0], cos[:, 1], cos[:, 1]], axis=-1)
        s = jnp.concatenate([-sin[:, 0], sin[:, 0], -sin[:, 1], sin[:, 1]], axis=-1)
        reps = LANE // dh
        c, s = jnp.tile(c, (1, reps)), jnp.tile(s, (1, reps))
        return (jnp.concatenate([jnp.ones((lc, LANE), F32), c], axis=0),
                jnp.concatenate([jnp.zeros((lc, LANE), F32), s], axis=0))

    return tables(HEAD_DIM) + tables(DIFF_QK_DIM)


def kernel(x, c, ctx, c_ctx, w_ada, b_ada, norm_g, w_in, w_out, qk_g_na, na_rpb, qk_g_diff, diff_lambda,
           diff_subln_g, qk_g_swa, swa_sink, lru_conv_w, lru_conv_b, lru_gate_w, lru_gate_b, lru_lam,
           w_router, w1, w3, w2):
    n_batch, n, d = x.shape
    lc = ctx.shape[1]
    depth = w_ada.shape[0]
    assert n_batch < MOD_ROWS and w_in.shape[2] == D_IN

    cvec = jnp.zeros((MOD_ROWS, d), F32).at[:n_batch].set(c).at[n_batch].set(c_ctx)
    mod = _ada(cvec, w_ada, b_ada)
    cs, ss, cd, sd = _rope_tables(n, lc)
    na_bias = jax.vmap(lambda rpb: _na_bias_tables(rpb, n))(na_rpb)
    xs = jnp.concatenate([ctx, x], axis=1)

    cap_lat = CAPACITY_FACTOR * n // N_EXPERTS
    cap_ctx = CAPACITY_FACTOR * lc // N_EXPERTS
    w_in_b, w_out_b = w_in.astype(BF16), w_out.astype(BF16)
    w1b, w3b, w2b = w1.astype(BF16), w3.astype(BF16), w2.astype(BF16)

    for l in range(depth):
        modl = mod[l]
        u_att, u_lru = _inproj(xs, norm_g[l, 0].reshape(1, d), modl, w_in_b, l, lc)

        pvec = jnp.concatenate([qk_g_na[l], jnp.tile(qk_g_diff[l], (1, 2)), qk_g_swa[l],
                                diff_subln_g[l].reshape(1, LANE), jnp.zeros((1, LANE), F32)], axis=0)
        qa, ka, va, qd, kd, vd, qs, ks, vs = _prep(u_att, pvec, cs, ss, cd, sd)

        gmax = lambda g: jnp.max(jnp.abs(g[0])) * jnp.max(jnp.abs(g[1]))
        b_na = (BOUND_SLACK * HEAD_DIM * ATTN_Q_SCALE * gmax(qk_g_na[l])
                + LOG2E * jnp.max(jnp.abs(na_rpb[l])))
        b_swa = jnp.maximum(BOUND_SLACK * HEAD_DIM * ATTN_Q_SCALE * gmax(qk_g_swa[l]),
                            LOG2E * jnp.max(jnp.abs(swa_sink[l])))
        b_diff = BOUND_SLACK * DIFF_QK_DIM * DIFF_Q_SCALE * gmax(qk_g_diff[l])

        o_a = _na(b_na, qa, ka, va, na_bias, l, lc)

        lambda_init = 0.8 - 0.6 * math.exp(-0.3 * l)
        dl = diff_lambda[l].astype(F32)
        lam = jnp.exp(jnp.sum(dl[0] * dl[1])) - jnp.exp(jnp.sum(dl[2] * dl[3])) + lambda_init
        o_d = _diff(jnp.stack([lam, b_diff]).astype(F32), qd, kd, vd, pvec, lc, 1.0 - lambda_init)

        o_s = _swa(b_swa, qs, ks, vs, swa_sink[l], lc)

        af, bf, ab, bb = _lru_gates(u_lru, lru_conv_w[l], lru_conv_b[l], lru_gate_w[l], lru_gate_b[l], lru_lam[l], lc)
        hf, hb = _lru_scan(af, bf, ab, bb, lc)

        xs = _outproj(o_a, o_d, o_s, hf, hb, u_lru, modl, w_out_b, l, xs, lc)

        wr_pad = jnp.zeros((d, LANE), F32).at[:, :N_EXPERTS].set(w_router[l])
        xma, affp = _router(xs, norm_g[l, 1].reshape(1, d), modl, wr_pad, lc)
        token_sets = [(_select(affp, lc, n, cap_lat), lc, cap_lat, None)]
        if l < depth - 1:
            token_sets.append((_select(affp, 0, lc, cap_ctx), 0, cap_ctx, n_batch))
        xs = _moe(tuple(token_sets), xma, xs, modl, w1b, w3b, w2b, l)
    return xs[:, lc:, :]
```
